```python
import jax
import jax.numpy as jnp
from jax import lax
import numpy as np

D_MODEL = 2048
BATCH = 4
SEQ = 2048
DEPTH = 4
DEC_BATCH = 8
DEC_SEQ = 1
PAST_LEN = 16384
PAGE_SIZE = 128

N_MIXERS = 3
EPS = 1e-6
MASK_VALUE = -1e30
F_FLOOR = 1e-30
HG_DK = 128
HG_HEADS = D_MODEL // HG_DK
HG_DV = D_MODEL // HG_HEADS
HG_FDIM = HG_HEADS * HG_DK
HG_CHUNK = 64
CONV_WIDTH = 3
SW_GROUPS = ((128, 1), (512, 4), (2048, 16))
SW_HEADS = 8
SW_HEAD_DIM = 128
SW_WIDTH = SW_HEADS * SW_HEAD_DIM
ROPE_THETA = 10000.0
MOE_GROUPS = 4
MOE_PER_GROUP = 8
MOE_EXPERTS = MOE_GROUPS * MOE_PER_GROUP
MOE_TOP_K = 2
MOE_FF = 1024
MOE_BLOCK = 128
N_A = len(range(0, DEPTH, N_MIXERS))
N_B = len(range(1, DEPTH, N_MIXERS))
N_C = len(range(2, DEPTH, N_MIXERS))

kernel_name = 'hybrid_hgrn2_conv_dilswa_hmoe_step'


def rmsnorm(x, g):
    xf = x.astype(jnp.float32)
    inv = lax.rsqrt(jnp.mean(xf * xf, axis=-1, keepdims=True) + EPS)
    return (xf * inv).astype(x.dtype) * g


def rope(x, pos):
    half = x.shape[-1] // 2
    inv_freq = ROPE_THETA ** (-jnp.arange(half, dtype=jnp.float32) / half)
    ang = pos.astype(jnp.float32)[:, None] * inv_freq[None, :]
    cos = jnp.cos(ang)[None, :, None, :].astype(x.dtype)
    sin = jnp.sin(ang)[None, :, None, :].astype(x.dtype)
    x1, x2 = x[..., :half], x[..., half:]
    return jnp.concatenate([x1 * cos - x2 * sin, x1 * sin + x2 * cos], axis=-1)


def gla_chunked(q, k, v, logf, s0):
    b, t, h, _ = q.shape
    dv = v.shape[-1]
    c = min(HG_CHUNK, t)
    n = -(-t // c)
    pad = n * c - t

    def prep(a):
        a = jnp.pad(a.astype(jnp.float32), ((0, 0), (0, pad), (0, 0), (0, 0)))
        return a.reshape(b, n, c, h, a.shape[-1]).transpose(1, 0, 3, 2, 4)

    qc, kc, vc, gc = prep(q), prep(k), prep(v), prep(logf)
    gcum = jnp.cumsum(gc, axis=3)
    causal = jnp.tril(jnp.ones((c, c), dtype=bool))[:, :, None]

    def step(s, inp):
        qi, ki, vi, gi = inp
        glast = gi[:, :, -1:, :]
        diff = gi[:, :, :, None, :] - gi[:, :, None, :, :]
        decay = jnp.where(causal, jnp.exp(jnp.where(causal, diff, 0.0)), 0.0)
        att = jnp.einsum('bhtk,bhsk,bhtsk->bhts', qi, ki, decay)
        o = (jnp.einsum('bhtk,bhkv->bhtv', qi * jnp.exp(gi), s)
             + jnp.einsum('bhts,bhsv->bhtv', att, vi))
        s_new = (jnp.exp(glast[:, :, 0, :, None]) * s
                 + jnp.einsum('bhsk,bhsv->bhkv', ki * jnp.exp(glast - gi), vi))
        return s_new, o

    s_t, o = lax.scan(step, s0.astype(jnp.float32), (qc, kc, vc, gcum))
    o = o.transpose(1, 0, 3, 2, 4).reshape(b, n * c, h, dv)[:, :t]
    return o, s_t.astype(s0.dtype)


def hgrn2_mixer(h, s0, w_in, w_out, g_norm, lb):
    b, t, _ = h.shape
    qp, fp, ip, gp = jnp.split(h @ w_in, [HG_FDIM, 2 * HG_FDIM, 2 * HG_FDIM + D_MODEL], axis=-1)
    q = jax.nn.silu(qp).reshape(b, t, HG_HEADS, HG_DK)
    fpre = fp.astype(jnp.float32).reshape(b, t, HG_HEADS, HG_DK)
    lbh = lb.astype(jnp.float32).reshape(HG_HEADS, HG_DK)
    f = lbh + (1.0 - lbh) * jax.nn.sigmoid(fpre)
    logf = jnp.log(jnp.maximum(f, F_FLOOR))
    k = (1.0 - lbh) * jax.nn.sigmoid(-fpre)
    iv = ip.reshape(b, t, HG_HEADS, HG_DV)
    o, s_t = gla_chunked(q, k, iv, logf, s0)
    o = rmsnorm(o.astype(h.dtype), g_norm.reshape(HG_HEADS, HG_DV)).reshape(b, t, D_MODEL)
    return (o * jax.nn.silu(gp)) @ w_out, s_t


def conv_mixer(h, buf, w_in, w_conv, w_out):
    t = h.shape[1]
    bg, cg, u = jnp.split(h @ w_in, 3, axis=-1)
    z = cg * u
    zp = jnp.concatenate([buf.astype(z.dtype), z], axis=1)
    y = zp[:, 0:t] * w_conv[0]
    for j in range(1, CONV_WIDTH):
        y = y + zp[:, j:j + t] * w_conv[j]
    return (bg * y) @ w_out, zp[:, -(CONV_WIDTH - 1):]


def dilated_prompt(q, k, v, dil, band):
    b, t, h, dh = q.shape
    blk = band
    tc = -(-t // dil)
    nb = -(-tc // blk)
    tcp = nb * blk
    pad = tcp * dil - t

    def fold(a):
        a = jnp.pad(a, ((0, 0), (0, pad), (0, 0), (0, 0)))
        return a.reshape(b, tcp, dil, h, dh).transpose(0, 2, 1, 3, 4)

    def band_keys(a):
        ap = jnp.pad(a, ((0, 0), (0, 0), (blk, 0), (0, 0), (0, 0)))
        prev = ap[:, :, :tcp].reshape(b, dil, nb, blk, h, dh)
        cur = a.reshape(b, dil, nb, blk, h, dh)
        return jnp.concatenate([prev, cur], axis=3)

    qb = fold(q).reshape(b, dil, nb, blk, h, dh)
    kb, vb = band_keys(fold(k)), band_keys(fold(v))
    s = jnp.einsum('brnqhd,brnkhd->brnhqk', qb, kb).astype(jnp.float32) * (dh ** -0.5)
    qi = jnp.arange(blk)[:, None] + blk
    ki = jnp.arange(2 * blk)[None, :]
    dist = qi - ki
    kidx = (jnp.arange(nb)[:, None, None] - 1) * blk + ki[None]
    mask = ((dist >= 0) & (dist <= band))[None] & (kidx >= 0)
    s = jnp.where(mask[None, None, :, None], s, MASK_VALUE)
    lse = jax.nn.logsumexp(s, axis=-1)
    p = jnp.where(mask[None, None, :, None], jnp.exp(s - lse[..., None]), 0.0).astype(v.dtype)
    o = jnp.einsum('brnhqk,brnkhd->brnqhd', p, vb)
    o = o.reshape(b, dil, tcp, h, dh).transpose(0, 2, 1, 3, 4).reshape(b, tcp * dil, h, dh)[:, :t]
    lse = lse.transpose(0, 1, 2, 4, 3).reshape(b, dil, tcp, h).transpose(0, 2, 1, 3).reshape(b, tcp * dil, h)[:, :t]
    return o, lse


def dilated_sample(q, k, v, buf, dil, band):
    b, t, h, dh = q.shape
    wb = buf.shape[1]
    kv_all = jnp.concatenate([buf.astype(k.dtype), jnp.stack([k, v], axis=2)], axis=1)
    idx = wb + jnp.arange(t)[:, None] - dil * jnp.arange(band + 1)[None, :]
    valid = (idx >= 0)[None, None]
    g = kv_all[:, jnp.maximum(idx, 0)]
    s = jnp.einsum('bqhd,bqjhd->bhqj', q, g[:, :, :, 0]).astype(jnp.float32) * (dh ** -0.5)
    s = jnp.where(valid, s, MASK_VALUE)
    lse = jax.nn.logsumexp(s, axis=-1)
    p = jnp.where(valid, jnp.exp(s - lse[..., None]), 0.0).astype(v.dtype)
    o = jnp.einsum('bhqj,bqjhd->bqhd', p, g[:, :, :, 1])
    return o, lse.transpose(0, 2, 1), kv_all[:, -wb:]


def swa_mixer(h, pos, caches, w_in, w_out):
    b, t, _ = h.shape
    qkv = (h @ w_in).reshape(b, t, len(SW_GROUPS), 3, SW_HEADS, SW_HEAD_DIM)
    outs, lses, new_bufs = [], [], []
    for gi, (win, dil) in enumerate(SW_GROUPS):
        q = rope(qkv[:, :, gi, 0], pos)
        k = rope(qkv[:, :, gi, 1], pos)
        v = qkv[:, :, gi, 2]
        if caches is None:
            o, lse = dilated_prompt(q, k, v, dil, win // dil)
            buf = jnp.stack([k, v], axis=2)[:, -min(win, t):]
        else:
            o, lse, buf = dilated_sample(q, k, v, caches[gi], dil, win // dil)
        outs.append(o)
        lses.append(lse)
        new_bufs.append(buf)
    wts = jax.nn.softmax(jnp.stack(lses, axis=0), axis=0).astype(h.dtype)
    o = jnp.einsum('gbth,gbthd->bthd', wts, jnp.stack(outs, axis=0))
    return o.reshape(b, t, SW_WIDTH) @ w_out, new_bufs


def routed_experts(xt, expert, gate, w_gu, w_down):
    n_assign = expert.shape[0]
    tok = jnp.arange(n_assign) // MOE_TOP_K
    blk = min(MOE_BLOCK, n_assign)
    n_blocks = -(-n_assign // blk) + MOE_EXPERTS
    order = jnp.argsort(expert)
    e_sorted = expert[order]
    counts = jnp.bincount(expert, length=MOE_EXPERTS)
    start = jnp.cumsum(counts) - counts
    padded = (counts + blk - 1) // blk * blk
    pend = jnp.cumsum(padded)
    pstart = pend - padded
    dest = pstart[e_sorted] + jnp.arange(n_assign) - start[e_sorted]
    rows = n_blocks * blk
    row_tok = jnp.zeros((rows,), jnp.int32).at[dest].set(tok[order].astype(jnp.int32))
    row_gate = jnp.zeros((rows,), xt.dtype).at[dest].set(gate[order])
    blk_expert = jnp.minimum(jnp.searchsorted(pend, jnp.arange(n_blocks) * blk, side='right'), MOE_EXPERTS - 1)
    xb = xt[row_tok].reshape(n_blocks, blk, xt.shape[-1])

    def expert_block(args):
        xi, e = args
        a, u = jnp.split(xi @ w_gu[e], 2, axis=-1)
        return (jax.nn.silu(a) * u) @ w_down[e]

    out = lax.map(expert_block, (xb, blk_expert))
    return jnp.zeros_like(xt).at[row_tok].add(out.reshape(rows, -1) * row_gate[:, None])


def hier_moe(h, w_group, b_group, w_router, b_router, w_gu, w_down):
    b, t, d = h.shape
    xt = h.reshape(b * t, d)
    glog = (xt @ w_group + b_group).astype(jnp.float32)
    gprob = jax.nn.softmax(glog, axis=-1)
    gsel = jnp.argmax(glog, axis=-1)
    elog = (xt @ w_router + b_router).astype(jnp.float32).reshape(b * t, MOE_GROUPS, MOE_PER_GROUP)
    elog_g = jnp.take_along_axis(elog, gsel[:, None, None], axis=1)[:, 0]
    top_v, top_i = lax.top_k(elog_g, MOE_TOP_K)
    gate = jnp.take_along_axis(gprob, gsel[:, None], axis=1) * jax.nn.softmax(top_v, axis=-1)
    expert = gsel[:, None] * MOE_PER_GROUP + top_i
    y = routed_experts(xt, expert.reshape(-1), gate.reshape(-1).astype(h.dtype), w_gu, w_down)
    return y.reshape(b, t, d)


def run_trunk(x, c, pos, st_hgrn, st_conv, swa_cache, ada_w, ada_b, norm_mix, norm_ffn, norm_final,
              hg_w_in, hg_w_out, hg_norm, hg_lower, cv_w_in, cv_w_conv, cv_w_out, sw_w_in, sw_w_out,
              moe_w_group, moe_b_group, moe_w_router, moe_b_router, moe_w_gu, moe_w_down):
    sm = jax.nn.softmax(hg_lower.astype(jnp.float32), axis=0)
    lower = jnp.cumsum(sm, axis=0) - sm[0]
    new_hgrn, new_conv = [], []
    new_swa = [[] for _ in SW_GROUPS]
    cond = jax.nn.silu(c)
    for i in range(DEPTH):
        kind, j = i % N_MIXERS, i // N_MIXERS
        mod = (cond @ ada_w[i] + ada_b[i])[:, None, :]
        sh1, sc1, g1, sh2, sc2, g2 = jnp.split(mod, 6, axis=-1)
        hmix = rmsnorm(x, norm_mix[i]) * (1 + sc1) + sh1
        if kind == 0:
            m, s_new = hgrn2_mixer(hmix, st_hgrn[j], hg_w_in[j], hg_w_out[j], hg_norm[j], lower[j])
            new_hgrn.append(s_new)
        elif kind == 1:
            m, buf = conv_mixer(hmix, st_conv[j], cv_w_in[j], cv_w_conv[j], cv_w_out[j])
            new_conv.append(buf)
        else:
            grp_cache = None if swa_cache is None else (swa_cache[0][j], swa_cache[1][j], swa_cache[2][j])
            m, bufs = swa_mixer(hmix, pos, grp_cache, sw_w_in[j], sw_w_out[j])
            for gi in range(len(SW_GROUPS)):
                new_swa[gi].append(bufs[gi])
        x = x + g1 * m
        hffn = rmsnorm(x, norm_ffn[i]) * (1 + sc2) + sh2
        x = x + g2 * hier_moe(hffn, moe_w_group[i], moe_b_group[i], moe_w_router[i], moe_b_router[i],
                              moe_w_gu[i], moe_w_down[i])
    y = rmsnorm(x, norm_final)
    return y, jnp.stack(new_hgrn), jnp.stack(new_conv), [jnp.stack(l) for l in new_swa]


def setup_inputs(seed: int = 0) -> dict:
    key = jax.random.key(seed)
    ks = iter(jax.random.split(key, 40))
    d = D_MODEL

    def nrm(shape, scale):
        return scale * jax.random.normal(next(ks), shape, jnp.float32)

    def swa_cache(win):
        return nrm((N_C, DEC_BATCH, min(win, PAST_LEN), 2, SW_HEADS, SW_HEAD_DIM), 1.0)

    return {
        'x_prompt': nrm((BATCH, SEQ, d), 1.0),
        'x_sample': nrm((DEC_BATCH, DEC_SEQ, d), 1.0),
        'state_hgrn': nrm((N_A, DEC_BATCH, HG_HEADS, HG_DK, HG_DV), 0.5),
        'state_conv': nrm((N_B, DEC_BATCH, CONV_WIDTH - 1, d), 1.0),
        'cache_swa_g1': swa_cache(SW_GROUPS[0][0]),
        'cache_swa_g2': swa_cache(SW_GROUPS[1][0]),
        'cache_swa_g3': swa_cache(SW_GROUPS[2][0]),
        'c_prompt': nrm((BATCH, d), 1.0),
        'c_sample': nrm((DEC_BATCH, d), 1.0),
        'ada_w': nrm((DEPTH, d, 6 * d), 0.5 * d ** -0.5),
        'ada_b': nrm((DEPTH, 6 * d), 0.02),
        'norm_mix': 1.0 + nrm((DEPTH, d), 0.05),
        'norm_ffn': 1.0 + nrm((DEPTH, d), 0.05),
        'norm_final': 1.0 + nrm((d,), 0.05),
        'hg_w_in': nrm((N_A, d, 2 * HG_FDIM + 2 * d), d ** -0.5),
        'hg_w_out': nrm((N_A, d, d), d ** -0.5),
        'hg_norm': 1.0 + nrm((N_A, d), 0.05),
        'hg_lower': nrm((N_A, HG_FDIM), 0.1),
        'cv_w_in': nrm((N_B, d, 3 * d), d ** -0.5),
        'cv_w_conv': nrm((N_B, CONV_WIDTH, d), CONV_WIDTH ** -0.5),
        'cv_w_out': nrm((N_B, d, d), d ** -0.5),
        'sw_w_in': nrm((N_C, d, len(SW_GROUPS) * 3 * SW_WIDTH), d ** -0.5),
        'sw_w_out': nrm((N_C, SW_WIDTH, d), SW_WIDTH ** -0.5),
        'moe_w_group': nrm((DEPTH, d, MOE_GROUPS), d ** -0.5),
        'moe_b_group': nrm((DEPTH, MOE_GROUPS), 0.01),
        'moe_w_router': nrm((DEPTH, d, MOE_EXPERTS), d ** -0.5),
        'moe_b_router': nrm((DEPTH, MOE_EXPERTS), 0.01),
        'moe_w_gu': nrm((DEPTH, MOE_EXPERTS, d, 2 * MOE_FF), d ** -0.5),
        'moe_w_down': nrm((DEPTH, MOE_EXPERTS, MOE_FF, d), MOE_FF ** -0.5),
    }


def reference(x_prompt, x_sample, state_hgrn, state_conv, cache_swa_g1, cache_swa_g2, cache_swa_g3,
              c_prompt, c_sample, ada_w, ada_b, norm_mix, norm_ffn, norm_final,
              hg_w_in, hg_w_out, hg_norm, hg_lower, cv_w_in, cv_w_conv, cv_w_out, sw_w_in, sw_w_out,
              moe_w_group, moe_b_group, moe_w_router, moe_b_router, moe_w_gu, moe_w_down):
    weights = (ada_w, ada_b, norm_mix, norm_ffn, norm_final, hg_w_in, hg_w_out, hg_norm, hg_lower,
               cv_w_in, cv_w_conv, cv_w_out, sw_w_in, sw_w_out,
               moe_w_group, moe_b_group, moe_w_router, moe_b_router, moe_w_gu, moe_w_down)
    bp = x_prompt.shape[0]
    pos_p = jnp.arange(x_prompt.shape[1])
    pos_s = PAST_LEN + jnp.arange(x_sample.shape[1])
    zero_hgrn = jnp.zeros((state_hgrn.shape[0], bp) + state_hgrn.shape[2:], x_prompt.dtype)
    zero_conv = jnp.zeros((state_conv.shape[0], bp) + state_conv.shape[2:], x_prompt.dtype)
    y_p, hg_p, cv_p, sw_p = run_trunk(x_prompt, c_prompt, pos_p, zero_hgrn, zero_conv, None, *weights)
    y_s, hg_s, cv_s, sw_s = run_trunk(x_sample, c_sample, pos_s, state_hgrn, state_conv,
                                      (cache_swa_g1, cache_swa_g2, cache_swa_g3), *weights)
    return (y_p, y_s, hg_p, hg_s, cv_p, cv_s, sw_p[0], sw_s[0], sw_p[1], sw_s[1], sw_p[2], sw_s[2])
```

```python
import functools

import jax
import jax.numpy as jnp
from jax import lax
from jax.experimental import pallas as pl
from jax.experimental.pallas import tpu as pltpu

F32 = jnp.float32
BF16 = jnp.bfloat16

D_MODEL = 2048
DEPTH = 4
N_MIXERS = 3
EPS = 1e-6
MASK_VALUE = -1e30
F_FLOOR = 1e-30
HG_DK = 128
HG_HEADS = D_MODEL // HG_DK
HG_CHUNK = 64
HG_SUB = 16
CONV_WIDTH = 3
SW_GROUPS = ((128, 1), (512, 4), (2048, 16))
SW_HEADS = 8
SW_HEAD_DIM = 128
SW_WIDTH = SW_HEADS * SW_HEAD_DIM
SW_BAND = 128
ROPE_THETA = 10000.0
MOE_GROUPS = 4
MOE_PER_GROUP = 8
MOE_EXPERTS = MOE_GROUPS * MOE_PER_GROUP
MOE_TOP_K = 2
MOE_FF = 1024
PAST_LEN = 16384

LANES = 128
SUBLANES = 8
VMEM_LIMIT = 52 * 1024 * 1024
MOE_ROWS = 256
MOE_FFC = 512
HIGHEST = lax.Precision.HIGHEST


def _cparams(sem):
    return pltpu.CompilerParams(dimension_semantics=sem, vmem_limit_bytes=VMEM_LIMIT)


def _sigmoid(x):
    return 1.0 / (1.0 + jnp.exp(-x))


def _silu(x):
    return x * _sigmoid(x)


def _ada_kernel(c_ref, w_ref, b_ref, o_ref):
    c = c_ref[...]
    cond = _silu(c).astype(BF16)
    o_ref[...] = jnp.dot(cond, w_ref[...].astype(BF16), preferred_element_type=F32) + b_ref[...]


def _ada_mod(c_all, ada_w, ada_b):
    rows = c_all.shape[0]
    n = ada_w.shape[-1]
    tn = 1024
    return pl.pallas_call(
        _ada_kernel,
        grid=(DEPTH, n // tn),
        in_specs=[
            pl.BlockSpec((rows, D_MODEL), lambda l, j: (0, 0)),
            pl.BlockSpec((None, D_MODEL, tn), lambda l, j: (l, 0, j)),
            pl.BlockSpec((None, 1, tn), lambda l, j: (l, 0, j)),
        ],
        out_specs=pl.BlockSpec((None, rows, tn), lambda l, j: (l, 0, j)),
        out_shape=jax.ShapeDtypeStruct((DEPTH, rows, n), F32),
        compiler_params=_cparams(("arbitrary", "arbitrary")),
        name="ada_mod",
    )(c_all, ada_w, ada_b.reshape(DEPTH, 1, n))


def _norm_mod_rows(x, gn, sc, sh):
    inv = lax.rsqrt(jnp.mean(x * x, axis=-1, keepdims=True) + EPS)
    return (x * inv) * gn * (1.0 + sc) + sh


def _norm_mod_into(x_ref, gn_ref, sc_ref, sh_ref, dst_refs, tm):
    ch = min(tm, 128)
    per_row = sc_ref.shape[0] != 1

    def body(c, carry):
        rs = pl.ds(pl.multiple_of(c * ch, ch), ch)
        sc = sc_ref[rs, :] if per_row else sc_ref[...]
        sh = sh_ref[rs, :] if per_row else sh_ref[...]
        h = _norm_mod_rows(x_ref[rs, :], gn_ref[...], sc, sh)
        for d in dst_refs:
            d[rs, :] = h.astype(d.dtype)
        return carry

    lax.fori_loop(0, tm // ch, body, 0)


def _mod_spec(mod, col, tm, rows_per_batch):
    r = mod.shape[1]
    bpb = max(rows_per_batch // tm, 1)
    return pl.BlockSpec((None, r, D_MODEL), lambda i, j: (i // bpb, 0, col))


def _rope_tile(acc, cos, sin_signed):
    outs = []
    for h in range(acc.shape[1] // SW_HEAD_DIM):
        xh = acc[:, h * SW_HEAD_DIM:(h + 1) * SW_HEAD_DIM]
        outs.append(xh * cos + pltpu.roll(xh, SW_HEAD_DIM // 2, 1) * sin_signed)
    return jnp.concatenate(outs, axis=1)


def _proj_kernel(x_ref, gn_ref, sc_ref, sh_ref, w_ref, *rest, tm, rope):
    if rope:
        cos_ref, sin_ref, o_ref, xb_ref = rest
    else:
        o_ref, xb_ref = rest
    j = pl.program_id(1)

    @pl.when(j == 0)
    def _():
        _norm_mod_into(x_ref, gn_ref, sc_ref, sh_ref, [xb_ref], tm)

    acc = jnp.dot(xb_ref[...], w_ref[...].astype(BF16), preferred_element_type=F32)
    if not rope:
        o_ref[...] = acc
    else:
        is_qk = (j % 3) < 2

        @pl.when(is_qk)
        def _():
            o_ref[...] = _rope_tile(acc, cos_ref[...], sin_ref[...])

        @pl.when(jnp.logical_not(is_qk))
        def _():
            o_ref[...] = acc


def _proj(x, gn, mod, shift_col, scale_col, w, *, tm, tn, rows_per_batch, rope=None):
    m, n = x.shape[0], w.shape[1]
    in_specs = [
        pl.BlockSpec((tm, D_MODEL), lambda i, j: (i, 0)),
        pl.BlockSpec((1, D_MODEL), lambda i, j: (0, 0)),
        _mod_spec(mod, scale_col, tm, rows_per_batch),
        _mod_spec(mod, shift_col, tm, rows_per_batch),
        pl.BlockSpec((D_MODEL, tn), lambda i, j: (0, j)),
    ]
    args = [x, gn.reshape(1, D_MODEL), mod, mod, w]
    if rope is not None:
        cos, sin_signed = rope
        nblk = cos.shape[0] // tm
        in_specs += [pl.BlockSpec((tm, SW_HEAD_DIM), lambda i, j: (i % nblk, 0))] * 2
        args += [cos, sin_signed]
    return pl.pallas_call(
        functools.partial(_proj_kernel, tm=tm, rope=rope is not None),
        grid=(m // tm, n // tn),
        in_specs=in_specs,
        out_specs=pl.BlockSpec((tm, tn), lambda i, j: (i, j)),
        out_shape=jax.ShapeDtypeStruct((m, n), F32),
        scratch_shapes=[pltpu.VMEM((tm, D_MODEL), BF16)],
        compiler_params=_cparams(("arbitrary", "arbitrary")),
        name="norm_proj",
    )(*args)


def _outproj_kernel(l_ref, w_ref, res_ref, g_ref, o_ref):
    acc = jnp.dot(l_ref[...].astype(BF16), w_ref[...].astype(BF16), preferred_element_type=F32)
    o_ref[...] = res_ref[...] + g_ref[...] * acc


def _outproj(lhs, w, res, mod, gate_col, *, tm, tn, rows_per_batch):
    m, k = lhs.shape
    n = w.shape[1]
    return pl.pallas_call(
        _outproj_kernel,
        grid=(m // tm, n // tn),
        in_specs=[
            pl.BlockSpec((tm, k), lambda i, j: (i, 0)),
            pl.BlockSpec((k, tn), lambda i, j: (0, j)),
            pl.BlockSpec((tm, tn), lambda i, j: (i, j)),
            _mod_spec_n(mod, gate_col, tm, tn, rows_per_batch),
        ],
        out_specs=pl.BlockSpec((tm, tn), lambda i, j: (i, j)),
        out_shape=jax.ShapeDtypeStruct((m, n), F32),
        compiler_params=_cparams(("arbitrary", "arbitrary")),
        name="out_proj",
    )(lhs, w, res, mod)


def _mod_spec_n(mod, col, tm, tn, rows_per_batch):
    r = mod.shape[1]
    bpb = max(rows_per_batch // tm, 1)
    per = D_MODEL // tn
    return pl.BlockSpec((None, r, tn), lambda i, j: (i // bpb, 0, col * per + j))


def _hgrn_gates(qp, fp, lb):
    q = _silu(qp)
    f = lb + (1.0 - lb) * _sigmoid(fp)
    logf = jnp.log(jnp.maximum(f, F_FLOOR))
    k = (1.0 - lb) * _sigmoid(-fp)
    return q, logf, k


def _head_norm_gate(o, gn, gp):
    inv = lax.rsqrt(jnp.mean(o * o, axis=-1, keepdims=True) + EPS)
    return (o * inv) * gn * _silu(gp)


def _gla_kernel(q_ref, f_ref, i_ref, g_ref, lb_ref, gn_ref, o_ref, s_out_ref, st_ref, *, n_chunks):
    c_len, n_sub = HG_CHUNK, HG_CHUNK // HG_SUB
    t = pl.program_id(2)

    @pl.when(t == 0)
    def _():
        st_ref[...] = jnp.zeros_like(st_ref)

    lb = lb_ref[...]
    gn = gn_ref[...]
    row = lax.broadcasted_iota(jnp.int32, (c_len, HG_DK), 0)
    r4 = lax.broadcasted_iota(jnp.int32, (c_len, n_sub * c_len), 0)
    c4 = lax.broadcasted_iota(jnp.int32, (c_len, n_sub * c_len), 1)
    mask4 = ((r4 // HG_SUB) == (c4 // c_len)) & ((c4 % c_len) <= r4)
    nt = (((1,), (1,)), ((), ()))
    tn = (((0,), (0,)), ((), ()))

    for c in range(n_chunks):
        rs = pl.ds(c * c_len, c_len)
        q, logf, k = _hgrn_gates(q_ref[rs, :], f_ref[rs, :], lb)
        v = i_ref[rs, :]
        g = logf
        sh = 1
        while sh < c_len:
            g = g + jnp.where(row >= sh, pltpu.roll(g, sh, 0), 0.0)
            sh *= 2
        refs = [g[HG_SUB * i + HG_SUB // 2 - 1:HG_SUB * i + HG_SUB // 2, :] for i in range(n_sub)]
        mrows = jnp.concatenate([jnp.broadcast_to(r, (HG_SUB, HG_DK)) for r in refs], axis=0)
        qh = (q * jnp.exp(g - mrows)).astype(BF16)
        khs = [k * jnp.exp(jnp.where(row < HG_SUB * (i + 1), refs[i] - g, -jnp.inf)) for i in range(n_sub)]
        kh = jnp.concatenate(khs, axis=0).astype(BF16)
        a4 = lax.dot_general(qh, kh, nt, preferred_element_type=F32)
        a4 = jnp.where(mask4, a4, 0.0).astype(BF16)
        vb = v.astype(BF16)
        v4 = jnp.concatenate([vb] * n_sub, axis=0)
        st = st_ref[...]
        o = jnp.dot(a4, v4, preferred_element_type=F32)
        o = o + lax.dot_general((q * jnp.exp(g)).astype(BF16), st.astype(BF16), nt,
                                preferred_element_type=F32)
        glast = g[c_len - 1:c_len, :]
        kd = (k * jnp.exp(glast - g)).astype(BF16)
        st_ref[...] = st * jnp.exp(glast) + lax.dot_general(vb, kd, tn, preferred_element_type=F32)
        o_ref[rs, :] = _head_norm_gate(o, gn, g_ref[rs, :]).astype(o_ref.dtype)

    @pl.when(t == pl.num_programs(2) - 1)
    def _():
        s_out_ref[...] = st_ref[...].T


def _gla_prompt(proj, lower, g_norm, batch, seq):
    tb = 512
    nt = seq // tb
    h = HG_HEADS

    def col(off):
        return pl.BlockSpec((tb, HG_DK), lambda b, hh, t: (b * nt + t, off * h + hh))

    return pl.pallas_call(
        functools.partial(_gla_kernel, n_chunks=tb // HG_CHUNK),
        grid=(batch, h, nt),
        in_specs=[col(0), col(1), col(2), col(3),
                  pl.BlockSpec((1, HG_DK), lambda b, hh, t: (0, hh)),
                  pl.BlockSpec((1, HG_DK), lambda b, hh, t: (0, hh))],
        out_specs=[pl.BlockSpec((tb, HG_DK), lambda b, hh, t: (b * nt + t, hh)),
                   pl.BlockSpec((None, None, HG_DK, HG_DK), lambda b, hh, t: (b, hh, 0, 0))],
        out_shape=[jax.ShapeDtypeStruct((batch * seq, D_MODEL), BF16),
                   jax.ShapeDtypeStruct((batch, h, HG_DK, HG_DK), F32)],
        scratch_shapes=[pltpu.VMEM((HG_DK, HG_DK), F32)],
        compiler_params=_cparams(("arbitrary", "arbitrary", "arbitrary")),
        name="hgrn_prompt",
    )(proj, proj, proj, proj, lower.reshape(1, D_MODEL), g_norm.reshape(1, D_MODEL))


def _gla_step_kernel(qt_ref, ft_ref, i_ref, g_ref, lbt_ref, gn_ref, s_ref, o_ref, s_out_ref):
    for h in range(HG_HEADS):
        cs = slice(h * HG_DK, (h + 1) * HG_DK)
        q, logf, k = _hgrn_gates(qt_ref[:, h:h + 1], ft_ref[:, h:h + 1], lbt_ref[:, h:h + 1])
        s_new = jnp.exp(logf) * s_ref[h] + k * i_ref[:, cs]
        s_out_ref[h] = s_new
        o = jnp.sum(s_new * q, axis=0, keepdims=True)
        o_ref[:, cs] = _head_norm_gate(o, gn_ref[:, cs], g_ref[:, cs])


def _gla_step(proj, lower, g_norm, state):
    b = proj.shape[0]
    h = HG_HEADS
    pq = proj[:, :D_MODEL].reshape(b, h, HG_DK).transpose(0, 2, 1)
    pf = proj[:, D_MODEL:2 * D_MODEL].reshape(b, h, HG_DK).transpose(0, 2, 1)
    p3 = proj.reshape(b, 1, 4 * D_MODEL)
    lbt = lower.reshape(h, HG_DK).T
    vec = lambda col: pl.BlockSpec((None, 1, D_MODEL), lambda i: (i, 0, col))
    tr = pl.BlockSpec((None, HG_DK, h), lambda i: (i, 0, 0))
    o, s_new = pl.pallas_call(
        _gla_step_kernel,
        grid=(b,),
        in_specs=[tr, tr, vec(2), vec(3),
                  pl.BlockSpec((HG_DK, h), lambda i: (0, 0)),
                  pl.BlockSpec((1, D_MODEL), lambda i: (0, 0)),
                  pl.BlockSpec((None, h, HG_DK, HG_DK), lambda i: (i, 0, 0, 0))],
        out_specs=[pl.BlockSpec((None, 1, D_MODEL), lambda i: (i, 0, 0)),
                   pl.BlockSpec((None, h, HG_DK, HG_DK), lambda i: (i, 0, 0, 0))],
        out_shape=[jax.ShapeDtypeStruct((b, 1, D_MODEL), F32),
                   jax.ShapeDtypeStruct(state.shape, F32)],
        compiler_params=_cparams(("arbitrary",)),
        name="hgrn_step",
    )(pq, pf, p3, p3, lbt, g_norm.reshape(1, D_MODEL), state)
    return o.reshape(b, D_MODEL), s_new


def _conv_kernel(bg_ref, cg_ref, u_ref, hc_ref, hu_ref, w_ref, o_ref, tail_ref, *, tm):
    t = pl.program_id(1)
    z = cg_ref[...] * u_ref[...]
    hz = hc_ref[...] * hu_ref[...]
    hz = jnp.where(t == 0, 0.0, hz)
    z1p, z2p = hz[SUBLANES - 1:SUBLANES, :], hz[SUBLANES - 2:SUBLANES - 1, :]
    row = lax.broadcasted_iota(jnp.int32, z.shape, 0)
    z1 = jnp.where(row == 0, z1p, pltpu.roll(z, 1, 0))
    z2 = jnp.where(row == 0, z2p, jnp.where(row == 1, z1p, pltpu.roll(z, 2, 0)))
    y = z2 * w_ref[0:1, :] + z1 * w_ref[1:2, :] + z * w_ref[2:3, :]
    o_ref[...] = (bg_ref[...] * y).astype(o_ref.dtype)
    tail_ref[...] = z[tm - (CONV_WIDTH - 1):, :]


def _conv_prompt(proj, w_conv, batch, seq):
    tm = 256
    nt = seq // tm
    hb = tm // SUBLANES
    blk = lambda col: pl.BlockSpec((tm, D_MODEL), lambda b, t: (b * nt + t, col))
    halo = lambda col: pl.BlockSpec(
        (SUBLANES, D_MODEL), lambda b, t: (jnp.maximum((b * nt + t) * hb - 1, 0), col))
    return pl.pallas_call(
        functools.partial(_conv_kernel, tm=tm),
        grid=(batch, nt),
        in_specs=[blk(0), blk(1), blk(2), halo(1), halo(2),
                  pl.BlockSpec((CONV_WIDTH, D_MODEL), lambda b, t: (0, 0))],
        out_specs=[pl.BlockSpec((tm, D_MODEL), lambda b, t: (b * nt + t, 0)),
                   pl.BlockSpec((None, CONV_WIDTH - 1, D_MODEL), lambda b, t: (b, 0, 0))],
        out_shape=[jax.ShapeDtypeStruct((batch * seq, D_MODEL), BF16),
                   jax.ShapeDtypeStruct((batch, CONV_WIDTH - 1, D_MODEL), F32)],
        compiler_params=_cparams(("arbitrary", "arbitrary")),
        name="conv_prompt",
    )(proj, proj, proj, proj, proj, w_conv)


def _conv_step_kernel(bg_ref, cg_ref, u_ref, buf_ref, w_ref, o_ref, nb_ref):
    z = cg_ref[...] * u_ref[...]
    buf = buf_ref[...]
    y = buf[:, 0:1, :] * w_ref[0:1, :] + buf[:, 1:2, :] * w_ref[1:2, :] + z * w_ref[2:3, :]
    o_ref[...] = bg_ref[...] * y
    nb_ref[:, 0:1, :] = buf[:, 1:2, :]
    nb_ref[:, 1:2, :] = z


def _conv_step(proj, w_conv, buf):
    b = proj.shape[0]
    p3 = proj.reshape(b, 1, 3 * D_MODEL)
    vec = lambda col: pl.BlockSpec((b, 1, D_MODEL), lambda i: (0, 0, col))
    o, nb = pl.pallas_call(
        _conv_step_kernel,
        grid=(1,),
        in_specs=[vec(0), vec(1), vec(2),
                  pl.BlockSpec((b, CONV_WIDTH - 1, D_MODEL), lambda i: (0, 0, 0)),
                  pl.BlockSpec((CONV_WIDTH, D_MODEL), lambda i: (0, 0))],
        out_specs=[pl.BlockSpec((b, 1, D_MODEL), lambda i: (0, 0, 0)),
                   pl.BlockSpec((b, CONV_WIDTH - 1, D_MODEL), lambda i: (0, 0, 0))],
        out_shape=[jax.ShapeDtypeStruct((b, 1, D_MODEL), F32),
                   jax.ShapeDtypeStruct(buf.shape, F32)],
        compiler_params=_cparams(("arbitrary",)),
        name="conv_step",
    )(p3, p3, p3, buf, w_conv)
    return o.reshape(b, D_MODEL), nb


def _swa_kernel(*refs, dil, n_units, n_heads, has_prev):
    if has_prev:
        qc_ref, kp_ref, kc_ref, vp_ref, vc_ref, o_ref, l_ref = refs
    else:
        qc_ref, kc_ref, vc_ref, o_ref, l_ref = refs
    blk = SW_BAND
    n = pl.program_id(1)
    ri = lax.broadcasted_iota(jnp.int32, (blk, blk), 0)
    ci = lax.broadcasted_iota(jnp.int32, (blk, blk), 1)
    m_cur = ci <= ri
    m_prev_tri = ci >= ri
    lane = lax.broadcasted_iota(jnp.int32, (blk, LANES), 1)
    scale = SW_HEAD_DIM ** -0.5
    nt = (((1,), (1,)), ((), ()))

    for u in range(n_units):
        if dil == 1:
            rows = pl.ds(u * blk, blk)
            prev_src = None
            if has_prev:
                prev_src = (kp_ref, vp_ref, pl.ds(0, blk), True) if u == 0 else \
                    (kc_ref, vc_ref, pl.ds((u - 1) * blk, blk), False)
        else:
            rows = pl.ds(u, blk, stride=dil)
            prev_src = (kp_ref, vp_ref, rows, True) if has_prev else None

        def head(h, lse_acc, rows=rows, prev_src=prev_src):
            cs = pl.ds(0, SW_HEAD_DIM) if n_heads == 1 else \
                pl.ds(pl.multiple_of(h * SW_HEAD_DIM, SW_HEAD_DIM), SW_HEAD_DIM)
            q = qc_ref[rows, cs].astype(BF16)
            s_c = lax.dot_general(q, kc_ref[rows, cs].astype(BF16), nt, preferred_element_type=F32) * scale
            s_c = jnp.where(m_cur, s_c, MASK_VALUE)
            m = jnp.max(s_c, axis=-1, keepdims=True)
            if prev_src is not None:
                kr, vr, prow, first_only = prev_src
                s_p = lax.dot_general(q, kr[prow, cs].astype(BF16), nt, preferred_element_type=F32) * scale
                m_prev = (m_prev_tri & (n > 0)) if first_only else m_prev_tri
                s_p = jnp.where(m_prev, s_p, MASK_VALUE)
                m = jnp.maximum(m, jnp.max(s_p, axis=-1, keepdims=True))
            p_c = jnp.exp(s_c - m)
            l = jnp.sum(p_c, axis=-1, keepdims=True)
            o = jnp.dot(p_c.astype(BF16), vc_ref[rows, cs].astype(BF16), preferred_element_type=F32)
            if prev_src is not None:
                p_p = jnp.exp(s_p - m)
                l = l + jnp.sum(p_p, axis=-1, keepdims=True)
                o = o + jnp.dot(p_p.astype(BF16), vr[prow, cs].astype(BF16), preferred_element_type=F32)
            o_ref[rows, cs] = o / l
            lse = m + jnp.log(l)
            return jnp.where(lane == h, lse, lse_acc)

        lse0 = jnp.zeros((blk, LANES), F32)
        l_ref[rows, :] = head(0, lse0) if n_heads == 1 else lax.fori_loop(0, n_heads, head, lse0)


def _swa_prompt_group(qkv, gi, dil, batch, seq):
    span = SW_BAND * dil
    if dil == 1:
        span, n_units = 512, 4
    else:
        n_units = dil
    nb = seq // span
    has_prev = nb > 1
    hb = SW_HEADS if dil == 1 else 1
    n_hblk = SW_HEADS // hb
    wcol = hb * SW_HEAD_DIM
    per = SW_WIDTH // wcol
    pspan = SW_BAND if dil == 1 else span
    pmul = span // pspan

    def cur(which):
        return pl.BlockSpec((span, wcol), lambda b, n, hh: (b * nb + n, (gi * 3 + which) * per + hh))

    def prev(which):
        return pl.BlockSpec(
            (pspan, wcol),
            lambda b, n, hh: (jnp.maximum((b * nb + n) * pmul - 1, 0), (gi * 3 + which) * per + hh))

    if has_prev:
        in_specs = [cur(0), prev(1), cur(1), prev(2), cur(2)]
        args = [qkv] * 5
    else:
        in_specs = [cur(0), cur(1), cur(2)]
        args = [qkv] * 3
    return pl.pallas_call(
        functools.partial(_swa_kernel, dil=dil, n_units=n_units, n_heads=hb, has_prev=has_prev),
        grid=(batch, nb, n_hblk),
        in_specs=in_specs,
        out_specs=[pl.BlockSpec((span, wcol), lambda b, n, hh: (b * nb + n, hh)),
                   pl.BlockSpec((span, LANES), lambda b, n, hh: (b * nb + n, hh))],
        out_shape=[jax.ShapeDtypeStruct((batch * seq, SW_WIDTH), F32),
                   jax.ShapeDtypeStruct((batch * seq, n_hblk * LANES), F32)],
        compiler_params=_cparams(("arbitrary", "arbitrary", "arbitrary")),
        name=f"swa_prompt_g{gi}",
    )(*args)


def _merge_heads(o_refs, l_refs, rs, hbs):
    outs = []
    for h in range(SW_HEADS):
        cs = slice(h * SW_HEAD_DIM, (h + 1) * SW_HEAD_DIM)
        lses = []
        for l_ref, hb in zip(l_refs, hbs):
            lane = (h // hb) * LANES + h % hb
            lses.append(l_ref[rs, lane:lane + 1])
        mx = jnp.maximum(jnp.maximum(lses[0], lses[1]), lses[2])
        es = [jnp.exp(l - mx) for l in lses]
        den = es[0] + es[1] + es[2]
        acc = (es[0] / den) * o_refs[0][rs, cs]
        acc = acc + (es[1] / den) * o_refs[1][rs, cs]
        acc = acc + (es[2] / den) * o_refs[2][rs, cs]
        outs.append(acc)
    return jnp.concatenate(outs, axis=1)


def _swa_out_kernel(o1, o2, o3, l1, l2, l3, w_ref, res_ref, g_ref, out_ref, xb_ref, *, tm, hbs):
    j = pl.program_id(1)

    @pl.when(j == 0)
    def _():
        ch = 128

        def body(c, carry):
            rs = pl.ds(pl.multiple_of(c * ch, ch), ch)
            xb_ref[rs, :] = _merge_heads((o1, o2, o3), (l1, l2, l3), rs, hbs).astype(BF16)
            return carry

        lax.fori_loop(0, tm // ch, body, 0)

    acc = jnp.dot(xb_ref[...], w_ref[...].astype(BF16), preferred_element_type=F32)
    out_ref[...] = res_ref[...] + g_ref[...] * acc


def _swa_out(os_, ls_, w, res, mod, gate_col, seq):
    tm, tn = 512, 512
    m = res.shape[0]
    hbs = tuple(SW_HEADS // (l.shape[1] // LANES) for l in ls_)
    row = lambda width: pl.BlockSpec((tm, width), lambda i, j: (i, 0))
    return pl.pallas_call(
        functools.partial(_swa_out_kernel, tm=tm, hbs=hbs),
        grid=(m // tm, D_MODEL // tn),
        in_specs=[row(SW_WIDTH)] * 3 + [row(l.shape[1]) for l in ls_] + [
            pl.BlockSpec((SW_WIDTH, tn), lambda i, j: (0, j)),
            pl.BlockSpec((tm, tn), lambda i, j: (i, j)),
            _mod_spec_n(mod, gate_col, tm, tn, seq),
        ],
        out_specs=pl.BlockSpec((tm, tn), lambda i, j: (i, j)),
        out_shape=jax.ShapeDtypeStruct((m, D_MODEL), F32),
        scratch_shapes=[pltpu.VMEM((tm, SW_WIDTH), BF16)],
        compiler_params=_cparams(("arbitrary", "arbitrary")),
        name="swa_out",
    )(*os_, *ls_, w, res, mod)


def _swa_step_kernel(qkv_ref, c1_ref, c2_ref, c3_ref, o_ref):
    scale = SW_HEAD_DIM ** -0.5
    outs, lses = [], []
    for gi, c_ref in enumerate((c1_ref, c2_ref, c3_ref)):
        q = qkv_ref[gi, 0]
        kn = qkv_ref[gi, 1]
        vn = qkv_ref[gi, 2]
        kc = c_ref[:, 0]
        vc = c_ref[:, 1]
        s = jnp.sum(kc * q[None], axis=-1, keepdims=True) * scale
        sn = jnp.sum(kn * q, axis=-1, keepdims=True) * scale
        m = jnp.maximum(jnp.max(s, axis=0), sn)
        p = jnp.exp(s - m[None])
        pn = jnp.exp(sn - m)
        l = jnp.sum(p, axis=0) + pn
        o = (jnp.sum(p * vc, axis=0) + pn * vn) / l
        outs.append(o)
        lses.append(m + jnp.log(l))
    mx = jnp.maximum(jnp.maximum(lses[0], lses[1]), lses[2])
    es = [jnp.exp(l - mx) for l in lses]
    den = es[0] + es[1] + es[2]
    o_ref[...] = (es[0] / den) * outs[0] + (es[1] / den) * outs[1] + (es[2] / den) * outs[2]


def _swa_step(qkv, caches):
    b = qkv.shape[0]
    q5 = qkv.reshape(b, len(SW_GROUPS), 3, SW_HEADS, SW_HEAD_DIM)
    views, specs = [], []
    for c, (win, dil) in zip(caches, SW_GROUPS):
        views.append(c.reshape(b, win // dil, dil, 2, SW_HEADS, SW_HEAD_DIM))
        specs.append(pl.BlockSpec((None, win // dil, None, 2, SW_HEADS, SW_HEAD_DIM),
                                  lambda i: (i, 0, 0, 0, 0, 0)))
    o = pl.pallas_call(
        _swa_step_kernel,
        grid=(b,),
        in_specs=[pl.BlockSpec((None, len(SW_GROUPS), 3, SW_HEADS, SW_HEAD_DIM),
                               lambda i: (i, 0, 0, 0, 0))] + specs,
        out_specs=pl.BlockSpec((None, SW_HEADS, SW_HEAD_DIM), lambda i: (i, 0, 0)),
        out_shape=jax.ShapeDtypeStruct((b, SW_HEADS, SW_HEAD_DIM), F32),
        compiler_params=_cparams(("arbitrary",)),
        name="swa_step",
    )(q5, *views)
    return o.reshape(b, SW_WIDTH)


def _cache_shift_kernel(new_ref, c1, c2, c3, o1, o2, o3, sem):
    copies = []
    for gi, (c, o) in enumerate(zip((c1, c2, c3), (o1, o2, o3))):
        w = c.shape[1]
        copies.append(pltpu.make_async_copy(c.at[:, pl.ds(1, w - 1)], o.at[:, pl.ds(0, w - 1)], sem.at[2 * gi]))
        copies.append(pltpu.make_async_copy(new_ref.at[:, gi], o.at[:, w - 1], sem.at[2 * gi + 1]))
    for cp in copies:
        cp.start()
    for cp in copies:
        cp.wait()


def _cache_shift(qkv, caches):
    b = qkv.shape[0]
    q5 = qkv.reshape(b, len(SW_GROUPS), 3, SW_HEADS, SW_HEAD_DIM)[:, :, 1:]
    any_spec = pl.BlockSpec(memory_space=pl.ANY)
    return pl.pallas_call(
        _cache_shift_kernel,
        in_specs=[any_spec] * 4,
        out_specs=[any_spec] * 3,
        out_shape=[jax.ShapeDtypeStruct(c.shape, F32) for c in caches],
        scratch_shapes=[pltpu.SemaphoreType.DMA((2 * len(SW_GROUPS),))],
        name="cache_shift",
    )(q5, *caches)


def _router_kernel(x_ref, gn_ref, sc_ref, sh_ref, wr_ref, br_ref, *rest, tm):
    h_ref, ids_ref, gates_ref = rest[-3:]
    _norm_mod_into(x_ref, gn_ref, sc_ref, sh_ref, [h_ref], tm)
    logits = jnp.dot(h_ref[...], wr_ref[...], precision=HIGHEST, preferred_element_type=F32) + br_ref[...]
    lane = lax.broadcasted_iota(jnp.int32, logits.shape, 1)
    big = jnp.int32(1 << 20)
    is_g = (lane >= MOE_EXPERTS) & (lane < MOE_EXPERTS + MOE_GROUPS)
    glog = jnp.where(is_g, logits, -jnp.inf)
    gmax = jnp.max(glog, axis=-1, keepdims=True)
    gsel = jnp.min(jnp.where(glog == gmax, lane - MOE_EXPERTS, big), axis=-1, keepdims=True)
    gsum = jnp.sum(jnp.where(is_g, jnp.exp(glog - gmax), 0.0), axis=-1, keepdims=True)
    pg = 1.0 / gsum
    in_grp = (lane < MOE_EXPERTS) & ((lane // MOE_PER_GROUP) == gsel)
    el = jnp.where(in_grp, logits, -jnp.inf)
    v1 = jnp.max(el, axis=-1, keepdims=True)
    i1 = jnp.min(jnp.where(el == v1, lane, big), axis=-1, keepdims=True)
    el2 = jnp.where(lane == i1, -jnp.inf, el)
    v2 = jnp.max(el2, axis=-1, keepdims=True)
    i2 = jnp.min(jnp.where(el2 == v2, lane, big), axis=-1, keepdims=True)
    e2 = jnp.exp(v2 - v1)
    den = 1.0 + e2
    ids_ref[...] = jnp.where(lane == 0, i1, jnp.where(lane == 1, i2, 0))
    gates_ref[...] = jnp.where(lane == 0, pg * (1.0 / den), jnp.where(lane == 1, pg * (e2 / den), 0.0))


def _router(x, gn, mod, wr, br, hcat, n_tok, *, tm, rows_per_batch, row0):
    m = x.shape[0]
    blk0 = row0 // tm
    bpb = max(rows_per_batch // tm, 1)
    nb = m // tm
    n_steps = nb + (1 if hcat is None and n_tok > m else 0)
    cl = lambda i: jnp.minimum(i, nb - 1)
    msp = lambda col: pl.BlockSpec((None, mod.shape[1], D_MODEL), lambda i: (cl(i) // bpb, 0, col))
    in_specs = [
        pl.BlockSpec((tm, D_MODEL), lambda i: (cl(i), 0)),
        pl.BlockSpec((1, D_MODEL), lambda i: (0, 0)),
        msp(4), msp(3),
        pl.BlockSpec((D_MODEL, LANES), lambda i: (0, 0)),
        pl.BlockSpec((1, LANES), lambda i: (0, 0)),
    ]
    args = [x, gn.reshape(1, D_MODEL), mod, mod, wr, br]
    aliases = {}
    if hcat is not None:
        in_specs.append(pl.BlockSpec(memory_space=pl.ANY))
        args.append(hcat)
        aliases = {len(args) - 1: 0}
    return pl.pallas_call(
        functools.partial(_router_kernel, tm=tm),
        grid=(n_steps,),
        in_specs=in_specs,
        out_specs=[pl.BlockSpec((tm, D_MODEL), lambda i: (blk0 + i, 0)),
                   pl.BlockSpec((tm, LANES), lambda i: (cl(i), 0)),
                   pl.BlockSpec((tm, LANES), lambda i: (cl(i), 0))],
        out_shape=[jax.ShapeDtypeStruct((n_tok, D_MODEL), F32),
                   jax.ShapeDtypeStruct((m, LANES), jnp.int32),
                   jax.ShapeDtypeStruct((m, LANES), F32)],
        input_output_aliases=aliases,
        compiler_params=_cparams(("arbitrary",)),
        name="moe_router",
    )(*args)


def _gather_kernel(tok_ref, nused_ref, h_hbm, o_ref, buf_ref, sem):
    b = pl.program_id(0)

    @pl.when(b < nused_ref[0])
    def _():
        def issue(r, carry):
            tok = tok_ref[b * MOE_ROWS + r]
            pltpu.make_async_copy(h_hbm.at[pl.ds(tok, 1)], buf_ref.at[pl.ds(r, 1)], sem).start()
            return carry

        lax.fori_loop(0, MOE_ROWS, issue, 0)
        pltpu.make_async_copy(h_hbm.at[pl.ds(0, MOE_ROWS)], buf_ref, sem).wait()
        o_ref[...] = buf_ref[...].astype(o_ref.dtype)

    @pl.when(b >= nused_ref[0])
    def _():
        o_ref[...] = jnp.zeros_like(o_ref)


def _clamp_blk(b, nused_ref):
    return jnp.minimum(b, nused_ref[0] - 1)


def _moe_gather(hcat, row_tok, n_used, n_blocks):
    return pl.pallas_call(
        _gather_kernel,
        grid_spec=pltpu.PrefetchScalarGridSpec(
            num_scalar_prefetch=2,
            grid=(n_blocks,),
            in_specs=[pl.BlockSpec(memory_space=pl.ANY)],
            out_specs=pl.BlockSpec((MOE_ROWS, D_MODEL), lambda b, tok, nu: (b, 0)),
            scratch_shapes=[pltpu.VMEM((MOE_ROWS, D_MODEL), F32), pltpu.SemaphoreType.DMA],
        ),
        out_shape=jax.ShapeDtypeStruct((n_blocks * MOE_ROWS, D_MODEL), BF16),
        compiler_params=pltpu.CompilerParams(dimension_semantics=("arbitrary",), vmem_limit_bytes=VMEM_LIMIT,
                                             disable_bounds_checks=True),
        name="moe_gather",
    )(row_tok, n_used, hcat)


def _new_expert(b, be_ref):
    return (b == 0) | (be_ref[b] != be_ref[jnp.maximum(b - 1, 0)])


def _gate_up_kernel(be_ref, nused_ref, x_ref, wg_ref, wu_ref, h_ref, wgb_ref, wub_ref):
    b = pl.program_id(1)

    @pl.when(b < nused_ref[0])
    def _():
        @pl.when(_new_expert(b, be_ref))
        def _():
            wgb_ref[...] = wg_ref[...].astype(BF16)
            wub_ref[...] = wu_ref[...].astype(BF16)

        x = x_ref[...]
        a = jnp.dot(x, wgb_ref[...], preferred_element_type=F32)
        u = jnp.dot(x, wub_ref[...], preferred_element_type=F32)
        h_ref[...] = (_silu(a) * u).astype(h_ref.dtype)

    @pl.when(b >= nused_ref[0])
    def _():
        h_ref[...] = jnp.zeros_like(h_ref)


def _moe_gate_up(xs, w_gu, blk_expert, n_used, n_blocks):
    nc = MOE_FF // MOE_FFC
    w_spec = lambda off: pl.BlockSpec(
        (None, D_MODEL, MOE_FFC), lambda c, b, be, nu: (be[_clamp_blk(b, nu)], 0, off * nc + c))
    return pl.pallas_call(
        _gate_up_kernel,
        grid_spec=pltpu.PrefetchScalarGridSpec(
            num_scalar_prefetch=2,
            grid=(nc, n_blocks),
            in_specs=[pl.BlockSpec((MOE_ROWS, D_MODEL), lambda c, b, be, nu: (_clamp_blk(b, nu), 0)),
                      w_spec(0), w_spec(1)],
            out_specs=pl.BlockSpec((MOE_ROWS, MOE_FFC), lambda c, b, be, nu: (b, c)),
            scratch_shapes=[pltpu.VMEM((D_MODEL, MOE_FFC), BF16)] * 2,
        ),
        out_shape=jax.ShapeDtypeStruct((n_blocks * MOE_ROWS, MOE_FF), BF16),
        compiler_params=_cparams(("arbitrary", "arbitrary")),
        name="moe_gate_up",
    )(blk_expert, n_used, xs, w_gu, w_gu)


def _down_kernel(be_ref, nused_ref, h_ref, wd_ref, y_ref, wdb_ref):
    b = pl.program_id(0)

    @pl.when(b < nused_ref[0])
    def _():
        @pl.when(_new_expert(b, be_ref))
        def _():
            wdb_ref[...] = wd_ref[...].astype(BF16)

        y_ref[...] = jnp.dot(h_ref[...], wdb_ref[...], preferred_element_type=F32)

    @pl.when(b >= nused_ref[0])
    def _():
        y_ref[...] = jnp.zeros_like(y_ref)


def _moe_down(hs, w_down, blk_expert, n_used, n_blocks):
    return pl.pallas_call(
        _down_kernel,
        grid_spec=pltpu.PrefetchScalarGridSpec(
            num_scalar_prefetch=2,
            grid=(n_blocks,),
            in_specs=[pl.BlockSpec((MOE_ROWS, MOE_FF), lambda b, be, nu: (_clamp_blk(b, nu), 0)),
                      pl.BlockSpec((None, MOE_FF, D_MODEL), lambda b, be, nu: (be[_clamp_blk(b, nu)], 0, 0))],
            out_specs=pl.BlockSpec((MOE_ROWS, D_MODEL), lambda b, be, nu: (b, 0)),
            scratch_shapes=[pltpu.VMEM((MOE_FF, D_MODEL), BF16)],
        ),
        out_shape=jax.ShapeDtypeStruct((n_blocks * MOE_ROWS, D_MODEL), F32),
        compiler_params=_cparams(("arbitrary",)),
        name="moe_down",
    )(blk_expert, n_used, hs, w_down)


def _combine_kernel(dest_ref, y_hbm, x_ref, g_ref, gates_ref, o_ref, y0_ref, y1_ref, sem, *, tm, tok0):
    i = pl.program_id(0)

    def issue(r, carry):
        a = (tok0 + i * tm + r) * MOE_TOP_K
        pltpu.make_async_copy(y_hbm.at[pl.ds(dest_ref[a], 1)], y0_ref.at[pl.ds(r, 1)], sem).start()
        pltpu.make_async_copy(y_hbm.at[pl.ds(dest_ref[a + 1], 1)], y1_ref.at[pl.ds(r, 1)], sem).start()
        return carry

    lax.fori_loop(0, tm, issue, 0)
    pltpu.make_async_copy(y_hbm.at[pl.ds(0, tm)], y0_ref, sem).wait()
    pltpu.make_async_copy(y_hbm.at[pl.ds(0, tm)], y1_ref, sem).wait()
    gt = gates_ref[...]
    y = y0_ref[...] * gt[:, 0:1] + y1_ref[...] * gt[:, 1:2]
    o_ref[...] = x_ref[...] + g_ref[...] * y


def _moe_combine(y, dest, x, mod, gates, *, tm, rows_per_batch, tok0):
    m = x.shape[0]
    bpb = max(rows_per_batch // tm, 1)
    return pl.pallas_call(
        functools.partial(_combine_kernel, tm=tm, tok0=tok0),
        grid_spec=pltpu.PrefetchScalarGridSpec(
            num_scalar_prefetch=1,
            grid=(m // tm,),
            in_specs=[pl.BlockSpec(memory_space=pl.ANY),
                      pl.BlockSpec((tm, D_MODEL), lambda i, d: (i, 0)),
                      pl.BlockSpec((None, mod.shape[1], D_MODEL), lambda i, d: (i // bpb, 0, 5)),
                      pl.BlockSpec((tm, LANES), lambda i, d: (i, 0))],
            out_specs=pl.BlockSpec((tm, D_MODEL), lambda i, d: (i, 0)),
            scratch_shapes=[pltpu.VMEM((tm, D_MODEL), F32), pltpu.VMEM((tm, D_MODEL), F32),
                            pltpu.SemaphoreType.DMA],
        ),
        out_shape=jax.ShapeDtypeStruct((m, D_MODEL), F32),
        compiler_params=pltpu.CompilerParams(dimension_semantics=("arbitrary",), vmem_limit_bytes=VMEM_LIMIT,
                                             disable_bounds_checks=True),
        name="moe_combine",
    )(dest, y, x, mod, gates)


def _moe_layer(xp, xs, mod_p, mod_s, gn, w_group, b_group, w_router, b_router, w_gu, w_down, seq):
    n_p, n_s = xp.shape[0], xs.shape[0]
    n_tok = n_p + n_s
    pad = LANES - MOE_EXPERTS - MOE_GROUPS
    wr = jnp.concatenate([w_router, w_group, jnp.zeros((D_MODEL, pad), F32)], axis=1)
    br = jnp.concatenate([b_router, b_group, jnp.zeros((pad,), F32)]).reshape(1, LANES)
    hcat, ids_p, gates_p = _router(xp, gn, mod_p, wr, br, None, n_tok, tm=256, rows_per_batch=seq, row0=0)
    hcat, ids_s, gates_s = _router(xs, gn, mod_s, wr, br, hcat, n_tok, tm=n_s, rows_per_batch=n_s, row0=n_p)

    expert = jnp.concatenate([ids_p[:, :MOE_TOP_K], ids_s[:, :MOE_TOP_K]], axis=0).reshape(-1)
    n_assign = n_tok * MOE_TOP_K
    n_blocks = -(-n_assign // MOE_ROWS) + MOE_EXPERTS
    onehot = (expert[:, None] == jnp.arange(MOE_EXPERTS, dtype=jnp.int32)[None, :]).astype(jnp.int32)
    csum = jnp.cumsum(onehot, axis=0)
    rank = jnp.sum(csum * onehot, axis=1) - 1
    counts = csum[-1]
    padded = (counts + MOE_ROWS - 1) // MOE_ROWS * MOE_ROWS
    pend = jnp.cumsum(padded)
    pstart = pend - padded
    dest = (pstart[expert] + rank).astype(jnp.int32)
    tok = jnp.arange(n_assign, dtype=jnp.int32) // MOE_TOP_K
    row_tok = jnp.zeros((n_blocks * MOE_ROWS,), jnp.int32).at[dest].set(tok)
    blk_expert = jnp.minimum(
        jnp.searchsorted(pend, jnp.arange(n_blocks, dtype=jnp.int32) * MOE_ROWS, side="right"),
        MOE_EXPERTS - 1).astype(jnp.int32)
    n_used = (pend[-1:] // MOE_ROWS).astype(jnp.int32)

    xs_sorted = _moe_gather(hcat, row_tok, n_used, n_blocks)
    hs = _moe_gate_up(xs_sorted, w_gu, blk_expert, n_used, n_blocks)
    y = _moe_down(hs, w_down, blk_expert, n_used, n_blocks)
    xp_new = _moe_combine(y, dest, xp, mod_p, gates_p, tm=256, rows_per_batch=seq, tok0=0)
    xs_new = _moe_combine(y, dest, xs, mod_s, gates_s, tm=n_s, rows_per_batch=n_s, tok0=n_p)
    return xp_new, xs_new


def _final_norm_kernel(x_ref, g_ref, o_ref):
    x = x_ref[...]
    inv = lax.rsqrt(jnp.mean(x * x, axis=-1, keepdims=True) + EPS)
    o_ref[...] = (x * inv) * g_ref[...]


def _final_norm(x, g, tm):
    m = x.shape[0]
    return pl.pallas_call(
        _final_norm_kernel,
        grid=(m // tm,),
        in_specs=[pl.BlockSpec((tm, D_MODEL), lambda i: (i, 0)), pl.BlockSpec((1, D_MODEL), lambda i: (0, 0))],
        out_specs=pl.BlockSpec((tm, D_MODEL), lambda i: (i, 0)),
        out_shape=jax.ShapeDtypeStruct((m, D_MODEL), F32),
        compiler_params=_cparams(("arbitrary",)),
        name="final_norm",
    )(x, g.reshape(1, D_MODEL))


def _rope_tables(pos):
    half = SW_HEAD_DIM // 2
    inv_freq = ROPE_THETA ** (-jnp.arange(half, dtype=F32) / half)
    ang = pos.astype(F32)[:, None] * inv_freq[None, :]
    cos, sin = jnp.cos(ang), jnp.sin(ang)
    return jnp.concatenate([cos, cos], axis=1), jnp.concatenate([-sin, sin], axis=1)


def kernel(x_prompt, x_sample, state_hgrn, state_conv, cache_swa_g1, cache_swa_g2, cache_swa_g3, c_prompt, c_sample, ada_w, ada_b, norm_mix, norm_ffn, norm_final, hg_w_in, hg_w_out, hg_norm, hg_lower, cv_w_in, cv_w_conv, cv_w_out, sw_w_in, sw_w_out, moe_w_group, moe_b_group, moe_w_router, moe_b_router, moe_w_gu, moe_w_down):
    bp, seq, d = x_prompt.shape
    bs = x_sample.shape[0]
    n_p = bp * seq
    xp = x_prompt.reshape(n_p, d)
    xs = x_sample.reshape(bs, d)

    mod_all = _ada_mod(jnp.concatenate([c_prompt, c_sample], axis=0), ada_w, ada_b)
    sm = jax.nn.softmax(hg_lower.astype(F32), axis=0)
    lower = jnp.cumsum(sm, axis=0) - sm[0]
    rope_p = _rope_tables(jnp.arange(seq))
    rope_s = _rope_tables(jnp.full((bs,), PAST_LEN, jnp.int32))

    tm_p, tn_p = 1024, 512
    hg_p, hg_s, cv_p, cv_s = [], [], [], []
    sw_p, sw_s = None, None
    for i in range(DEPTH):
        kind, j = i % N_MIXERS, i // N_MIXERS
        mod_p = mod_all[i, :bp].reshape(bp, 1, 6 * d)
        mod_s = mod_all[i, bp:].reshape(1, bs, 6 * d)
        pp = dict(tm=tm_p, tn=tn_p, rows_per_batch=seq)
        ps = dict(tm=bs, tn=1024, rows_per_batch=bs)
        if kind == 0:
            proj_p = _proj(xp, norm_mix[i], mod_p, 0, 1, hg_w_in[j], **pp)
            proj_s = _proj(xs, norm_mix[i], mod_s, 0, 1, hg_w_in[j], **ps)
            lhs_p, st_p = _gla_prompt(proj_p, lower[j], hg_norm[j], bp, seq)
            lhs_s, st_s = _gla_step(proj_s, lower[j], hg_norm[j], state_hgrn[j])
            hg_p.append(st_p)
            hg_s.append(st_s)
            xp = _outproj(lhs_p, hg_w_out[j], xp, mod_p, 2, **pp)
            xs = _outproj(lhs_s, hg_w_out[j], xs, mod_s, 2, **ps)
        elif kind == 1:
            proj_p = _proj(xp, norm_mix[i], mod_p, 0, 1, cv_w_in[j], **pp)
            proj_s = _proj(xs, norm_mix[i], mod_s, 0, 1, cv_w_in[j], **ps)
            lhs_p, tail_p = _conv_prompt(proj_p, cv_w_conv[j], bp, seq)
            lhs_s, tail_s = _conv_step(proj_s, cv_w_conv[j], state_conv[j])
            cv_p.append(tail_p)
            cv_s.append(tail_s)
            xp = _outproj(lhs_p, cv_w_out[j], xp, mod_p, 2, **pp)
            xs = _outproj(lhs_s, cv_w_out[j], xs, mod_s, 2, **ps)
        else:
            pp_r = dict(tm=512, tn=SW_WIDTH, rows_per_batch=seq)
            qkv_p = _proj(xp, norm_mix[i], mod_p, 0, 1, sw_w_in[j], rope=rope_p, **pp_r)
            qkv_s = _proj(xs, norm_mix[i], mod_s, 0, 1, sw_w_in[j], rope=rope_s, **ps)
            os_, ls_ = [], []
            for gi, (win, dil) in enumerate(SW_GROUPS):
                o_g, l_g = _swa_prompt_group(qkv_p, gi, dil, bp, seq)
                os_.append(o_g)
                ls_.append(l_g)
            xp = _swa_out(os_, ls_, sw_w_out[j], xp, mod_p, 2, seq)
            caches = (cache_swa_g1[j], cache_swa_g2[j], cache_swa_g3[j])
            o_s = _swa_step(qkv_s, caches)
            xs = _outproj(o_s, sw_w_out[j], xs, mod_s, 2, **ps)
            q6 = qkv_p.reshape(bp, seq, len(SW_GROUPS), 3, SW_HEADS, SW_HEAD_DIM)
            sw_p = [q6[:, seq - min(win, seq):, gi, 1:][None] for gi, (win, _) in enumerate(SW_GROUPS)]
            sw_s = [c[None] for c in _cache_shift(qkv_s, caches)]
        xp, xs = _moe_layer(xp, xs, mod_p, mod_s, norm_ffn[i], moe_w_group[i], moe_b_group[i],
                            moe_w_router[i], moe_b_router[i], moe_w_gu[i], moe_w_down[i], seq)

    y_p = _final_norm(xp, norm_final, 512).reshape(bp, seq, d)
    y_s = _final_norm(xs, norm_final, bs).reshape(bs, 1, d)
    return (y_p, y_s, jnp.stack(hg_p), jnp.stack(hg_s), jnp.stack(cv_p), jnp.stack(cv_s),
            sw_p[0], sw_s[0], sw_p[1], sw_s[1], sw_p[2], sw_s[2])
```

```python
import functools

import jax
import jax.numpy as jnp
from jax import lax
from jax.experimental import pallas as pl
from jax.experimental.pallas import tpu as pltpu

F32 = jnp.float32
BF16 = jnp.bfloat16

D_MODEL = 2048
DEPTH = 4
N_MIXERS = 3
EPS = 1e-6
MASK_VALUE = -1e30
F_FLOOR = 1e-30
HG_DK = 128
HG_HEADS = D_MODEL // HG_DK
HG_CHUNK = 64
HG_SUB = 16
CONV_WIDTH = 3
SW_GROUPS = ((128, 1), (512, 4), (2048, 16))
SW_HEADS = 8
SW_HEAD_DIM = 128
SW_WIDTH = SW_HEADS * SW_HEAD_DIM
SW_BAND = 128
ROPE_THETA = 10000.0
MOE_GROUPS = 4
MOE_PER_GROUP = 8
MOE_EXPERTS = MOE_GROUPS * MOE_PER_GROUP
MOE_TOP_K = 2
MOE_FF = 1024
PAST_LEN = 16384

LANES = 128
SUBLANES = 8
VMEM_LIMIT = 52 * 1024 * 1024
MOE_ROWS = 256
MOE_FFC = 512
ROW_TILES = D_MODEL // LANES
HIGHEST = lax.Precision.HIGHEST


def _cparams(sem):
    return pltpu.CompilerParams(dimension_semantics=sem, vmem_limit_bytes=VMEM_LIMIT)


def _sigmoid(x):
    return 1.0 / (1.0 + jnp.exp(-x))


def _silu(x):
    return x * _sigmoid(x)


def _bf16_round(x):
    return x.astype(BF16).astype(F32)


def _ada_kernel(c_ref, w_ref, b_ref, o_ref):
    c = c_ref[...]
    cond = _silu(c).astype(BF16)
    o_ref[...] = jnp.dot(cond, w_ref[...].astype(BF16), preferred_element_type=F32) + b_ref[...]


def _ada_mod(c_all, ada_w, ada_b):
    rows = c_all.shape[0]
    n = ada_w.shape[-1]
    tn = 1024
    return pl.pallas_call(
        _ada_kernel,
        grid=(DEPTH, n // tn),
        in_specs=[
            pl.BlockSpec((rows, D_MODEL), lambda l, j: (0, 0)),
            pl.BlockSpec((None, D_MODEL, tn), lambda l, j: (l, 0, j)),
            pl.BlockSpec((None, 1, tn), lambda l, j: (l, 0, j)),
        ],
        out_specs=pl.BlockSpec((None, rows, tn), lambda l, j: (l, 0, j)),
        out_shape=jax.ShapeDtypeStruct((DEPTH, rows, n), F32),
        compiler_params=_cparams(("arbitrary", "arbitrary")),
        name="ada_mod",
    )(c_all, ada_w, ada_b.reshape(DEPTH, 1, n))


def _norm_mod_rows(x, gn, sc, sh):
    inv = lax.rsqrt(jnp.mean(x * x, axis=-1, keepdims=True) + EPS)
    return (x * inv) * gn * (1.0 + sc) + sh


def _norm_mod_into(x_ref, gn_ref, sc_ref, sh_ref, dst_refs, tm):
    ch = min(tm, 128)
    per_row = sc_ref.shape[0] != 1

    def body(c, carry):
        rs = pl.ds(pl.multiple_of(c * ch, ch), ch)
        sc = sc_ref[rs, :] if per_row else sc_ref[...]
        sh = sh_ref[rs, :] if per_row else sh_ref[...]
        h = _norm_mod_rows(x_ref[rs, :], gn_ref[...], sc, sh)
        for d in dst_refs:
            d[rs, :] = h.astype(d.dtype)
        return carry

    lax.fori_loop(0, tm // ch, body, 0)


def _norm_mod_kernel(x_ref, gn_ref, sc_ref, sh_ref, o_ref, *, tm):
    _norm_mod_into(x_ref, gn_ref, sc_ref, sh_ref, [o_ref], tm)


def _norm_mod(x, gn, mod, shift_col, scale_col, *, tm, rows_per_batch):
    m = x.shape[0]
    bpb = max(rows_per_batch // tm, 1)
    msp = lambda col: pl.BlockSpec((None, mod.shape[1], D_MODEL), lambda i: (i // bpb, 0, col))
    return pl.pallas_call(
        functools.partial(_norm_mod_kernel, tm=tm),
        grid=(m // tm,),
        in_specs=[pl.BlockSpec((tm, D_MODEL), lambda i: (i, 0)),
                  pl.BlockSpec((1, D_MODEL), lambda i: (0, 0)),
                  msp(scale_col), msp(shift_col)],
        out_specs=pl.BlockSpec((tm, D_MODEL), lambda i: (i, 0)),
        out_shape=jax.ShapeDtypeStruct((m, D_MODEL), BF16),
        compiler_params=_cparams(("arbitrary",)),
        name="norm_mod",
    )(x, gn.reshape(1, D_MODEL), mod, mod)


def _rope_tile(acc, cos, sin_signed):
    outs = []
    for h in range(acc.shape[1] // SW_HEAD_DIM):
        xh = acc[:, h * SW_HEAD_DIM:(h + 1) * SW_HEAD_DIM]
        outs.append(xh * cos + pltpu.roll(xh, SW_HEAD_DIM // 2, 1) * sin_signed)
    return jnp.concatenate(outs, axis=1)


def _mm_kernel(x_ref, w_ref, *rest, epilogue):
    o_ref, wb_ref = rest[-2:]
    j, i = pl.program_id(0), pl.program_id(1)

    @pl.when(i == 0)
    def _():
        wb_ref[...] = w_ref[...].astype(BF16)

    acc = jnp.dot(x_ref[...].astype(BF16), wb_ref[...], preferred_element_type=F32)
    if epilogue == "rope":
        cos_ref, sin_ref = rest[:2]
        is_qk = (j % 3) < 2

        @pl.when(is_qk)
        def _():
            o_ref[...] = _rope_tile(acc, cos_ref[...], sin_ref[...])

        @pl.when(jnp.logical_not(is_qk))
        def _():
            o_ref[...] = acc
    elif epilogue == "resgate":
        res_ref, g_ref = rest[:2]
        o_ref[...] = res_ref[...] + g_ref[...] * acc
    else:
        o_ref[...] = acc


def _mm(x, w, *, tm, tn, rope=None, resgate=None):
    m, k = x.shape
    n = w.shape[1]
    in_specs = [pl.BlockSpec((tm, k), lambda j, i: (i, 0)),
                pl.BlockSpec((k, tn), lambda j, i: (0, j))]
    args = [x, w]
    epilogue = "none"
    if rope is not None:
        epilogue = "rope"
        nblk = rope[0].shape[0] // tm
        in_specs += [pl.BlockSpec((tm, SW_HEAD_DIM), lambda j, i: (i % nblk, 0))] * 2
        args += list(rope)
    if resgate is not None:
        epilogue = "resgate"
        res, mod, gate_col, rows_per_batch = resgate
        bpb = max(rows_per_batch // tm, 1)
        per = D_MODEL // tn
        in_specs += [pl.BlockSpec((tm, tn), lambda j, i: (i, j)),
                     pl.BlockSpec((None, mod.shape[1], tn), lambda j, i: (i // bpb, 0, gate_col * per + j))]
        args += [res, mod]
    return pl.pallas_call(
        functools.partial(_mm_kernel, epilogue=epilogue),
        grid=(n // tn, m // tm),
        in_specs=in_specs,
        out_specs=pl.BlockSpec((tm, tn), lambda j, i: (i, j)),
        out_shape=jax.ShapeDtypeStruct((m, n), F32),
        scratch_shapes=[pltpu.VMEM((k, tn), BF16)],
        compiler_params=_cparams(("arbitrary", "arbitrary")),
        name="mm_" + epilogue,
    )(*args)


def _hgrn_gates(qp, fp, lb):
    q = _silu(qp)
    f = lb + (1.0 - lb) * _sigmoid(fp)
    logf = jnp.log(jnp.maximum(f, F_FLOOR))
    k = (1.0 - lb) * _sigmoid(-fp)
    return q, logf, k


def _head_norm_gate(o, gn, gp):
    inv = lax.rsqrt(jnp.mean(o * o, axis=-1, keepdims=True) + EPS)
    return (o * inv) * gn * _silu(gp)


def _gla_kernel(q_ref, f_ref, i_ref, g_ref, lb_ref, gn_ref, o_ref, s_out_ref, st_ref, *, n_chunks, n_hb):
    c_len, n_sub = HG_CHUNK, HG_CHUNK // HG_SUB
    t = pl.program_id(2)

    @pl.when(t == 0)
    def _():
        st_ref[...] = jnp.zeros_like(st_ref)

    row = lax.broadcasted_iota(jnp.int32, (c_len, HG_DK), 0)
    r4 = lax.broadcasted_iota(jnp.int32, (c_len, n_sub * c_len), 0)
    c4 = lax.broadcasted_iota(jnp.int32, (c_len, n_sub * c_len), 1)
    mask4 = ((r4 // HG_SUB) == (c4 // c_len)) & ((c4 % c_len) <= r4)
    nt = (((1,), (1,)), ((), ()))
    tn = (((0,), (0,)), ((), ()))

    for c, hh in [(c, hh) for c in range(n_chunks) for hh in range(n_hb)]:
        rs = pl.ds(c * c_len, c_len)
        cs = slice(hh * HG_DK, (hh + 1) * HG_DK)
        lb, gn = lb_ref[:, cs], gn_ref[:, cs]
        q, logf, k = _hgrn_gates(q_ref[rs, cs], f_ref[rs, cs], lb)
        v = i_ref[rs, cs]
        g = logf
        sh = 1
        while sh < c_len:
            g = g + jnp.where(row >= sh, pltpu.roll(g, sh, 0), 0.0)
            sh *= 2
        refs = [g[HG_SUB * i + HG_SUB // 2 - 1:HG_SUB * i + HG_SUB // 2, :] for i in range(n_sub)]
        mrows = jnp.concatenate([jnp.broadcast_to(r, (HG_SUB, HG_DK)) for r in refs], axis=0)
        qh = (q * jnp.exp(g - mrows)).astype(BF16)
        khs = [k * jnp.exp(jnp.where(row < HG_SUB * (i + 1), refs[i] - g, -jnp.inf)) for i in range(n_sub)]
        kh = jnp.concatenate(khs, axis=0).astype(BF16)
        a4 = lax.dot_general(qh, kh, nt, preferred_element_type=F32)
        a4 = jnp.where(mask4, a4, 0.0).astype(BF16)
        vb = v.astype(BF16)
        v4 = jnp.concatenate([vb] * n_sub, axis=0)
        st = st_ref[hh]
        o = jnp.dot(a4, v4, preferred_element_type=F32)
        o = o + lax.dot_general((q * jnp.exp(g)).astype(BF16), st.astype(BF16), nt,
                                preferred_element_type=F32)
        glast = g[c_len - 1:c_len, :]
        kd = (k * jnp.exp(glast - g)).astype(BF16)
        st_ref[hh] = st * jnp.exp(glast) + lax.dot_general(vb, kd, tn, preferred_element_type=F32)
        o_ref[rs, cs] = _head_norm_gate(o, gn, g_ref[rs, cs]).astype(o_ref.dtype)

    @pl.when(t == pl.num_programs(2) - 1)
    def _():
        for hh in range(n_hb):
            s_out_ref[hh] = st_ref[hh].T


def _gla_prompt(proj, lower, g_norm, batch, seq):
    tb = 512
    n_hb = 2
    nt = seq // tb
    hblk = HG_HEADS // n_hb
    wcol = n_hb * HG_DK

    def col(off):
        return pl.BlockSpec((tb, wcol), lambda b, hh, t: (b * nt + t, off * hblk + hh))

    return pl.pallas_call(
        functools.partial(_gla_kernel, n_chunks=tb // HG_CHUNK, n_hb=n_hb),
        grid=(batch, hblk, nt),
        in_specs=[col(0), col(1), col(2), col(3),
                  pl.BlockSpec((1, wcol), lambda b, hh, t: (0, hh)),
                  pl.BlockSpec((1, wcol), lambda b, hh, t: (0, hh))],
        out_specs=[pl.BlockSpec((tb, wcol), lambda b, hh, t: (b * nt + t, hh)),
                   pl.BlockSpec((None, n_hb, HG_DK, HG_DK), lambda b, hh, t: (b, hh, 0, 0))],
        out_shape=[jax.ShapeDtypeStruct((batch * seq, D_MODEL), BF16),
                   jax.ShapeDtypeStruct((batch, HG_HEADS, HG_DK, HG_DK), F32)],
        scratch_shapes=[pltpu.VMEM((n_hb, HG_DK, HG_DK), F32)],
        compiler_params=_cparams(("arbitrary", "arbitrary", "arbitrary")),
        name="hgrn_prompt",
    )(proj, proj, proj, proj, lower.reshape(1, D_MODEL), g_norm.reshape(1, D_MODEL))


def _gla_step_kernel(qt_ref, ft_ref, i_ref, g_ref, lbt_ref, gn_ref, s_ref, o_ref, s_out_ref):
    for h in range(HG_HEADS):
        cs = slice(h * HG_DK, (h + 1) * HG_DK)
        q, logf, k = _hgrn_gates(qt_ref[:, h:h + 1], ft_ref[:, h:h + 1], lbt_ref[:, h:h + 1])
        dec = jnp.exp(logf)
        s_old = s_ref[h]
        v = i_ref[:, cs]
        s_out_ref[h] = dec * s_old + k * v
        qd = _bf16_round(jnp.broadcast_to(q * dec, s_old.shape))
        o = jnp.sum(_bf16_round(s_old) * qd, axis=0, keepdims=True) + jnp.sum(q * k, axis=0, keepdims=True) * v
        o_ref[:, cs] = _head_norm_gate(o, gn_ref[:, cs], g_ref[:, cs])


def _gla_step(proj, lower, g_norm, state):
    b = proj.shape[0]
    h = HG_HEADS
    pq = proj[:, :D_MODEL].reshape(b, h, HG_DK).transpose(0, 2, 1)
    pf = proj[:, D_MODEL:2 * D_MODEL].reshape(b, h, HG_DK).transpose(0, 2, 1)
    p3 = proj.reshape(b, 1, 4 * D_MODEL)
    lbt = lower.reshape(h, HG_DK).T
    vec = lambda col: pl.BlockSpec((None, 1, D_MODEL), lambda i: (i, 0, col))
    tr = pl.BlockSpec((None, HG_DK, h), lambda i: (i, 0, 0))
    o, s_new = pl.pallas_call(
        _gla_step_kernel,
        grid=(b,),
        in_specs=[tr, tr, vec(2), vec(3),
                  pl.BlockSpec((HG_DK, h), lambda i: (0, 0)),
                  pl.BlockSpec((1, D_MODEL), lambda i: (0, 0)),
                  pl.BlockSpec((None, h, HG_DK, HG_DK), lambda i: (i, 0, 0, 0))],
        out_specs=[pl.BlockSpec((None, 1, D_MODEL), lambda i: (i, 0, 0)),
                   pl.BlockSpec((None, h, HG_DK, HG_DK), lambda i: (i, 0, 0, 0))],
        out_shape=[jax.ShapeDtypeStruct((b, 1, D_MODEL), F32),
                   jax.ShapeDtypeStruct(state.shape, F32)],
        compiler_params=_cparams(("arbitrary",)),
        name="hgrn_step",
    )(pq, pf, p3, p3, lbt, g_norm.reshape(1, D_MODEL), state)
    return o.reshape(b, D_MODEL), s_new


def _conv_kernel(bg_ref, cg_ref, u_ref, hc_ref, hu_ref, w_ref, o_ref, tail_ref, *, tm):
    t = pl.program_id(1)
    z = cg_ref[...] * u_ref[...]
    hz = hc_ref[...] * hu_ref[...]
    hz = jnp.where(t == 0, 0.0, hz)
    z1p, z2p = hz[SUBLANES - 1:SUBLANES, :], hz[SUBLANES - 2:SUBLANES - 1, :]
    row = lax.broadcasted_iota(jnp.int32, z.shape, 0)
    z1 = jnp.where(row == 0, z1p, pltpu.roll(z, 1, 0))
    z2 = jnp.where(row == 0, z2p, jnp.where(row == 1, z1p, pltpu.roll(z, 2, 0)))
    y = z2 * w_ref[0:1, :] + z1 * w_ref[1:2, :] + z * w_ref[2:3, :]
    o_ref[...] = (bg_ref[...] * y).astype(o_ref.dtype)
    tail_ref[...] = z[tm - (CONV_WIDTH - 1):, :]


def _conv_prompt(proj, w_conv, batch, seq):
    tm = 256
    nt = seq // tm
    hb = tm // SUBLANES
    blk = lambda col: pl.BlockSpec((tm, D_MODEL), lambda b, t: (b * nt + t, col))
    halo = lambda col: pl.BlockSpec(
        (SUBLANES, D_MODEL), lambda b, t: (jnp.maximum((b * nt + t) * hb - 1, 0), col))
    return pl.pallas_call(
        functools.partial(_conv_kernel, tm=tm),
        grid=(batch, nt),
        in_specs=[blk(0), blk(1), blk(2), halo(1), halo(2),
                  pl.BlockSpec((CONV_WIDTH, D_MODEL), lambda b, t: (0, 0))],
        out_specs=[pl.BlockSpec((tm, D_MODEL), lambda b, t: (b * nt + t, 0)),
                   pl.BlockSpec((None, CONV_WIDTH - 1, D_MODEL), lambda b, t: (b, 0, 0))],
        out_shape=[jax.ShapeDtypeStruct((batch * seq, D_MODEL), BF16),
                   jax.ShapeDtypeStruct((batch, CONV_WIDTH - 1, D_MODEL), F32)],
        compiler_params=_cparams(("arbitrary", "arbitrary")),
        name="conv_prompt",
    )(proj, proj, proj, proj, proj, w_conv)


def _conv_step_kernel(bg_ref, cg_ref, u_ref, buf_ref, w_ref, o_ref, nb_ref):
    z = cg_ref[...] * u_ref[...]
    buf = buf_ref[...]
    y = buf[:, 0:1, :] * w_ref[0:1, :] + buf[:, 1:2, :] * w_ref[1:2, :] + z * w_ref[2:3, :]
    o_ref[...] = bg_ref[...] * y
    nb_ref[:, 0:1, :] = buf[:, 1:2, :]
    nb_ref[:, 1:2, :] = z


def _conv_step(proj, w_conv, buf):
    b = proj.shape[0]
    p3 = proj.reshape(b, 1, 3 * D_MODEL)
    vec = lambda col: pl.BlockSpec((b, 1, D_MODEL), lambda i: (0, 0, col))
    o, nb = pl.pallas_call(
        _conv_step_kernel,
        grid=(1,),
        in_specs=[vec(0), vec(1), vec(2),
                  pl.BlockSpec((b, CONV_WIDTH - 1, D_MODEL), lambda i: (0, 0, 0)),
                  pl.BlockSpec((CONV_WIDTH, D_MODEL), lambda i: (0, 0))],
        out_specs=[pl.BlockSpec((b, 1, D_MODEL), lambda i: (0, 0, 0)),
                   pl.BlockSpec((b, CONV_WIDTH - 1, D_MODEL), lambda i: (0, 0, 0))],
        out_shape=[jax.ShapeDtypeStruct((b, 1, D_MODEL), F32),
                   jax.ShapeDtypeStruct(buf.shape, F32)],
        compiler_params=_cparams(("arbitrary",)),
        name="conv_step",
    )(p3, p3, p3, buf, w_conv)
    return o.reshape(b, D_MODEL), nb


def _swa_kernel(*refs, dil, n_units, n_heads, has_prev):
    if has_prev:
        qc_ref, kp_ref, kc_ref, vp_ref, vc_ref, o_ref, l_ref = refs
    else:
        qc_ref, kc_ref, vc_ref, o_ref, l_ref = refs
    blk = SW_BAND
    n = pl.program_id(1)
    ri = lax.broadcasted_iota(jnp.int32, (blk, blk), 0)
    ci = lax.broadcasted_iota(jnp.int32, (blk, blk), 1)
    m_cur = ci <= ri
    m_prev_tri = ci >= ri
    lane = lax.broadcasted_iota(jnp.int32, (blk, LANES), 1)
    scale = SW_HEAD_DIM ** -0.5
    nt = (((1,), (1,)), ((), ()))

    for u in range(n_units):
        if dil == 1:
            rows = pl.ds(u * blk, blk)
            prev_src = None
            if has_prev:
                prev_src = (kp_ref, vp_ref, pl.ds(0, blk), True) if u == 0 else \
                    (kc_ref, vc_ref, pl.ds((u - 1) * blk, blk), False)
        else:
            rows = pl.ds(u, blk, stride=dil)
            prev_src = (kp_ref, vp_ref, rows, True) if has_prev else None

        def head(h, lse_acc, rows=rows, prev_src=prev_src):
            cs = pl.ds(0, SW_HEAD_DIM) if n_heads == 1 else \
                pl.ds(pl.multiple_of(h * SW_HEAD_DIM, SW_HEAD_DIM), SW_HEAD_DIM)
            q = qc_ref[rows, cs].astype(BF16)
            s_c = lax.dot_general(q, kc_ref[rows, cs].astype(BF16), nt, preferred_element_type=F32) * scale
            s_c = jnp.where(m_cur, s_c, MASK_VALUE)
            m = jnp.max(s_c, axis=-1, keepdims=True)
            if prev_src is not None:
                kr, vr, prow, first_only = prev_src
                s_p = lax.dot_general(q, kr[prow, cs].astype(BF16), nt, preferred_element_type=F32) * scale
                m_prev = (m_prev_tri & (n > 0)) if first_only else m_prev_tri
                s_p = jnp.where(m_prev, s_p, MASK_VALUE)
                m = jnp.maximum(m, jnp.max(s_p, axis=-1, keepdims=True))
            p_c = jnp.exp(s_c - m)
            l = jnp.sum(p_c, axis=-1, keepdims=True)
            o = jnp.dot(p_c.astype(BF16), vc_ref[rows, cs].astype(BF16), preferred_element_type=F32)
            if prev_src is not None:
                p_p = jnp.exp(s_p - m)
                l = l + jnp.sum(p_p, axis=-1, keepdims=True)
                o = o + jnp.dot(p_p.astype(BF16), vr[prow, cs].astype(BF16), preferred_element_type=F32)
            o_ref[rows, cs] = o / l
            lse = m + jnp.log(l)
            return jnp.where(lane == h, lse, lse_acc)

        lse0 = jnp.zeros((blk, LANES), F32)
        l_ref[rows, :] = head(0, lse0) if n_heads == 1 else lax.fori_loop(0, n_heads, head, lse0)


def _swa_prompt_group(qkv, gi, dil, batch, seq):
    span = SW_BAND * dil
    if dil == 1:
        span, n_units = 512, 4
    else:
        n_units = dil
    nb = seq // span
    has_prev = nb > 1
    hb = SW_HEADS if dil == 1 else 1
    n_hblk = SW_HEADS // hb
    wcol = hb * SW_HEAD_DIM
    per = SW_WIDTH // wcol
    pspan = SW_BAND if dil == 1 else span
    pmul = span // pspan

    def cur(which):
        return pl.BlockSpec((span, wcol), lambda b, n, hh: (b * nb + n, (gi * 3 + which) * per + hh))

    def prev(which):
        return pl.BlockSpec(
            (pspan, wcol),
            lambda b, n, hh: (jnp.maximum((b * nb + n) * pmul - 1, 0), (gi * 3 + which) * per + hh))

    if has_prev:
        in_specs = [cur(0), prev(1), cur(1), prev(2), cur(2)]
        args = [qkv] * 5
    else:
        in_specs = [cur(0), cur(1), cur(2)]
        args = [qkv] * 3
    return pl.pallas_call(
        functools.partial(_swa_kernel, dil=dil, n_units=n_units, n_heads=hb, has_prev=has_prev),
        grid=(batch, nb, n_hblk),
        in_specs=in_specs,
        out_specs=[pl.BlockSpec((span, wcol), lambda b, n, hh: (b * nb + n, hh)),
                   pl.BlockSpec((span, LANES), lambda b, n, hh: (b * nb + n, hh))],
        out_shape=[jax.ShapeDtypeStruct((batch * seq, SW_WIDTH), F32),
                   jax.ShapeDtypeStruct((batch * seq, n_hblk * LANES), F32)],
        compiler_params=_cparams(("arbitrary", "arbitrary", "arbitrary")),
        name=f"swa_prompt_g{gi}",
    )(*args)


def _merge_heads(o_refs, l_refs, rs, hbs):
    outs = []
    for h in range(SW_HEADS):
        cs = slice(h * SW_HEAD_DIM, (h + 1) * SW_HEAD_DIM)
        lses = []
        for l_ref, hb in zip(l_refs, hbs):
            lane = (h // hb) * LANES + h % hb
            lses.append(l_ref[rs, lane:lane + 1])
        mx = jnp.maximum(jnp.maximum(lses[0], lses[1]), lses[2])
        es = [jnp.exp(l - mx) for l in lses]
        den = es[0] + es[1] + es[2]
        acc = None
        for e, o_ref in zip(es, o_refs):
            o = o_ref[rs, cs]
            term = _bf16_round(jnp.broadcast_to(e / den, o.shape)) * _bf16_round(o)
            acc = term if acc is None else acc + term
        outs.append(acc)
    return jnp.concatenate(outs, axis=1)


def _swa_merge_kernel(o1, o2, o3, l1, l2, l3, out_ref, *, tm, hbs):
    ch = 128

    def body(c, carry):
        rs = pl.ds(pl.multiple_of(c * ch, ch), ch)
        out_ref[rs, :] = _merge_heads((o1, o2, o3), (l1, l2, l3), rs, hbs).astype(out_ref.dtype)
        return carry

    lax.fori_loop(0, tm // ch, body, 0)


def _swa_merge(os_, ls_):
    tm = 512
    m = os_[0].shape[0]
    hbs = tuple(SW_HEADS // (l.shape[1] // LANES) for l in ls_)
    row = lambda width: pl.BlockSpec((tm, width), lambda i: (i, 0))
    return pl.pallas_call(
        functools.partial(_swa_merge_kernel, tm=tm, hbs=hbs),
        grid=(m // tm,),
        in_specs=[row(SW_WIDTH)] * 3 + [row(l.shape[1]) for l in ls_],
        out_specs=row(SW_WIDTH),
        out_shape=jax.ShapeDtypeStruct((m, SW_WIDTH), BF16),
        compiler_params=_cparams(("arbitrary",)),
        name="swa_merge",
    )(*os_, *ls_)


def _swa_step_kernel(qkv_ref, c1_ref, c2_ref, c3_ref, o_ref):
    scale = SW_HEAD_DIM ** -0.5
    outs, lses = [], []
    for gi, c_ref in enumerate((c1_ref, c2_ref, c3_ref)):
        q = _bf16_round(qkv_ref[gi, 0])
        kn = _bf16_round(qkv_ref[gi, 1])
        vn = _bf16_round(qkv_ref[gi, 2])
        kc = _bf16_round(c_ref[:, 0])
        vc = _bf16_round(c_ref[:, 1])
        s = jnp.sum(kc * q[None], axis=-1, keepdims=True) * scale
        sn = jnp.sum(kn * q, axis=-1, keepdims=True) * scale
        m = jnp.maximum(jnp.max(s, axis=0), sn)
        lse = m + jnp.log(jnp.sum(jnp.exp(s - m[None]), axis=0) + jnp.exp(sn - m))
        shape = vc.shape
        p = _bf16_round(jnp.broadcast_to(jnp.exp(s - lse[None]), shape))
        pn = _bf16_round(jnp.broadcast_to(jnp.exp(sn - lse), shape[1:]))
        outs.append(jnp.sum(p * vc, axis=0) + pn * vn)
        lses.append(lse)
    mx = jnp.maximum(jnp.maximum(lses[0], lses[1]), lses[2])
    es = [jnp.exp(l - mx) for l in lses]
    den = es[0] + es[1] + es[2]
    acc = None
    for e, o in zip(es, outs):
        term = _bf16_round(jnp.broadcast_to(e / den, o.shape)) * _bf16_round(o)
        acc = term if acc is None else acc + term
    o_ref[...] = acc


def _swa_step(qkv, caches):
    b = qkv.shape[0]
    q5 = qkv.reshape(b, len(SW_GROUPS), 3, SW_HEADS, SW_HEAD_DIM)
    views, specs = [], []
    for c, (win, dil) in zip(caches, SW_GROUPS):
        views.append(c.reshape(b, win // dil, dil, 2, SW_HEADS, SW_HEAD_DIM))
        specs.append(pl.BlockSpec((None, win // dil, None, 2, SW_HEADS, SW_HEAD_DIM),
                                  lambda i: (i, 0, 0, 0, 0, 0)))
    o = pl.pallas_call(
        _swa_step_kernel,
        grid=(b,),
        in_specs=[pl.BlockSpec((None, len(SW_GROUPS), 3, SW_HEADS, SW_HEAD_DIM),
                               lambda i: (i, 0, 0, 0, 0))] + specs,
        out_specs=pl.BlockSpec((None, SW_HEADS, SW_HEAD_DIM), lambda i: (i, 0, 0)),
        out_shape=jax.ShapeDtypeStruct((b, SW_HEADS, SW_HEAD_DIM), F32),
        compiler_params=_cparams(("arbitrary",)),
        name="swa_step",
    )(q5, *views)
    return o.reshape(b, SW_WIDTH)


def _cache_shift_kernel(cur_ref, nxt_ref, knew_ref, vnew_ref, o_ref, *, wb):
    k = pl.program_id(1)
    last = pl.num_programs(1) - 1

    def row(i, carry):
        o_ref[i] = cur_ref[i + 1]
        return carry

    lax.fori_loop(0, wb - 1, row, 0, unroll=8)

    @pl.when(k < last)
    def _():
        o_ref[wb - 1] = nxt_ref[0]

    @pl.when(k == last)
    def _():
        o_ref[wb - 1, 0] = knew_ref[...]
        o_ref[wb - 1, 1] = vnew_ref[...]


def _cache_shift(qkv, caches):
    b = qkv.shape[0]
    q5 = qkv.reshape(b, len(SW_GROUPS), 3, SW_HEADS, SW_HEAD_DIM)
    outs = []
    for gi, c in enumerate(caches):
        w = c.shape[1]
        wb = min(w, 512)
        tail = (2, SW_HEADS, SW_HEAD_DIM)
        outs.append(pl.pallas_call(
            functools.partial(_cache_shift_kernel, wb=wb),
            grid=(b, w // wb),
            in_specs=[
                pl.BlockSpec((None, wb) + tail, lambda i, k: (i, k, 0, 0, 0)),
                pl.BlockSpec((None, 1) + tail, lambda i, k: (i, jnp.minimum((k + 1) * wb, w - 1), 0, 0, 0)),
                pl.BlockSpec((None, None, None, SW_HEADS, SW_HEAD_DIM), lambda i, k: (i, gi, 1, 0, 0)),
                pl.BlockSpec((None, None, None, SW_HEADS, SW_HEAD_DIM), lambda i, k: (i, gi, 2, 0, 0)),
            ],
            out_specs=pl.BlockSpec((None, wb) + tail, lambda i, k: (i, k, 0, 0, 0)),
            out_shape=jax.ShapeDtypeStruct(c.shape, F32),
            compiler_params=_cparams(("arbitrary", "arbitrary")),
            name=f"cache_shift_g{gi}",
        )(c, c, q5, q5))
    return outs


def _kv_window_kernel(k_ref, v_ref, o_ref, *, rows):
    for kv, src in enumerate((k_ref, v_ref)):
        for h in range(SW_HEADS):
            o_ref[pl.ds(kv * SW_HEADS + h, rows, stride=2 * SW_HEADS), :] = \
                src[:, h * SW_HEAD_DIM:(h + 1) * SW_HEAD_DIM]


def _kv_window(qkv, gi, win, batch, seq):
    wlen = min(win, seq)
    rows = 128
    nt = wlen // rows
    per_row = 2 * SW_HEADS
    src = lambda which: pl.BlockSpec(
        (rows, SW_WIDTH), lambda b, t: ((b * seq + seq - wlen) // rows + t, gi * 3 + which))
    out = pl.pallas_call(
        functools.partial(_kv_window_kernel, rows=rows),
        grid=(batch, nt),
        in_specs=[src(1), src(2)],
        out_specs=pl.BlockSpec((None, rows * per_row, SW_HEAD_DIM), lambda b, t: (b, t, 0)),
        out_shape=jax.ShapeDtypeStruct((batch, wlen * per_row, SW_HEAD_DIM), F32),
        compiler_params=_cparams(("arbitrary", "arbitrary")),
        name=f"kv_window_g{gi}",
    )(qkv, qkv)
    return out.reshape(batch, wlen, 2, SW_HEADS, SW_HEAD_DIM)


def _router_kernel(x_ref, gn_ref, sc_ref, sh_ref, wr_ref, br_ref, *rest, tm):
    h_ref, ids_ref, gates_ref, hf_ref = rest[-4:]
    _norm_mod_into(x_ref, gn_ref, sc_ref, sh_ref, [hf_ref], tm)
    for s in range(ROW_TILES):
        h_ref[pl.ds(s, tm, stride=ROW_TILES), :] = hf_ref[:, s * LANES:(s + 1) * LANES]
    logits = jnp.dot(hf_ref[...].astype(BF16), wr_ref[...].astype(BF16),
                     preferred_element_type=F32) + br_ref[...]
    lane = lax.broadcasted_iota(jnp.int32, logits.shape, 1)
    big = jnp.int32(1 << 20)
    is_g = (lane >= MOE_EXPERTS) & (lane < MOE_EXPERTS + MOE_GROUPS)
    glog = jnp.where(is_g, logits, -jnp.inf)
    gmax = jnp.max(glog, axis=-1, keepdims=True)
    gsel = jnp.min(jnp.where(glog == gmax, lane - MOE_EXPERTS, big), axis=-1, keepdims=True)
    gsum = jnp.sum(jnp.where(is_g, jnp.exp(glog - gmax), 0.0), axis=-1, keepdims=True)
    pg = 1.0 / gsum
    in_grp = (lane < MOE_EXPERTS) & ((lane // MOE_PER_GROUP) == gsel)
    el = jnp.where(in_grp, logits, -jnp.inf)
    v1 = jnp.max(el, axis=-1, keepdims=True)
    i1 = jnp.min(jnp.where(el == v1, lane, big), axis=-1, keepdims=True)
    el2 = jnp.where(lane == i1, -jnp.inf, el)
    v2 = jnp.max(el2, axis=-1, keepdims=True)
    i2 = jnp.min(jnp.where(el2 == v2, lane, big), axis=-1, keepdims=True)
    e2 = jnp.exp(v2 - v1)
    den = 1.0 + e2
    ids_ref[...] = jnp.where(lane == 0, i1, jnp.where(lane == 1, i2, 0))
    gates_ref[...] = jnp.where(lane == 0, pg * (1.0 / den), jnp.where(lane == 1, pg * (e2 / den), 0.0))


def _router(x, gn, mod, wr, br, hcat, n_tok, *, tm, rows_per_batch, row0):
    m = x.shape[0]
    blk0 = row0 // tm
    bpb = max(rows_per_batch // tm, 1)
    nb = m // tm
    n_steps = nb + (1 if hcat is None and n_tok > m else 0)
    cl = lambda i: jnp.minimum(i, nb - 1)
    msp = lambda col: pl.BlockSpec((None, mod.shape[1], D_MODEL), lambda i: (cl(i) // bpb, 0, col))
    in_specs = [
        pl.BlockSpec((tm, D_MODEL), lambda i: (cl(i), 0)),
        pl.BlockSpec((1, D_MODEL), lambda i: (0, 0)),
        msp(4), msp(3),
        pl.BlockSpec((D_MODEL, LANES), lambda i: (0, 0)),
        pl.BlockSpec((1, LANES), lambda i: (0, 0)),
    ]
    args = [x, gn.reshape(1, D_MODEL), mod, mod, wr, br]
    aliases = {}
    if hcat is not None:
        in_specs.append(pl.BlockSpec(memory_space=pl.ANY))
        args.append(hcat)
        aliases = {len(args) - 1: 0}
    return pl.pallas_call(
        functools.partial(_router_kernel, tm=tm),
        grid=(n_steps,),
        in_specs=in_specs,
        out_specs=[pl.BlockSpec((tm * ROW_TILES, LANES), lambda i: (blk0 + i, 0)),
                   pl.BlockSpec((tm, LANES), lambda i: (cl(i), 0)),
                   pl.BlockSpec((tm, LANES), lambda i: (cl(i), 0))],
        out_shape=[jax.ShapeDtypeStruct((n_tok * ROW_TILES, LANES), F32),
                   jax.ShapeDtypeStruct((m, LANES), jnp.int32),
                   jax.ShapeDtypeStruct((m, LANES), F32)],
        scratch_shapes=[pltpu.VMEM((tm, D_MODEL), F32)],
        input_output_aliases=aliases,
        compiler_params=_cparams(("arbitrary",)),
        name="moe_router",
    )(*args)


def _gather_kernel(tok_ref, nused_ref, h_hbm, o_ref, buf_ref, sem):
    b = pl.program_id(0)
    n_used = nused_ref[0]
    rt = ROW_TILES

    def issue(blk, slot):
        def body(r, carry):
            tok = tok_ref[blk * MOE_ROWS + r]
            pltpu.make_async_copy(h_hbm.at[pl.ds(pl.multiple_of(tok * rt, rt), rt)],
                                  buf_ref.at[slot, pl.ds(pl.multiple_of(r * rt, rt), rt)],
                                  sem.at[slot]).start()
            return carry

        lax.fori_loop(0, MOE_ROWS, body, 0, unroll=8)

    @pl.when(b == 0)
    def _():
        issue(0, 0)

    @pl.when(b + 1 < n_used)
    def _():
        issue(b + 1, (b + 1) % 2)

    @pl.when(b < n_used)
    def _():
        slot = b % 2
        pltpu.make_async_copy(h_hbm.at[pl.ds(0, MOE_ROWS * rt)], buf_ref.at[slot], sem.at[slot]).wait()
        for s in range(rt):
            o_ref[:, s * LANES:(s + 1) * LANES] = \
                buf_ref[slot, pl.ds(s, MOE_ROWS, stride=rt), :].astype(o_ref.dtype)

    @pl.when(b >= n_used)
    def _():
        o_ref[...] = jnp.zeros_like(o_ref)


def _clamp_blk(b, nused_ref):
    return jnp.minimum(b, nused_ref[0] - 1)


def _moe_gather(hcat, row_tok, n_used, n_blocks):
    return pl.pallas_call(
        _gather_kernel,
        grid_spec=pltpu.PrefetchScalarGridSpec(
            num_scalar_prefetch=2,
            grid=(n_blocks,),
            in_specs=[pl.BlockSpec(memory_space=pl.ANY)],
            out_specs=pl.BlockSpec((MOE_ROWS, D_MODEL), lambda b, tok, nu: (b, 0)),
            scratch_shapes=[pltpu.VMEM((2, MOE_ROWS * ROW_TILES, LANES), F32), pltpu.SemaphoreType.DMA((2,))],
        ),
        out_shape=jax.ShapeDtypeStruct((n_blocks * MOE_ROWS, D_MODEL), BF16),
        compiler_params=pltpu.CompilerParams(dimension_semantics=("arbitrary",), vmem_limit_bytes=VMEM_LIMIT,
                                             disable_bounds_checks=True),
        name="moe_gather",
    )(row_tok, n_used, hcat)


def _new_expert(b, be_ref):
    return (b == 0) | (be_ref[b] != be_ref[jnp.maximum(b - 1, 0)])


def _gate_up_kernel(be_ref, nused_ref, x_ref, wg_ref, wu_ref, h_ref, wgb_ref, wub_ref):
    b = pl.program_id(1)

    @pl.when(b < nused_ref[0])
    def _():
        @pl.when(_new_expert(b, be_ref))
        def _():
            wgb_ref[...] = wg_ref[...].astype(BF16)
            wub_ref[...] = wu_ref[...].astype(BF16)

        x = x_ref[...]
        a = jnp.dot(x, wgb_ref[...], preferred_element_type=F32)
        u = jnp.dot(x, wub_ref[...], preferred_element_type=F32)
        h_ref[...] = (_silu(a) * u).astype(h_ref.dtype)

    @pl.when(b >= nused_ref[0])
    def _():
        h_ref[...] = jnp.zeros_like(h_ref)


def _moe_gate_up(xs, w_gu, blk_expert, n_used, n_blocks):
    nc = MOE_FF // MOE_FFC
    w_spec = lambda off: pl.BlockSpec(
        (None, D_MODEL, MOE_FFC), lambda c, b, be, nu: (be[_clamp_blk(b, nu)], 0, off * nc + c))
    return pl.pallas_call(
        _gate_up_kernel,
        grid_spec=pltpu.PrefetchScalarGridSpec(
            num_scalar_prefetch=2,
            grid=(nc, n_blocks),
            in_specs=[pl.BlockSpec((MOE_ROWS, D_MODEL), lambda c, b, be, nu: (_clamp_blk(b, nu), 0)),
                      w_spec(0), w_spec(1)],
            out_specs=pl.BlockSpec((MOE_ROWS, MOE_FFC), lambda c, b, be, nu: (b, c)),
            scratch_shapes=[pltpu.VMEM((D_MODEL, MOE_FFC), BF16)] * 2,
        ),
        out_shape=jax.ShapeDtypeStruct((n_blocks * MOE_ROWS, MOE_FF), BF16),
        compiler_params=_cparams(("arbitrary", "arbitrary")),
        name="moe_gate_up",
    )(blk_expert, n_used, xs, w_gu, w_gu)


def _down_kernel(be_ref, nused_ref, h_ref, wd_ref, y_ref, wdb_ref):
    b = pl.program_id(0)

    @pl.when(b < nused_ref[0])
    def _():
        @pl.when(_new_expert(b, be_ref))
        def _():
            wdb_ref[...] = wd_ref[...].astype(BF16)

        y = jnp.dot(h_ref[...], wdb_ref[...], preferred_element_type=F32)
        for s in range(ROW_TILES):
            y_ref[pl.ds(s, MOE_ROWS, stride=ROW_TILES), :] = y[:, s * LANES:(s + 1) * LANES]

    @pl.when(b >= nused_ref[0])
    def _():
        y_ref[...] = jnp.zeros_like(y_ref)


def _moe_down(hs, w_down, blk_expert, n_used, n_blocks):
    return pl.pallas_call(
        _down_kernel,
        grid_spec=pltpu.PrefetchScalarGridSpec(
            num_scalar_prefetch=2,
            grid=(n_blocks,),
            in_specs=[pl.BlockSpec((MOE_ROWS, MOE_FF), lambda b, be, nu: (_clamp_blk(b, nu), 0)),
                      pl.BlockSpec((None, MOE_FF, D_MODEL), lambda b, be, nu: (be[_clamp_blk(b, nu)], 0, 0))],
            out_specs=pl.BlockSpec((MOE_ROWS * ROW_TILES, LANES), lambda b, be, nu: (b, 0)),
            scratch_shapes=[pltpu.VMEM((MOE_FF, D_MODEL), BF16)],
        ),
        out_shape=jax.ShapeDtypeStruct((n_blocks * MOE_ROWS * ROW_TILES, LANES), F32),
        compiler_params=_cparams(("arbitrary",)),
        name="moe_down",
    )(blk_expert, n_used, hs, w_down)


def _combine_kernel(dest_ref, y_hbm, x_ref, g_ref, gates_ref, o_ref, y0_ref, y1_ref, sem, *, tm, tok0):
    i = pl.program_id(0)
    rt = ROW_TILES

    def issue(tile, slot):
        def body(r, carry):
            a = (tok0 + tile * tm + r) * MOE_TOP_K
            for j, y_ref in enumerate((y0_ref, y1_ref)):
                pltpu.make_async_copy(y_hbm.at[pl.ds(pl.multiple_of(dest_ref[a + j] * rt, rt), rt)],
                                      y_ref.at[slot, pl.ds(pl.multiple_of(r * rt, rt), rt)],
                                      sem.at[slot]).start()
            return carry

        lax.fori_loop(0, tm, body, 0, unroll=4)

    @pl.when(i == 0)
    def _():
        issue(0, 0)

    @pl.when(i + 1 < pl.num_programs(0))
    def _():
        issue(i + 1, (i + 1) % 2)

    slot = i % 2
    for y_ref in (y0_ref, y1_ref):
        pltpu.make_async_copy(y_hbm.at[pl.ds(0, tm * rt)], y_ref.at[slot], sem.at[slot]).wait()
    gt = gates_ref[...]
    g0 = jnp.broadcast_to(gt[:, 0:1], (tm, LANES))
    g1 = jnp.broadcast_to(gt[:, 1:2], (tm, LANES))
    for s in range(rt):
        cs = slice(s * LANES, (s + 1) * LANES)
        rows = pl.ds(s, tm, stride=rt)
        y = y0_ref[slot, rows, :] * g0 + y1_ref[slot, rows, :] * g1
        o_ref[:, cs] = x_ref[:, cs] + g_ref[:, cs] * y


def _moe_combine(y, dest, x, mod, gates, *, tm, rows_per_batch, tok0):
    m = x.shape[0]
    bpb = max(rows_per_batch // tm, 1)
    return pl.pallas_call(
        functools.partial(_combine_kernel, tm=tm, tok0=tok0),
        grid_spec=pltpu.PrefetchScalarGridSpec(
            num_scalar_prefetch=1,
            grid=(m // tm,),
            in_specs=[pl.BlockSpec(memory_space=pl.ANY),
                      pl.BlockSpec((tm, D_MODEL), lambda i, d: (i, 0)),
                      pl.BlockSpec((None, mod.shape[1], D_MODEL), lambda i, d: (i // bpb, 0, 5)),
                      pl.BlockSpec((tm, LANES), lambda i, d: (i, 0))],
            out_specs=pl.BlockSpec((tm, D_MODEL), lambda i, d: (i, 0)),
            scratch_shapes=[pltpu.VMEM((2, tm * ROW_TILES, LANES), F32)] * 2
            + [pltpu.SemaphoreType.DMA((2,))],
        ),
        out_shape=jax.ShapeDtypeStruct((m, D_MODEL), F32),
        compiler_params=pltpu.CompilerParams(dimension_semantics=("arbitrary",), vmem_limit_bytes=VMEM_LIMIT,
                                             disable_bounds_checks=True),
        name="moe_combine",
    )(dest, y, x, mod, gates)


def _moe_layer(xp, xs, mod_p, mod_s, gn, w_group, b_group, w_router, b_router, w_gu, w_down, seq):
    n_p, n_s = xp.shape[0], xs.shape[0]
    n_tok = n_p + n_s
    pad = LANES - MOE_EXPERTS - MOE_GROUPS
    wr = jnp.concatenate([w_router, w_group, jnp.zeros((D_MODEL, pad), F32)], axis=1)
    br = jnp.concatenate([b_router, b_group, jnp.zeros((pad,), F32)]).reshape(1, LANES)
    hcat, ids_p, gates_p = _router(xp, gn, mod_p, wr, br, None, n_tok, tm=256, rows_per_batch=seq, row0=0)
    hcat, ids_s, gates_s = _router(xs, gn, mod_s, wr, br, hcat, n_tok, tm=n_s, rows_per_batch=n_s, row0=n_p)

    expert = jnp.concatenate([ids_p[:, :MOE_TOP_K], ids_s[:, :MOE_TOP_K]], axis=0).reshape(-1)
    n_assign = n_tok * MOE_TOP_K
    n_blocks = -(-n_assign // MOE_ROWS) + MOE_EXPERTS
    onehot = (expert[:, None] == jnp.arange(MOE_EXPERTS, dtype=jnp.int32)[None, :]).astype(jnp.int32)
    csum = jnp.cumsum(onehot, axis=0)
    rank = jnp.sum(csum * onehot, axis=1) - 1
    counts = csum[-1]
    padded = (counts + MOE_ROWS - 1) // MOE_ROWS * MOE_ROWS
    pend = jnp.cumsum(padded)
    pstart = pend - padded
    dest = (pstart[expert] + rank).astype(jnp.int32)
    tok = jnp.arange(n_assign, dtype=jnp.int32) // MOE_TOP_K
    row_tok = jnp.zeros((n_blocks * MOE_ROWS,), jnp.int32).at[dest].set(tok)
    blk_expert = jnp.minimum(
        jnp.searchsorted(pend, jnp.arange(n_blocks, dtype=jnp.int32) * MOE_ROWS, side="right"),
        MOE_EXPERTS - 1).astype(jnp.int32)
    n_used = (pend[-1:] // MOE_ROWS).astype(jnp.int32)

    xs_sorted = _moe_gather(hcat, row_tok, n_used, n_blocks)
    hs = _moe_gate_up(xs_sorted, w_gu, blk_expert, n_used, n_blocks)
    y = _moe_down(hs, w_down, blk_expert, n_used, n_blocks)
    xp_new = _moe_combine(y, dest, xp, mod_p, gates_p, tm=256, rows_per_batch=seq, tok0=0)
    xs_new = _moe_combine(y, dest, xs, mod_s, gates_s, tm=n_s, rows_per_batch=n_s, tok0=n_p)
    return xp_new, xs_new


def _final_norm_kernel(x_ref, g_ref, o_ref):
    x = x_ref[...]
    inv = lax.rsqrt(jnp.mean(x * x, axis=-1, keepdims=True) + EPS)
    o_ref[...] = (x * inv) * g_ref[...]


def _final_norm(x, g, tm):
    m = x.shape[0]
    return pl.pallas_call(
        _final_norm_kernel,
        grid=(m // tm,),
        in_specs=[pl.BlockSpec((tm, D_MODEL), lambda i: (i, 0)), pl.BlockSpec((1, D_MODEL), lambda i: (0, 0))],
        out_specs=pl.BlockSpec((tm, D_MODEL), lambda i: (i, 0)),
        out_shape=jax.ShapeDtypeStruct((m, D_MODEL), F32),
        compiler_params=_cparams(("arbitrary",)),
        name="final_norm",
    )(x, g.reshape(1, D_MODEL))


def _rope_tables(pos):
    half = SW_HEAD_DIM // 2
    inv_freq = ROPE_THETA ** (-jnp.arange(half, dtype=F32) / half)
    ang = pos.astype(F32)[:, None] * inv_freq[None, :]
    cos, sin = jnp.cos(ang), jnp.sin(ang)
    return jnp.concatenate([cos, cos], axis=1), jnp.concatenate([-sin, sin], axis=1)


def kernel(x_prompt, x_sample, state_hgrn, state_conv, cache_swa_g1, cache_swa_g2, cache_swa_g3, c_prompt, c_sample, ada_w, ada_b, norm_mix, norm_ffn, norm_final, hg_w_in, hg_w_out, hg_norm, hg_lower, cv_w_in, cv_w_conv, cv_w_out, sw_w_in, sw_w_out, moe_w_group, moe_b_group, moe_w_router, moe_b_router, moe_w_gu, moe_w_down):
    bp, seq, d = x_prompt.shape
    bs = x_sample.shape[0]
    n_p = bp * seq
    xp = x_prompt.reshape(n_p, d)
    xs = x_sample.reshape(bs, d)

    mod_all = _ada_mod(jnp.concatenate([c_prompt, c_sample], axis=0), ada_w, ada_b)
    sm = jax.nn.softmax(hg_lower.astype(F32), axis=0)
    lower = jnp.cumsum(sm, axis=0) - sm[0]
    rope_p = _rope_tables(jnp.arange(seq))
    rope_s = _rope_tables(jnp.full((bs,), PAST_LEN, jnp.int32))

    tm_p, tn_p = 1024, 512
    hg_p, hg_s, cv_p, cv_s = [], [], [], []
    sw_p, sw_s = None, None
    for i in range(DEPTH):
        kind, j = i % N_MIXERS, i // N_MIXERS
        mod_p = mod_all[i, :bp].reshape(bp, 1, 6 * d)
        mod_s = mod_all[i, bp:].reshape(1, bs, 6 * d)
        pp = dict(tm=tm_p, tn=tn_p)
        ps = dict(tm=bs, tn=1024)
        hp = _norm_mod(xp, norm_mix[i], mod_p, 0, 1, tm=512, rows_per_batch=seq)
        hs = _norm_mod(xs, norm_mix[i], mod_s, 0, 1, tm=bs, rows_per_batch=bs)
        if kind == 0:
            proj_p = _mm(hp, hg_w_in[j], **pp)
            proj_s = _mm(hs, hg_w_in[j], **ps)
            lhs_p, st_p = _gla_prompt(proj_p, lower[j], hg_norm[j], bp, seq)
            lhs_s, st_s = _gla_step(proj_s, lower[j], hg_norm[j], state_hgrn[j])
            hg_p.append(st_p)
            hg_s.append(st_s)
            w_out = hg_w_out[j]
        elif kind == 1:
            proj_p = _mm(hp, cv_w_in[j], **pp)
            proj_s = _mm(hs, cv_w_in[j], **ps)
            lhs_p, tail_p = _conv_prompt(proj_p, cv_w_conv[j], bp, seq)
            lhs_s, tail_s = _conv_step(proj_s, cv_w_conv[j], state_conv[j])
            cv_p.append(tail_p)
            cv_s.append(tail_s)
            w_out = cv_w_out[j]
        else:
            qkv_p = _mm(hp, sw_w_in[j], tm=512, tn=SW_WIDTH, rope=rope_p)
            qkv_s = _mm(hs, sw_w_in[j], rope=rope_s, **ps)
            os_, ls_ = [], []
            for gi, (win, dil) in enumerate(SW_GROUPS):
                o_g, l_g = _swa_prompt_group(qkv_p, gi, dil, bp, seq)
                os_.append(o_g)
                ls_.append(l_g)
            lhs_p = _swa_merge(os_, ls_)
            caches = (cache_swa_g1[j], cache_swa_g2[j], cache_swa_g3[j])
            lhs_s = _swa_step(qkv_s, caches)
            w_out = sw_w_out[j]
            sw_p = [_kv_window(qkv_p, gi, win, bp, seq)[None] for gi, (win, _) in enumerate(SW_GROUPS)]
            sw_s = [c[None] for c in _cache_shift(qkv_s, caches)]
        xp = _mm(lhs_p, w_out, resgate=(xp, mod_p, 2, seq), **pp)
        xs = _mm(lhs_s, w_out, resgate=(xs, mod_s, 2, bs), **ps)
        xp, xs = _moe_layer(xp, xs, mod_p, mod_s, norm_ffn[i], moe_w_group[i], moe_b_group[i],
                            moe_w_router[i], moe_b_router[i], moe_w_gu[i], moe_w_down[i], seq)

    y_p = _final_norm(xp, norm_final, 512).reshape(bp, seq, d)
    y_s = _final_norm(xs, norm_final, bs).reshape(bs, 1, d)
    return (y_p, y_s, jnp.stack(hg_p), jnp.stack(hg_s), jnp.stack(cv_p), jnp.stack(cv_s),
            sw_p[0], sw_s[0], sw_p[1], sw_s[1], sw_p[2], sw_s[2])
```

```python
import functools

import jax
import jax.numpy as jnp
from jax import lax
from jax.experimental import pallas as pl
from jax.experimental.pallas import tpu as pltpu

F32 = jnp.float32
BF16 = jnp.bfloat16

D_MODEL = 2048
DEPTH = 4
N_MIXERS = 3
EPS = 1e-6
MASK_VALUE = -1e30
F_FLOOR = 1e-30
HG_DK = 128
HG_HEADS = D_MODEL // HG_DK
HG_CHUNK = 64
HG_SUB = 16
CONV_WIDTH = 3
SW_GROUPS = ((128, 1), (512, 4), (2048, 16))
SW_HEADS = 8
SW_HEAD_DIM = 128
SW_WIDTH = SW_HEADS * SW_HEAD_DIM
SW_BAND = 128
ROPE_THETA = 10000.0
MOE_GROUPS = 4
MOE_PER_GROUP = 8
MOE_EXPERTS = MOE_GROUPS * MOE_PER_GROUP
MOE_TOP_K = 2
MOE_FF = 1024
PAST_LEN = 16384

LANES = 128
SUBLANES = 8
VMEM_LIMIT = 52 * 1024 * 1024
MOE_ROWS = 256
MOE_FFC = 512
ROW_TILES = D_MODEL // LANES
DMA_UNROLL = 8
HIGHEST = lax.Precision.HIGHEST


def _cparams(sem):
    return pltpu.CompilerParams(dimension_semantics=sem, vmem_limit_bytes=VMEM_LIMIT)


def _sigmoid(x):
    return 1.0 / (1.0 + jnp.exp(-x))


def _silu(x):
    return x * _sigmoid(x)


def _bf16_round(x):
    return x.astype(BF16).astype(F32)


def _ada_kernel(c_ref, w_ref, b_ref, o_ref):
    c = c_ref[...]
    cond = _silu(c).astype(BF16)
    o_ref[...] = jnp.dot(cond, w_ref[...].astype(BF16), preferred_element_type=F32) + b_ref[...]


def _ada_mod(c_all, ada_w, ada_b):
    rows = c_all.shape[0]
    n = ada_w.shape[-1]
    tn = 1024
    return pl.pallas_call(
        _ada_kernel,
        grid=(DEPTH, n // tn),
        in_specs=[
            pl.BlockSpec((rows, D_MODEL), lambda l, j: (0, 0)),
            pl.BlockSpec((None, D_MODEL, tn), lambda l, j: (l, 0, j)),
            pl.BlockSpec((None, 1, tn), lambda l, j: (l, 0, j)),
        ],
        out_specs=pl.BlockSpec((None, rows, tn), lambda l, j: (l, 0, j)),
        out_shape=jax.ShapeDtypeStruct((DEPTH, rows, n), F32),
        compiler_params=_cparams(("arbitrary", "arbitrary")),
        name="ada_mod",
    )(c_all, ada_w, ada_b.reshape(DEPTH, 1, n))


def _norm_mod_rows(x, gn, sc, sh):
    inv = lax.rsqrt(jnp.mean(x * x, axis=-1, keepdims=True) + EPS)
    return (x * inv) * gn * (1.0 + sc) + sh


def _norm_mod_into(x_ref, gn_ref, sc_ref, sh_ref, dst_refs, tm):
    ch = min(tm, 128)
    per_row = sc_ref.shape[0] != 1

    def body(c, carry):
        rs = pl.ds(pl.multiple_of(c * ch, ch), ch)
        sc = sc_ref[rs, :] if per_row else sc_ref[...]
        sh = sh_ref[rs, :] if per_row else sh_ref[...]
        h = _norm_mod_rows(x_ref[rs, :], gn_ref[...], sc, sh)
        for d in dst_refs:
            d[rs, :] = h.astype(d.dtype)
        return carry

    lax.fori_loop(0, tm // ch, body, 0)


def _norm_mod_kernel(x_ref, gn_ref, sc_ref, sh_ref, o_ref, *, tm):
    _norm_mod_into(x_ref, gn_ref, sc_ref, sh_ref, [o_ref], tm)


def _norm_mod(x, gn, mod, shift_col, scale_col, *, tm, rows_per_batch):
    m = x.shape[0]
    bpb = max(rows_per_batch // tm, 1)
    msp = lambda col: pl.BlockSpec((None, mod.shape[1], D_MODEL), lambda i: (i // bpb, 0, col))
    return pl.pallas_call(
        functools.partial(_norm_mod_kernel, tm=tm),
        grid=(m // tm,),
        in_specs=[pl.BlockSpec((tm, D_MODEL), lambda i: (i, 0)),
                  pl.BlockSpec((1, D_MODEL), lambda i: (0, 0)),
                  msp(scale_col), msp(shift_col)],
        out_specs=pl.BlockSpec((tm, D_MODEL), lambda i: (i, 0)),
        out_shape=jax.ShapeDtypeStruct((m, D_MODEL), BF16),
        compiler_params=_cparams(("arbitrary",)),
        name="norm_mod",
    )(x, gn.reshape(1, D_MODEL), mod, mod)


def _rope_tile(acc, cos, sin_signed):
    outs = []
    for h in range(acc.shape[1] // SW_HEAD_DIM):
        xh = acc[:, h * SW_HEAD_DIM:(h + 1) * SW_HEAD_DIM]
        outs.append(xh * cos + pltpu.roll(xh, SW_HEAD_DIM // 2, 1) * sin_signed)
    return jnp.concatenate(outs, axis=1)


def _mm_kernel(x_ref, w_ref, *rest, epilogue):
    o_ref, wb_ref = rest[-2:]
    j, i = pl.program_id(0), pl.program_id(1)

    @pl.when(i == 0)
    def _():
        wb_ref[...] = w_ref[...].astype(BF16)

    acc = jnp.dot(x_ref[...].astype(BF16), wb_ref[...], preferred_element_type=F32)
    if epilogue == "rope":
        cos_ref, sin_ref = rest[:2]
        is_qk = (j % 3) < 2
        cos = jnp.where(is_qk, cos_ref[...], 1.0)
        sin = jnp.where(is_qk, sin_ref[...], 0.0)
        o_ref[...] = _rope_tile(acc, cos, sin)
    elif epilogue == "resgate":
        res_ref, g_ref = rest[:2]
        o_ref[...] = res_ref[...] + g_ref[...] * acc
    else:
        o_ref[...] = acc


def _mm(x, w, wl, *, tm, tn, rope=None, resgate=None):
    m, k = x.shape
    n = w.shape[2]
    in_specs = [pl.BlockSpec((tm, k), lambda j, i: (i, 0)),
                pl.BlockSpec((None, k, tn), lambda j, i: (wl, 0, j))]
    args = [x, w]
    epilogue = "none"
    if rope is not None:
        epilogue = "rope"
        nblk = rope[0].shape[0] // tm
        in_specs += [pl.BlockSpec((tm, SW_HEAD_DIM), lambda j, i: (i % nblk, 0))] * 2
        args += list(rope)
    if resgate is not None:
        epilogue = "resgate"
        res, mod, gate_col, rows_per_batch = resgate
        bpb = max(rows_per_batch // tm, 1)
        per = D_MODEL // tn
        in_specs += [pl.BlockSpec((tm, tn), lambda j, i: (i, j)),
                     pl.BlockSpec((None, mod.shape[1], tn), lambda j, i: (i // bpb, 0, gate_col * per + j))]
        args += [res, mod]
    return pl.pallas_call(
        functools.partial(_mm_kernel, epilogue=epilogue),
        grid=(n // tn, m // tm),
        in_specs=in_specs,
        out_specs=pl.BlockSpec((tm, tn), lambda j, i: (i, j)),
        out_shape=jax.ShapeDtypeStruct((m, n), F32),
        scratch_shapes=[pltpu.VMEM((k, tn), BF16)],
        compiler_params=_cparams(("arbitrary", "arbitrary")),
        name="mm_" + epilogue,
    )(*args)


def _hgrn_gates(qp, fp, lb):
    q = _silu(qp)
    f = lb + (1.0 - lb) * _sigmoid(fp)
    logf = jnp.log(jnp.maximum(f, F_FLOOR))
    k = (1.0 - lb) * _sigmoid(-fp)
    return q, logf, k


def _head_norm_gate(o, gn, gp):
    inv = lax.rsqrt(jnp.mean(o * o, axis=-1, keepdims=True) + EPS)
    return (o * inv) * gn * _silu(gp)


def _gla_kernel(q_ref, f_ref, i_ref, g_ref, lb_ref, gn_ref, o_ref, s_out_ref, st_ref, *, n_chunks, n_hb):
    c_len, n_sub = HG_CHUNK, HG_CHUNK // HG_SUB
    t = pl.program_id(2)

    @pl.when(t == 0)
    def _():
        st_ref[...] = jnp.zeros_like(st_ref)

    row = lax.broadcasted_iota(jnp.int32, (c_len, HG_DK), 0)
    r4 = lax.broadcasted_iota(jnp.int32, (c_len, n_sub * c_len), 0)
    c4 = lax.broadcasted_iota(jnp.int32, (c_len, n_sub * c_len), 1)
    mask4 = ((r4 // HG_SUB) == (c4 // c_len)) & ((c4 % c_len) <= r4)
    nt = (((1,), (1,)), ((), ()))
    tn = (((0,), (0,)), ((), ()))

    for c, hh in [(c, hh) for c in range(n_chunks) for hh in range(n_hb)]:
        rs = pl.ds(c * c_len, c_len)
        cs = slice(hh * HG_DK, (hh + 1) * HG_DK)
        lb, gn = lb_ref[:, cs], gn_ref[:, cs]
        q, logf, k = _hgrn_gates(q_ref[rs, cs], f_ref[rs, cs], lb)
        v = i_ref[rs, cs]
        g = logf
        sh = 1
        while sh < c_len:
            g = g + jnp.where(row >= sh, pltpu.roll(g, sh, 0), 0.0)
            sh *= 2
        refs = [g[HG_SUB * i + HG_SUB // 2 - 1:HG_SUB * i + HG_SUB // 2, :] for i in range(n_sub)]
        mrows = jnp.concatenate([jnp.broadcast_to(r, (HG_SUB, HG_DK)) for r in refs], axis=0)
        qh = (q * jnp.exp(g - mrows)).astype(BF16)
        khs = [k * jnp.exp(jnp.where(row < HG_SUB * (i + 1), refs[i] - g, -jnp.inf)) for i in range(n_sub)]
        kh = jnp.concatenate(khs, axis=0).astype(BF16)
        a4 = lax.dot_general(qh, kh, nt, preferred_element_type=F32)
        a4 = jnp.where(mask4, a4, 0.0).astype(BF16)
        vb = v.astype(BF16)
        v4 = jnp.concatenate([vb] * n_sub, axis=0)
        st = st_ref[hh]
        o = jnp.dot(a4, v4, preferred_element_type=F32)
        o = o + lax.dot_general((q * jnp.exp(g)).astype(BF16), st.astype(BF16), nt,
                                preferred_element_type=F32)
        glast = g[c_len - 1:c_len, :]
        kd = (k * jnp.exp(glast - g)).astype(BF16)
        st_ref[hh] = st * jnp.exp(glast) + lax.dot_general(vb, kd, tn, preferred_element_type=F32)
        o_ref[rs, cs] = _head_norm_gate(o, gn, g_ref[rs, cs]).astype(o_ref.dtype)

    @pl.when(t == pl.num_programs(2) - 1)
    def _():
        for hh in range(n_hb):
            s_out_ref[hh] = st_ref[hh].T


def _gla_prompt(proj, lower, g_norm, batch, seq):
    tb = 512
    n_hb = 2
    nt = seq // tb
    hblk = HG_HEADS // n_hb
    wcol = n_hb * HG_DK

    def col(off):
        return pl.BlockSpec((tb, wcol), lambda b, hh, t: (b * nt + t, off * hblk + hh))

    return pl.pallas_call(
        functools.partial(_gla_kernel, n_chunks=tb // HG_CHUNK, n_hb=n_hb),
        grid=(batch, hblk, nt),
        in_specs=[col(0), col(1), col(2), col(3),
                  pl.BlockSpec((1, wcol), lambda b, hh, t: (0, hh)),
                  pl.BlockSpec((1, wcol), lambda b, hh, t: (0, hh))],
        out_specs=[pl.BlockSpec((tb, wcol), lambda b, hh, t: (b * nt + t, hh)),
                   pl.BlockSpec((None, n_hb, HG_DK, HG_DK), lambda b, hh, t: (b, hh, 0, 0))],
        out_shape=[jax.ShapeDtypeStruct((batch * seq, D_MODEL), BF16),
                   jax.ShapeDtypeStruct((batch, HG_HEADS, HG_DK, HG_DK), F32)],
        scratch_shapes=[pltpu.VMEM((n_hb, HG_DK, HG_DK), F32)],
        compiler_params=_cparams(("arbitrary", "arbitrary", "arbitrary")),
        name="hgrn_prompt",
    )(proj, proj, proj, proj, lower.reshape(1, D_MODEL), g_norm.reshape(1, D_MODEL))


def _gla_step_kernel(qt_ref, ft_ref, i_ref, g_ref, lbt_ref, gn_ref, s_ref, o_ref, s_out_ref):
    for h in range(HG_HEADS):
        cs = slice(h * HG_DK, (h + 1) * HG_DK)
        q, logf, k = _hgrn_gates(qt_ref[:, h:h + 1], ft_ref[:, h:h + 1], lbt_ref[:, h:h + 1])
        dec = jnp.exp(logf)
        s_old = s_ref[h]
        v = i_ref[:, cs]
        s_out_ref[h] = dec * s_old + k * v
        qd = _bf16_round(jnp.broadcast_to(q * dec, s_old.shape))
        o = jnp.sum(_bf16_round(s_old) * qd, axis=0, keepdims=True) + jnp.sum(q * k, axis=0, keepdims=True) * v
        o_ref[:, cs] = _head_norm_gate(o, gn_ref[:, cs], g_ref[:, cs])


def _gla_step(proj, lower, g_norm, states, layer):
    b = proj.shape[0]
    h = HG_HEADS
    pq = proj[:, :D_MODEL].reshape(b, h, HG_DK).transpose(0, 2, 1)
    pf = proj[:, D_MODEL:2 * D_MODEL].reshape(b, h, HG_DK).transpose(0, 2, 1)
    p3 = proj.reshape(b, 1, 4 * D_MODEL)
    lbt = lower.reshape(h, HG_DK).T
    vec = lambda col: pl.BlockSpec((None, 1, D_MODEL), lambda i: (i, 0, col))
    tr = pl.BlockSpec((None, HG_DK, h), lambda i: (i, 0, 0))
    o, s_new = pl.pallas_call(
        _gla_step_kernel,
        grid=(b,),
        in_specs=[tr, tr, vec(2), vec(3),
                  pl.BlockSpec((HG_DK, h), lambda i: (0, 0)),
                  pl.BlockSpec((1, D_MODEL), lambda i: (0, 0)),
                  pl.BlockSpec((None, None, h, HG_DK, HG_DK), lambda i: (layer, i, 0, 0, 0))],
        out_specs=[pl.BlockSpec((None, 1, D_MODEL), lambda i: (i, 0, 0)),
                   pl.BlockSpec((None, h, HG_DK, HG_DK), lambda i: (i, 0, 0, 0))],
        out_shape=[jax.ShapeDtypeStruct((b, 1, D_MODEL), F32),
                   jax.ShapeDtypeStruct(states.shape[1:], F32)],
        compiler_params=_cparams(("arbitrary",)),
        name="hgrn_step",
    )(pq, pf, p3, p3, lbt, g_norm.reshape(1, D_MODEL), states)
    return o.reshape(b, D_MODEL), s_new


def _conv_kernel(bg_ref, cg_ref, u_ref, hc_ref, hu_ref, w_ref, o_ref, tail_ref, *, tm):
    t = pl.program_id(1)
    z = cg_ref[...] * u_ref[...]
    hz = hc_ref[...] * hu_ref[...]
    hz = jnp.where(t == 0, 0.0, hz)
    z1p, z2p = hz[SUBLANES - 1:SUBLANES, :], hz[SUBLANES - 2:SUBLANES - 1, :]
    row = lax.broadcasted_iota(jnp.int32, z.shape, 0)
    z1 = jnp.where(row == 0, z1p, pltpu.roll(z, 1, 0))
    z2 = jnp.where(row == 0, z2p, jnp.where(row == 1, z1p, pltpu.roll(z, 2, 0)))
    y = z2 * w_ref[0:1, :] + z1 * w_ref[1:2, :] + z * w_ref[2:3, :]
    o_ref[...] = (bg_ref[...] * y).astype(o_ref.dtype)
    tail_ref[...] = z[tm - (CONV_WIDTH - 1):, :]


def _conv_prompt(proj, w_conv, batch, seq):
    tm = 256
    nt = seq // tm
    hb = tm // SUBLANES
    blk = lambda col: pl.BlockSpec((tm, D_MODEL), lambda b, t: (b * nt + t, col))
    halo = lambda col: pl.BlockSpec(
        (SUBLANES, D_MODEL), lambda b, t: (jnp.maximum((b * nt + t) * hb - 1, 0), col))
    return pl.pallas_call(
        functools.partial(_conv_kernel, tm=tm),
        grid=(batch, nt),
        in_specs=[blk(0), blk(1), blk(2), halo(1), halo(2),
                  pl.BlockSpec((CONV_WIDTH, D_MODEL), lambda b, t: (0, 0))],
        out_specs=[pl.BlockSpec((tm, D_MODEL), lambda b, t: (b * nt + t, 0)),
                   pl.BlockSpec((None, CONV_WIDTH - 1, D_MODEL), lambda b, t: (b, 0, 0))],
        out_shape=[jax.ShapeDtypeStruct((batch * seq, D_MODEL), BF16),
                   jax.ShapeDtypeStruct((batch, CONV_WIDTH - 1, D_MODEL), F32)],
        compiler_params=_cparams(("arbitrary", "arbitrary")),
        name="conv_prompt",
    )(proj, proj, proj, proj, proj, w_conv)


def _conv_step_kernel(bg_ref, cg_ref, u_ref, buf_ref, w_ref, o_ref, nb_ref):
    z = cg_ref[...] * u_ref[...]
    buf = buf_ref[...]
    y = buf[:, 0:1, :] * w_ref[0:1, :] + buf[:, 1:2, :] * w_ref[1:2, :] + z * w_ref[2:3, :]
    o_ref[...] = bg_ref[...] * y
    nb_ref[:, 0:1, :] = buf[:, 1:2, :]
    nb_ref[:, 1:2, :] = z


def _conv_step(proj, w_conv, buf):
    b = proj.shape[0]
    p3 = proj.reshape(b, 1, 3 * D_MODEL)
    vec = lambda col: pl.BlockSpec((b, 1, D_MODEL), lambda i: (0, 0, col))
    o, nb = pl.pallas_call(
        _conv_step_kernel,
        grid=(1,),
        in_specs=[vec(0), vec(1), vec(2),
                  pl.BlockSpec((b, CONV_WIDTH - 1, D_MODEL), lambda i: (0, 0, 0)),
                  pl.BlockSpec((CONV_WIDTH, D_MODEL), lambda i: (0, 0))],
        out_specs=[pl.BlockSpec((b, 1, D_MODEL), lambda i: (0, 0, 0)),
                   pl.BlockSpec((b, CONV_WIDTH - 1, D_MODEL), lambda i: (0, 0, 0))],
        out_shape=[jax.ShapeDtypeStruct((b, 1, D_MODEL), F32),
                   jax.ShapeDtypeStruct(buf.shape, F32)],
        compiler_params=_cparams(("arbitrary",)),
        name="conv_step",
    )(p3, p3, p3, buf, w_conv)
    return o.reshape(b, D_MODEL), nb


def _swa_kernel(*refs, dil, n_units, n_heads, has_prev):
    if has_prev:
        qc_ref, kp_ref, kc_ref, vp_ref, vc_ref, o_ref, l_ref = refs
    else:
        qc_ref, kc_ref, vc_ref, o_ref, l_ref = refs
    blk = SW_BAND
    n = pl.program_id(1)
    ri = lax.broadcasted_iota(jnp.int32, (blk, blk), 0)
    ci = lax.broadcasted_iota(jnp.int32, (blk, blk), 1)
    m_cur = ci <= ri
    m_prev_tri = ci >= ri
    lane = lax.broadcasted_iota(jnp.int32, (blk, LANES), 1)
    scale = SW_HEAD_DIM ** -0.5
    nt = (((1,), (1,)), ((), ()))

    for u in range(n_units):
        if dil == 1:
            rows = pl.ds(u * blk, blk)
            prev_src = None
            if has_prev:
                prev_src = (kp_ref, vp_ref, pl.ds(0, blk), True) if u == 0 else \
                    (kc_ref, vc_ref, pl.ds((u - 1) * blk, blk), False)
        else:
            rows = pl.ds(u, blk, stride=dil)
            prev_src = (kp_ref, vp_ref, rows, True) if has_prev else None

        def head(h, lse_acc, rows=rows, prev_src=prev_src):
            cs = pl.ds(0, SW_HEAD_DIM) if n_heads == 1 else \
                pl.ds(pl.multiple_of(h * SW_HEAD_DIM, SW_HEAD_DIM), SW_HEAD_DIM)
            q = qc_ref[rows, cs].astype(BF16)
            s_c = lax.dot_general(q, kc_ref[rows, cs].astype(BF16), nt, preferred_element_type=F32) * scale
            s_c = jnp.where(m_cur, s_c, MASK_VALUE)
            m = jnp.max(s_c, axis=-1, keepdims=True)
            if prev_src is not None:
                kr, vr, prow, first_only = prev_src
                s_p = lax.dot_general(q, kr[prow, cs].astype(BF16), nt, preferred_element_type=F32) * scale
                m_prev = (m_prev_tri & (n > 0)) if first_only else m_prev_tri
                s_p = jnp.where(m_prev, s_p, MASK_VALUE)
                m = jnp.maximum(m, jnp.max(s_p, axis=-1, keepdims=True))
            p_c = jnp.exp(s_c - m)
            l = jnp.sum(p_c, axis=-1, keepdims=True)
            o = jnp.dot(p_c.astype(BF16), vc_ref[rows, cs].astype(BF16), preferred_element_type=F32)
            if prev_src is not None:
                p_p = jnp.exp(s_p - m)
                l = l + jnp.sum(p_p, axis=-1, keepdims=True)
                o = o + jnp.dot(p_p.astype(BF16), vr[prow, cs].astype(BF16), preferred_element_type=F32)
            o_ref[rows, cs] = o / l
            lse = m + jnp.log(l)
            return jnp.where(lane == h, lse, lse_acc)

        lse0 = jnp.zeros((blk, LANES), F32)
        l_ref[rows, :] = head(0, lse0) if n_heads == 1 else lax.fori_loop(0, n_heads, head, lse0)


def _swa_prompt_group(qkv, gi, dil, batch, seq):
    span = SW_BAND * dil
    if dil == 1:
        span, n_units = 512, 4
    else:
        n_units = dil
    nb = seq // span
    has_prev = nb > 1
    hb = SW_HEADS if dil == 1 else 1
    n_hblk = SW_HEADS // hb
    wcol = hb * SW_HEAD_DIM
    per = SW_WIDTH // wcol
    pspan = SW_BAND if dil == 1 else span
    pmul = span // pspan

    def cur(which):
        return pl.BlockSpec((span, wcol), lambda b, n, hh: (b * nb + n, (gi * 3 + which) * per + hh))

    def prev(which):
        return pl.BlockSpec(
            (pspan, wcol),
            lambda b, n, hh: (jnp.maximum((b * nb + n) * pmul - 1, 0), (gi * 3 + which) * per + hh))

    if has_prev:
        in_specs = [cur(0), prev(1), cur(1), prev(2), cur(2)]
        args = [qkv] * 5
    else:
        in_specs = [cur(0), cur(1), cur(2)]
        args = [qkv] * 3
    return pl.pallas_call(
        functools.partial(_swa_kernel, dil=dil, n_units=n_units, n_heads=hb, has_prev=has_prev),
        grid=(batch, nb, n_hblk),
        in_specs=in_specs,
        out_specs=[pl.BlockSpec((span, wcol), lambda b, n, hh: (b * nb + n, hh)),
                   pl.BlockSpec((span, LANES), lambda b, n, hh: (b * nb + n, hh))],
        out_shape=[jax.ShapeDtypeStruct((batch * seq, SW_WIDTH), F32),
                   jax.ShapeDtypeStruct((batch * seq, n_hblk * LANES), F32)],
        compiler_params=_cparams(("arbitrary", "arbitrary", "arbitrary")),
        name=f"swa_prompt_g{gi}",
    )(*args)


def _merge_heads(o_refs, l_refs, rs, hbs):
    outs = []
    for h in range(SW_HEADS):
        cs = slice(h * SW_HEAD_DIM, (h + 1) * SW_HEAD_DIM)
        lses = []
        for l_ref, hb in zip(l_refs, hbs):
            lane = (h // hb) * LANES + h % hb
            lses.append(l_ref[rs, lane:lane + 1])
        mx = jnp.maximum(jnp.maximum(lses[0], lses[1]), lses[2])
        es = [jnp.exp(l - mx) for l in lses]
        den = es[0] + es[1] + es[2]
        acc = None
        for e, o_ref in zip(es, o_refs):
            o = o_ref[rs, cs]
            term = _bf16_round(jnp.broadcast_to(e / den, o.shape)) * _bf16_round(o)
            acc = term if acc is None else acc + term
        outs.append(acc)
    return jnp.concatenate(outs, axis=1)


def _swa_merge_kernel(o1, o2, o3, l1, l2, l3, out_ref, *, tm, hbs):
    ch = 128

    def body(c, carry):
        rs = pl.ds(pl.multiple_of(c * ch, ch), ch)
        out_ref[rs, :] = _merge_heads((o1, o2, o3), (l1, l2, l3), rs, hbs).astype(out_ref.dtype)
        return carry

    lax.fori_loop(0, tm // ch, body, 0)


def _swa_merge(os_, ls_):
    tm = 512
    m = os_[0].shape[0]
    hbs = tuple(SW_HEADS // (l.shape[1] // LANES) for l in ls_)
    row = lambda width: pl.BlockSpec((tm, width), lambda i: (i, 0))
    return pl.pallas_call(
        functools.partial(_swa_merge_kernel, tm=tm, hbs=hbs),
        grid=(m // tm,),
        in_specs=[row(SW_WIDTH)] * 3 + [row(l.shape[1]) for l in ls_],
        out_specs=row(SW_WIDTH),
        out_shape=jax.ShapeDtypeStruct((m, SW_WIDTH), BF16),
        compiler_params=_cparams(("arbitrary",)),
        name="swa_merge",
    )(*os_, *ls_)


def _swa_step_kernel(qkv_ref, c1_ref, c2_ref, c3_ref, o_ref):
    scale = SW_HEAD_DIM ** -0.5
    outs, lses = [], []
    for gi, c_ref in enumerate((c1_ref, c2_ref, c3_ref)):
        q = _bf16_round(qkv_ref[gi, 0])
        kn = _bf16_round(qkv_ref[gi, 1])
        vn = _bf16_round(qkv_ref[gi, 2])
        kc = _bf16_round(c_ref[:, 0])
        vc = _bf16_round(c_ref[:, 1])
        s = jnp.sum(kc * q[None], axis=-1, keepdims=True) * scale
        sn = jnp.sum(kn * q, axis=-1, keepdims=True) * scale
        m = jnp.maximum(jnp.max(s, axis=0), sn)
        lse = m + jnp.log(jnp.sum(jnp.exp(s - m[None]), axis=0) + jnp.exp(sn - m))
        shape = vc.shape
        p = _bf16_round(jnp.broadcast_to(jnp.exp(s - lse[None]), shape))
        pn = _bf16_round(jnp.broadcast_to(jnp.exp(sn - lse), shape[1:]))
        outs.append(jnp.sum(p * vc, axis=0) + pn * vn)
        lses.append(lse)
    mx = jnp.maximum(jnp.maximum(lses[0], lses[1]), lses[2])
    es = [jnp.exp(l - mx) for l in lses]
    den = es[0] + es[1] + es[2]
    acc = None
    for e, o in zip(es, outs):
        term = _bf16_round(jnp.broadcast_to(e / den, o.shape)) * _bf16_round(o)
        acc = term if acc is None else acc + term
    o_ref[...] = acc


def _swa_step(qkv, caches):
    b = qkv.shape[0]
    q5 = qkv.reshape(b, len(SW_GROUPS), 3, SW_HEADS, SW_HEAD_DIM)
    views, specs = [], []
    for c, (win, dil) in zip(caches, SW_GROUPS):
        views.append(c.reshape(b, win // dil, dil, 2, SW_HEADS, SW_HEAD_DIM))
        specs.append(pl.BlockSpec((None, win // dil, None, 2, SW_HEADS, SW_HEAD_DIM),
                                  lambda i: (i, 0, 0, 0, 0, 0)))
    o = pl.pallas_call(
        _swa_step_kernel,
        grid=(b,),
        in_specs=[pl.BlockSpec((None, len(SW_GROUPS), 3, SW_HEADS, SW_HEAD_DIM),
                               lambda i: (i, 0, 0, 0, 0))] + specs,
        out_specs=pl.BlockSpec((None, SW_HEADS, SW_HEAD_DIM), lambda i: (i, 0, 0)),
        out_shape=jax.ShapeDtypeStruct((b, SW_HEADS, SW_HEAD_DIM), F32),
        compiler_params=_cparams(("arbitrary",)),
        name="swa_step",
    )(q5, *views)
    return o.reshape(b, SW_WIDTH)


def _cache_shift_kernel(cur_ref, nxt_ref, knew_ref, vnew_ref, o_ref, *, wb):
    k = pl.program_id(1)
    last = pl.num_programs(1) - 1

    def row(i, carry):
        o_ref[i] = cur_ref[i + 1]
        return carry

    lax.fori_loop(0, wb - 1, row, 0, unroll=8)

    @pl.when(k < last)
    def _():
        o_ref[wb - 1] = nxt_ref[0]

    @pl.when(k == last)
    def _():
        o_ref[wb - 1, 0] = knew_ref[...]
        o_ref[wb - 1, 1] = vnew_ref[...]


def _cache_shift(qkv, caches):
    b = qkv.shape[0]
    q5 = qkv.reshape(b, len(SW_GROUPS), 3, SW_HEADS, SW_HEAD_DIM)
    outs = []
    for gi, c in enumerate(caches):
        w = c.shape[1]
        wb = min(w, 512)
        tail = (2, SW_HEADS, SW_HEAD_DIM)
        outs.append(pl.pallas_call(
            functools.partial(_cache_shift_kernel, wb=wb),
            grid=(b, w // wb),
            in_specs=[
                pl.BlockSpec((None, wb) + tail, lambda i, k: (i, k, 0, 0, 0)),
                pl.BlockSpec((None, 1) + tail, lambda i, k: (i, jnp.minimum((k + 1) * wb, w - 1), 0, 0, 0)),
                pl.BlockSpec((None, None, None, SW_HEADS, SW_HEAD_DIM), lambda i, k: (i, gi, 1, 0, 0)),
                pl.BlockSpec((None, None, None, SW_HEADS, SW_HEAD_DIM), lambda i, k: (i, gi, 2, 0, 0)),
            ],
            out_specs=pl.BlockSpec((None, wb) + tail, lambda i, k: (i, k, 0, 0, 0)),
            out_shape=jax.ShapeDtypeStruct(c.shape, F32),
            compiler_params=_cparams(("arbitrary", "arbitrary")),
            name=f"cache_shift_g{gi}",
        )(c, c, q5, q5))
    return outs


def _kv_window_kernel(k_ref, v_ref, o_ref, *, rows):
    for kv, src in enumerate((k_ref, v_ref)):
        for h in range(SW_HEADS):
            o_ref[pl.ds(kv * SW_HEADS + h, rows, stride=2 * SW_HEADS), :] = \
                src[:, h * SW_HEAD_DIM:(h + 1) * SW_HEAD_DIM]


def _kv_window(qkv, gi, win, batch, seq):
    wlen = min(win, seq)
    rows = 128
    nt = wlen // rows
    per_row = 2 * SW_HEADS
    src = lambda which: pl.BlockSpec(
        (rows, SW_WIDTH), lambda b, t: ((b * seq + seq - wlen) // rows + t, gi * 3 + which))
    out = pl.pallas_call(
        functools.partial(_kv_window_kernel, rows=rows),
        grid=(batch, nt),
        in_specs=[src(1), src(2)],
        out_specs=pl.BlockSpec((None, rows * per_row, SW_HEAD_DIM), lambda b, t: (b, t, 0)),
        out_shape=jax.ShapeDtypeStruct((batch, wlen * per_row, SW_HEAD_DIM), F32),
        compiler_params=_cparams(("arbitrary", "arbitrary")),
        name=f"kv_window_g{gi}",
    )(qkv, qkv)
    return out.reshape(batch, wlen, 2, SW_HEADS, SW_HEAD_DIM)


def _router_kernel(x_ref, gn_ref, sc_ref, sh_ref, wr_ref, br_ref, *rest, tm):
    h_ref, ids_ref, gates_ref, hf_ref = rest[-4:]
    _norm_mod_into(x_ref, gn_ref, sc_ref, sh_ref, [hf_ref], tm)
    for s in range(ROW_TILES):
        h_ref[pl.ds(s, tm, stride=ROW_TILES), :] = hf_ref[:, s * LANES:(s + 1) * LANES]
    logits = jnp.dot(hf_ref[...].astype(BF16), wr_ref[...].astype(BF16),
                     preferred_element_type=F32) + br_ref[...]
    lane = lax.broadcasted_iota(jnp.int32, logits.shape, 1)
    big = jnp.int32(1 << 20)
    is_g = (lane >= MOE_EXPERTS) & (lane < MOE_EXPERTS + MOE_GROUPS)
    glog = jnp.where(is_g, logits, -jnp.inf)
    gmax = jnp.max(glog, axis=-1, keepdims=True)
    gsel = jnp.min(jnp.where(glog == gmax, lane - MOE_EXPERTS, big), axis=-1, keepdims=True)
    gsum = jnp.sum(jnp.where(is_g, jnp.exp(glog - gmax), 0.0), axis=-1, keepdims=True)
    pg = 1.0 / gsum
    in_grp = (lane < MOE_EXPERTS) & ((lane // MOE_PER_GROUP) == gsel)
    el = jnp.where(in_grp, logits, -jnp.inf)
    v1 = jnp.max(el, axis=-1, keepdims=True)
    i1 = jnp.min(jnp.where(el == v1, lane, big), axis=-1, keepdims=True)
    el2 = jnp.where(lane == i1, -jnp.inf, el)
    v2 = jnp.max(el2, axis=-1, keepdims=True)
    i2 = jnp.min(jnp.where(el2 == v2, lane, big), axis=-1, keepdims=True)
    e2 = jnp.exp(v2 - v1)
    den = 1.0 + e2
    ids_ref[...] = jnp.where(lane == 0, i1, jnp.where(lane == 1, i2, 0))
    gates_ref[...] = jnp.where(lane == 0, pg * (1.0 / den), jnp.where(lane == 1, pg * (e2 / den), 0.0))


def _router(x, gn, mod, wr, br, hcat, n_tok, *, tm, rows_per_batch, row0):
    m = x.shape[0]
    blk0 = row0 // tm
    bpb = max(rows_per_batch // tm, 1)
    nb = m // tm
    n_steps = nb + (1 if hcat is None and n_tok > m else 0)
    cl = lambda i: jnp.minimum(i, nb - 1)
    msp = lambda col: pl.BlockSpec((None, mod.shape[1], D_MODEL), lambda i: (cl(i) // bpb, 0, col))
    in_specs = [
        pl.BlockSpec((tm, D_MODEL), lambda i: (cl(i), 0)),
        pl.BlockSpec((1, D_MODEL), lambda i: (0, 0)),
        msp(4), msp(3),
        pl.BlockSpec((D_MODEL, LANES), lambda i: (0, 0)),
        pl.BlockSpec((1, LANES), lambda i: (0, 0)),
    ]
    args = [x, gn.reshape(1, D_MODEL), mod, mod, wr, br]
    aliases = {}
    if hcat is not None:
        in_specs.append(pl.BlockSpec(memory_space=pl.ANY))
        args.append(hcat)
        aliases = {len(args) - 1: 0}
    return pl.pallas_call(
        functools.partial(_router_kernel, tm=tm),
        grid=(n_steps,),
        in_specs=in_specs,
        out_specs=[pl.BlockSpec((tm * ROW_TILES, LANES), lambda i: (blk0 + i, 0)),
                   pl.BlockSpec((tm, LANES), lambda i: (cl(i), 0)),
                   pl.BlockSpec((tm, LANES), lambda i: (cl(i), 0))],
        out_shape=[jax.ShapeDtypeStruct((n_tok * ROW_TILES, LANES), F32),
                   jax.ShapeDtypeStruct((m, LANES), jnp.int32),
                   jax.ShapeDtypeStruct((m, LANES), F32)],
        scratch_shapes=[pltpu.VMEM((tm, D_MODEL), F32)],
        input_output_aliases=aliases,
        compiler_params=_cparams(("arbitrary",)),
        name="moe_router",
    )(*args)


def _gather_kernel(tok_ref, nused_ref, h_hbm, o_ref, buf_ref, sem):
    b = pl.program_id(0)
    n_used = nused_ref[0]
    rt = ROW_TILES

    def issue(blk, slot):
        def body(g, carry):
            for u in range(DMA_UNROLL):
                r = g * DMA_UNROLL + u
                tok = tok_ref[blk * MOE_ROWS + r]
                pltpu.make_async_copy(h_hbm.at[pl.ds(pl.multiple_of(tok * rt, rt), rt)],
                                      buf_ref.at[slot, pl.ds(pl.multiple_of(r * rt, rt), rt)],
                                      sem.at[slot]).start(priority=u % 2)
            return carry

        lax.fori_loop(0, MOE_ROWS // DMA_UNROLL, body, 0)

    @pl.when(b == 0)
    def _():
        issue(0, 0)

    @pl.when(b + 1 < n_used)
    def _():
        issue(b + 1, (b + 1) % 2)

    @pl.when(b < n_used)
    def _():
        slot = b % 2
        pltpu.make_async_copy(h_hbm.at[pl.ds(0, MOE_ROWS * rt)], buf_ref.at[slot], sem.at[slot]).wait()
        for s in range(rt):
            o_ref[:, s * LANES:(s + 1) * LANES] = \
                buf_ref[slot, pl.ds(s, MOE_ROWS, stride=rt), :].astype(o_ref.dtype)

    @pl.when(b >= n_used)
    def _():
        o_ref[...] = jnp.zeros_like(o_ref)


def _clamp_blk(b, nused_ref):
    return jnp.minimum(b, nused_ref[0] - 1)


def _moe_gather(hcat, row_tok, n_used, n_blocks):
    return pl.pallas_call(
        _gather_kernel,
        grid_spec=pltpu.PrefetchScalarGridSpec(
            num_scalar_prefetch=2,
            grid=(n_blocks,),
            in_specs=[pl.BlockSpec(memory_space=pl.ANY)],
            out_specs=pl.BlockSpec((MOE_ROWS, D_MODEL), lambda b, tok, nu: (b, 0)),
            scratch_shapes=[pltpu.VMEM((2, MOE_ROWS * ROW_TILES, LANES), F32), pltpu.SemaphoreType.DMA((2,))],
        ),
        out_shape=jax.ShapeDtypeStruct((n_blocks * MOE_ROWS, D_MODEL), BF16),
        compiler_params=pltpu.CompilerParams(dimension_semantics=("arbitrary",), vmem_limit_bytes=VMEM_LIMIT,
                                             disable_bounds_checks=True),
        name="moe_gather",
    )(row_tok, n_used, hcat)


def _new_expert(b, be_ref):
    return (b == 0) | (be_ref[b] != be_ref[jnp.maximum(b - 1, 0)])


def _gate_up_kernel(be_ref, wi_ref, nused_ref, x_ref, wg_ref, wu_ref, h_ref, wgb_ref, wub_ref):
    b = pl.program_id(1)

    @pl.when(b < nused_ref[0])
    def _():
        @pl.when(_new_expert(b, be_ref))
        def _():
            wgb_ref[...] = wg_ref[...].astype(BF16)
            wub_ref[...] = wu_ref[...].astype(BF16)

        x = x_ref[...]
        a = jnp.dot(x, wgb_ref[...], preferred_element_type=F32)
        u = jnp.dot(x, wub_ref[...], preferred_element_type=F32)
        h_ref[...] = (_silu(a) * u).astype(h_ref.dtype)

    @pl.when(b >= nused_ref[0])
    def _():
        h_ref[...] = jnp.zeros_like(h_ref)


def _moe_gate_up(xs, w_gu, layer, blk_expert, w_idx, n_used, n_blocks):
    nc = MOE_FF // MOE_FFC
    w_spec = lambda off: pl.BlockSpec(
        (None, None, D_MODEL, MOE_FFC),
        lambda c, b, be, wi, nu: (layer, wi[_clamp_blk(b, nu)], 0, off * nc + c))
    return pl.pallas_call(
        _gate_up_kernel,
        grid_spec=pltpu.PrefetchScalarGridSpec(
            num_scalar_prefetch=3,
            grid=(nc, n_blocks),
            in_specs=[pl.BlockSpec((MOE_ROWS, D_MODEL), lambda c, b, be, wi, nu: (_clamp_blk(b, nu), 0)),
                      w_spec(0), w_spec(1)],
            out_specs=pl.BlockSpec((MOE_ROWS, MOE_FFC), lambda c, b, be, wi, nu: (b, c)),
            scratch_shapes=[pltpu.VMEM((D_MODEL, MOE_FFC), BF16)] * 2,
        ),
        out_shape=jax.ShapeDtypeStruct((n_blocks * MOE_ROWS, MOE_FF), BF16),
        compiler_params=_cparams(("arbitrary", "arbitrary")),
        name="moe_gate_up",
    )(blk_expert, w_idx, n_used, xs, w_gu, w_gu)


def _down_kernel(be_ref, wi_ref, nused_ref, h_ref, wd_ref, y_ref, wdb_ref):
    b = pl.program_id(0)

    @pl.when(b < nused_ref[0])
    def _():
        @pl.when(_new_expert(b, be_ref))
        def _():
            wdb_ref[...] = wd_ref[...].astype(BF16)

        y = jnp.dot(h_ref[...], wdb_ref[...], preferred_element_type=F32)
        for s in range(ROW_TILES):
            y_ref[pl.ds(s, MOE_ROWS, stride=ROW_TILES), :] = y[:, s * LANES:(s + 1) * LANES]

    @pl.when(b >= nused_ref[0])
    def _():
        y_ref[...] = jnp.zeros_like(y_ref)


def _moe_down(hs, w_down, layer, blk_expert, w_idx, n_used, n_blocks):
    return pl.pallas_call(
        _down_kernel,
        grid_spec=pltpu.PrefetchScalarGridSpec(
            num_scalar_prefetch=3,
            grid=(n_blocks,),
            in_specs=[pl.BlockSpec((MOE_ROWS, MOE_FF), lambda b, be, wi, nu: (_clamp_blk(b, nu), 0)),
                      pl.BlockSpec((None, None, MOE_FF, D_MODEL),
                                   lambda b, be, wi, nu: (layer, wi[_clamp_blk(b, nu)], 0, 0))],
            out_specs=pl.BlockSpec((MOE_ROWS * ROW_TILES, LANES), lambda b, be, wi, nu: (b, 0)),
            scratch_shapes=[pltpu.VMEM((MOE_FF, D_MODEL), BF16)],
        ),
        out_shape=jax.ShapeDtypeStruct((n_blocks * MOE_ROWS * ROW_TILES, LANES), F32),
        compiler_params=_cparams(("arbitrary",)),
        name="moe_down",
    )(blk_expert, w_idx, n_used, hs, w_down)


def _combine_kernel(dest_ref, y_hbm, x_ref, g_ref, gates_ref, o_ref, y0_ref, y1_ref, sem, *, tm, tok0):
    i = pl.program_id(0)
    rt = ROW_TILES

    def issue(tile, slot):
        def body(g, carry):
            for u in range(DMA_UNROLL // 2):
                r = g * (DMA_UNROLL // 2) + u
                a = (tok0 + tile * tm + r) * MOE_TOP_K
                for j, y_ref in enumerate((y0_ref, y1_ref)):
                    pltpu.make_async_copy(y_hbm.at[pl.ds(pl.multiple_of(dest_ref[a + j] * rt, rt), rt)],
                                          y_ref.at[slot, pl.ds(pl.multiple_of(r * rt, rt), rt)],
                                          sem.at[slot]).start(priority=j)
            return carry

        lax.fori_loop(0, tm // (DMA_UNROLL // 2), body, 0)

    @pl.when(i == 0)
    def _():
        issue(0, 0)

    @pl.when(i + 1 < pl.num_programs(0))
    def _():
        issue(i + 1, (i + 1) % 2)

    slot = i % 2
    for y_ref in (y0_ref, y1_ref):
        pltpu.make_async_copy(y_hbm.at[pl.ds(0, tm * rt)], y_ref.at[slot], sem.at[slot]).wait()
    gt = gates_ref[...]
    g0 = jnp.broadcast_to(gt[:, 0:1], (tm, LANES))
    g1 = jnp.broadcast_to(gt[:, 1:2], (tm, LANES))
    for s in range(rt):
        cs = slice(s * LANES, (s + 1) * LANES)
        rows = pl.ds(s, tm, stride=rt)
        y = y0_ref[slot, rows, :] * g0 + y1_ref[slot, rows, :] * g1
        o_ref[:, cs] = x_ref[:, cs] + g_ref[:, cs] * y


def _moe_combine(y, dest, x, mod, gates, *, tm, rows_per_batch, tok0):
    m = x.shape[0]
    bpb = max(rows_per_batch // tm, 1)
    return pl.pallas_call(
        functools.partial(_combine_kernel, tm=tm, tok0=tok0),
        grid_spec=pltpu.PrefetchScalarGridSpec(
            num_scalar_prefetch=1,
            grid=(m // tm,),
            in_specs=[pl.BlockSpec(memory_space=pl.ANY),
                      pl.BlockSpec((tm, D_MODEL), lambda i, d: (i, 0)),
                      pl.BlockSpec((None, mod.shape[1], D_MODEL), lambda i, d: (i // bpb, 0, 5)),
                      pl.BlockSpec((tm, LANES), lambda i, d: (i, 0))],
            out_specs=pl.BlockSpec((tm, D_MODEL), lambda i, d: (i, 0)),
            scratch_shapes=[pltpu.VMEM((2, tm * ROW_TILES, LANES), F32)] * 2
            + [pltpu.SemaphoreType.DMA((2,))],
        ),
        out_shape=jax.ShapeDtypeStruct((m, D_MODEL), F32),
        compiler_params=pltpu.CompilerParams(dimension_semantics=("arbitrary",), vmem_limit_bytes=VMEM_LIMIT,
                                             disable_bounds_checks=True),
        name="moe_combine",
    )(dest, y, x, mod, gates)


def _moe_layer(xp, xs, mod_p, mod_s, gn, w_group, b_group, w_router, b_router, w_gu, w_down, layer, seq):
    n_p, n_s = xp.shape[0], xs.shape[0]
    n_tok = n_p + n_s
    pad = LANES - MOE_EXPERTS - MOE_GROUPS
    wr = jnp.concatenate([w_router, w_group, jnp.zeros((D_MODEL, pad), F32)], axis=1)
    br = jnp.concatenate([b_router, b_group, jnp.zeros((pad,), F32)]).reshape(1, LANES)
    hcat, ids_p, gates_p = _router(xp, gn, mod_p, wr, br, None, n_tok, tm=256, rows_per_batch=seq, row0=0)
    hcat, ids_s, gates_s = _router(xs, gn, mod_s, wr, br, hcat, n_tok, tm=n_s, rows_per_batch=n_s, row0=n_p)

    expert = jnp.concatenate([ids_p[:, :MOE_TOP_K], ids_s[:, :MOE_TOP_K]], axis=0).reshape(-1)
    n_assign = n_tok * MOE_TOP_K
    n_blocks = -(-n_assign // MOE_ROWS) + MOE_EXPERTS
    onehot = (expert[:, None] == jnp.arange(MOE_EXPERTS, dtype=jnp.int32)[None, :]).astype(jnp.int32)
    csum = jnp.cumsum(onehot, axis=0)
    rank = jnp.sum(csum * onehot, axis=1) - 1
    counts = csum[-1]
    padded = (counts + MOE_ROWS - 1) // MOE_ROWS * MOE_ROWS
    pend = jnp.cumsum(padded)
    pstart = pend - padded
    dest = (pstart[expert] + rank).astype(jnp.int32)
    tok = jnp.arange(n_assign, dtype=jnp.int32) // MOE_TOP_K
    row_tok = jnp.zeros((n_blocks * MOE_ROWS,), jnp.int32).at[dest].set(tok, unique_indices=True)
    blk_ids = jnp.arange(n_blocks, dtype=jnp.int32)
    blk_expert = jnp.minimum(jnp.searchsorted(pend, blk_ids * MOE_ROWS, side="right"),
                             MOE_EXPERTS - 1).astype(jnp.int32)
    n_used = (pend[-1:] // MOE_ROWS).astype(jnp.int32)
    first = (blk_ids == 0) | (blk_expert != jnp.roll(blk_expert, 1))
    next_run = jnp.minimum(pend[blk_expert] // MOE_ROWS, n_used[0] - 1)
    w_idx = jnp.where(first, blk_expert, blk_expert[next_run]).astype(jnp.int32)

    xs_sorted = _moe_gather(hcat, row_tok, n_used, n_blocks)
    hs = _moe_gate_up(xs_sorted, w_gu, layer, blk_expert, w_idx, n_used, n_blocks)
    y = _moe_down(hs, w_down, layer, blk_expert, w_idx, n_used, n_blocks)
    xp_new = _moe_combine(y, dest, xp, mod_p, gates_p, tm=256, rows_per_batch=seq, tok0=0)
    xs_new = _moe_combine(y, dest, xs, mod_s, gates_s, tm=n_s, rows_per_batch=n_s, tok0=n_p)
    return xp_new, xs_new


def _final_norm_kernel(x_ref, g_ref, o_ref):
    x = x_ref[...]
    inv = lax.rsqrt(jnp.mean(x * x, axis=-1, keepdims=True) + EPS)
    o_ref[...] = (x * inv) * g_ref[...]


def _final_norm(x, g, tm):
    m = x.shape[0]
    return pl.pallas_call(
        _final_norm_kernel,
        grid=(m // tm,),
        in_specs=[pl.BlockSpec((tm, D_MODEL), lambda i: (i, 0)), pl.BlockSpec((1, D_MODEL), lambda i: (0, 0))],
        out_specs=pl.BlockSpec((tm, D_MODEL), lambda i: (i, 0)),
        out_shape=jax.ShapeDtypeStruct((m, D_MODEL), F32),
        compiler_params=_cparams(("arbitrary",)),
        name="final_norm",
    )(x, g.reshape(1, D_MODEL))


def _rope_tables(pos):
    half = SW_HEAD_DIM // 2
    inv_freq = ROPE_THETA ** (-jnp.arange(half, dtype=F32) / half)
    ang = pos.astype(F32)[:, None] * inv_freq[None, :]
    cos, sin = jnp.cos(ang), jnp.sin(ang)
    return jnp.concatenate([cos, cos], axis=1), jnp.concatenate([-sin, sin], axis=1)


def kernel(x_prompt, x_sample, state_hgrn, state_conv, cache_swa_g1, cache_swa_g2, cache_swa_g3, c_prompt, c_sample, ada_w, ada_b, norm_mix, norm_ffn, norm_final, hg_w_in, hg_w_out, hg_norm, hg_lower, cv_w_in, cv_w_conv, cv_w_out, sw_w_in, sw_w_out, moe_w_group, moe_b_group, moe_w_router, moe_b_router, moe_w_gu, moe_w_down):
    bp, seq, d = x_prompt.shape
    bs = x_sample.shape[0]
    n_p = bp * seq
    xp = x_prompt.reshape(n_p, d)
    xs = x_sample.reshape(bs, d)

    mod_all = _ada_mod(jnp.concatenate([c_prompt, c_sample], axis=0), ada_w, ada_b)
    sm = jax.nn.softmax(hg_lower.astype(F32), axis=0)
    lower = jnp.cumsum(sm, axis=0) - sm[0]
    rope_p = _rope_tables(jnp.arange(seq))
    rope_s = _rope_tables(jnp.full((bs,), PAST_LEN, jnp.int32))

    tm_p, tn_p = 1024, 512
    hg_p, hg_s, cv_p, cv_s = [], [], [], []
    sw_p, sw_s = None, None
    for i in range(DEPTH):
        kind, j = i % N_MIXERS, i // N_MIXERS
        mod_p = mod_all[i, :bp].reshape(bp, 1, 6 * d)
        mod_s = mod_all[i, bp:].reshape(1, bs, 6 * d)
        pp = dict(tm=tm_p, tn=tn_p)
        pp_in = dict(tm=tm_p, tn=1024)
        ps = dict(tm=bs, tn=1024)
        hp = _norm_mod(xp, norm_mix[i], mod_p, 0, 1, tm=512, rows_per_batch=seq)
        hs = _norm_mod(xs, norm_mix[i], mod_s, 0, 1, tm=bs, rows_per_batch=bs)
        if kind == 0:
            proj_p = _mm(hp, hg_w_in, j, **pp_in)
            proj_s = _mm(hs, hg_w_in, j, **ps)
            lhs_p, st_p = _gla_prompt(proj_p, lower[j], hg_norm[j], bp, seq)
            lhs_s, st_s = _gla_step(proj_s, lower[j], hg_norm[j], state_hgrn, j)
            hg_p.append(st_p)
            hg_s.append(st_s)
            w_out = hg_w_out
        elif kind == 1:
            proj_p = _mm(hp, cv_w_in, j, **pp_in)
            proj_s = _mm(hs, cv_w_in, j, **ps)
            lhs_p, tail_p = _conv_prompt(proj_p, cv_w_conv[j], bp, seq)
            lhs_s, tail_s = _conv_step(proj_s, cv_w_conv[j], state_conv[j])
            cv_p.append(tail_p)
            cv_s.append(tail_s)
            w_out = cv_w_out
        else:
            qkv_p = _mm(hp, sw_w_in, j, tm=512, tn=SW_WIDTH, rope=rope_p)
            qkv_s = _mm(hs, sw_w_in, j, rope=rope_s, **ps)
            os_, ls_ = [], []
            for gi, (win, dil) in enumerate(SW_GROUPS):
                o_g, l_g = _swa_prompt_group(qkv_p, gi, dil, bp, seq)
                os_.append(o_g)
                ls_.append(l_g)
            lhs_p = _swa_merge(os_, ls_)
            caches = (cache_swa_g1[j], cache_swa_g2[j], cache_swa_g3[j])
            lhs_s = _swa_step(qkv_s, caches)
            w_out = sw_w_out
            sw_p = [_kv_window(qkv_p, gi, win, bp, seq)[None] for gi, (win, _) in enumerate(SW_GROUPS)]
            sw_s = [c[None] for c in _cache_shift(qkv_s, caches)]
        xp = _mm(lhs_p, w_out, j, resgate=(xp, mod_p, 2, seq), **pp)
        xs = _mm(lhs_s, w_out, j, resgate=(xs, mod_s, 2, bs), **ps)
        xp, xs = _moe_layer(xp, xs, mod_p, mod_s, norm_ffn[i], moe_w_group[i], moe_b_group[i],
                            moe_w_router[i], moe_b_router[i], moe_w_gu, moe_w_down, i, seq)

    y_p = _final_norm(xp, norm_final, 512).reshape(bp, seq, d)
    y_s = _final_norm(xs, norm_final, bs).reshape(bs, 1, d)
    return (y_p, y_s, jnp.stack(hg_p), jnp.stack(hg_s), jnp.stack(cv_p), jnp.stack(cv_s),
            sw_p[0], sw_s[0], sw_p[1], sw_s[1], sw_p[2], sw_s[2])
```

```python
import functools

import jax
import jax.numpy as jnp
from jax import lax
from jax.experimental import pallas as pl
from jax.experimental.pallas import tpu as pltpu

F32 = jnp.float32
BF16 = jnp.bfloat16

D_MODEL = 2048
DEPTH = 4
N_MIXERS = 3
EPS = 1e-6
MASK_VALUE = -1e30
F_FLOOR = 1e-30
HG_DK = 128
HG_HEADS = D_MODEL // HG_DK
HG_CHUNK = 64
HG_SUB = 16
CONV_WIDTH = 3
SW_GROUPS = ((128, 1), (512, 4), (2048, 16))
SW_HEADS = 8
SW_HEAD_DIM = 128
SW_WIDTH = SW_HEADS * SW_HEAD_DIM
SW_BAND = 128
ROPE_THETA = 10000.0
MOE_GROUPS = 4
MOE_PER_GROUP = 8
MOE_EXPERTS = MOE_GROUPS * MOE_PER_GROUP
MOE_TOP_K = 2
MOE_FF = 1024
PAST_LEN = 16384

LANES = 128
SUBLANES = 8
VMEM_LIMIT = 52 * 1024 * 1024
MOE_ROWS = 256
MOE_FFC = 512
ROW_TILES = D_MODEL // LANES
DMA_UNROLL = 8
W_CHUNK = 256
GU_CHUNKS = D_MODEL // W_CHUNK
RUN_CHUNKS = GU_CHUNKS + MOE_FF // W_CHUNK
W_STAGES = 4
HIGHEST = lax.Precision.HIGHEST


def _cparams(sem):
    return pltpu.CompilerParams(dimension_semantics=sem, vmem_limit_bytes=VMEM_LIMIT)


def _sigmoid(x):
    return 1.0 / (1.0 + jnp.exp(-x))


def _silu(x):
    return x * _sigmoid(x)


def _bf16_round(x):
    return x.astype(BF16).astype(F32)


def _ada_kernel(c_ref, w_ref, b_ref, o_ref):
    c = c_ref[...]
    cond = _silu(c).astype(BF16)
    o_ref[...] = jnp.dot(cond, w_ref[...].astype(BF16), preferred_element_type=F32) + b_ref[...]


def _ada_mod(c_all, ada_w, ada_b):
    rows = c_all.shape[0]
    n = ada_w.shape[-1]
    tn = 1024
    return pl.pallas_call(
        _ada_kernel,
        grid=(DEPTH, n // tn),
        in_specs=[
            pl.BlockSpec((rows, D_MODEL), lambda l, j: (0, 0)),
            pl.BlockSpec((None, D_MODEL, tn), lambda l, j: (l, 0, j)),
            pl.BlockSpec((None, 1, tn), lambda l, j: (l, 0, j)),
        ],
        out_specs=pl.BlockSpec((None, rows, tn), lambda l, j: (l, 0, j)),
        out_shape=jax.ShapeDtypeStruct((DEPTH, rows, n), F32),
        compiler_params=_cparams(("arbitrary", "arbitrary")),
        name="ada_mod",
    )(c_all, ada_w, ada_b.reshape(DEPTH, 1, n))


def _norm_mod_rows(x, gn, sc, sh):
    inv = lax.rsqrt(jnp.mean(x * x, axis=-1, keepdims=True) + EPS)
    return (x * inv) * gn * (1.0 + sc) + sh


def _norm_mod_into(x_ref, gn_ref, sc_ref, sh_ref, dst_refs, tm):
    ch = min(tm, 128)
    per_row = sc_ref.shape[0] != 1

    def body(c, carry):
        rs = pl.ds(pl.multiple_of(c * ch, ch), ch)
        sc = sc_ref[rs, :] if per_row else sc_ref[...]
        sh = sh_ref[rs, :] if per_row else sh_ref[...]
        h = _norm_mod_rows(x_ref[rs, :], gn_ref[...], sc, sh)
        for d in dst_refs:
            d[rs, :] = h.astype(d.dtype)
        return carry

    lax.fori_loop(0, tm // ch, body, 0)


def _norm_mod_kernel(x_ref, gn_ref, sc_ref, sh_ref, o_ref, *, tm):
    _norm_mod_into(x_ref, gn_ref, sc_ref, sh_ref, [o_ref], tm)


def _norm_mod(x, gn, mod, shift_col, scale_col, *, tm, rows_per_batch):
    m = x.shape[0]
    bpb = max(rows_per_batch // tm, 1)
    msp = lambda col: pl.BlockSpec((None, mod.shape[1], D_MODEL), lambda i: (i // bpb, 0, col))
    return pl.pallas_call(
        functools.partial(_norm_mod_kernel, tm=tm),
        grid=(m // tm,),
        in_specs=[pl.BlockSpec((tm, D_MODEL), lambda i: (i, 0)),
                  pl.BlockSpec((1, D_MODEL), lambda i: (0, 0)),
                  msp(scale_col), msp(shift_col)],
        out_specs=pl.BlockSpec((tm, D_MODEL), lambda i: (i, 0)),
        out_shape=jax.ShapeDtypeStruct((m, D_MODEL), BF16),
        compiler_params=_cparams(("arbitrary",)),
        name="norm_mod",
    )(x, gn.reshape(1, D_MODEL), mod, mod)


def _rope_tile(acc, cos, sin_signed):
    outs = []
    for h in range(acc.shape[1] // SW_HEAD_DIM):
        xh = acc[:, h * SW_HEAD_DIM:(h + 1) * SW_HEAD_DIM]
        outs.append(xh * cos + pltpu.roll(xh, SW_HEAD_DIM // 2, 1) * sin_signed)
    return jnp.concatenate(outs, axis=1)


def _mm_kernel(x_ref, w_ref, *rest, epilogue):
    o_ref, wb_ref = rest[-2:]
    j, i = pl.program_id(0), pl.program_id(1)

    @pl.when(i == 0)
    def _():
        wb_ref[...] = w_ref[...].astype(BF16)

    acc = jnp.dot(x_ref[...].astype(BF16), wb_ref[...], preferred_element_type=F32)
    if epilogue == "rope":
        cos_ref, sin_ref = rest[:2]
        is_qk = (j % 3) < 2
        cos = jnp.where(is_qk, cos_ref[...], 1.0)
        sin = jnp.where(is_qk, sin_ref[...], 0.0)
        o_ref[...] = _rope_tile(acc, cos, sin)
    elif epilogue == "resgate":
        res_ref, g_ref = rest[:2]
        o_ref[...] = res_ref[...] + g_ref[...] * acc
    else:
        o_ref[...] = acc


def _mm(x, w, wl, *, tm, tn, rope=None, resgate=None):
    m, k = x.shape
    n = w.shape[2]
    in_specs = [pl.BlockSpec((tm, k), lambda j, i: (i, 0)),
                pl.BlockSpec((None, k, tn), lambda j, i: (wl, 0, j))]
    args = [x, w]
    epilogue = "none"
    if rope is not None:
        epilogue = "rope"
        nblk = rope[0].shape[0] // tm
        in_specs += [pl.BlockSpec((tm, SW_HEAD_DIM), lambda j, i: (i % nblk, 0))] * 2
        args += list(rope)
    if resgate is not None:
        epilogue = "resgate"
        res, mod, gate_col, rows_per_batch = resgate
        bpb = max(rows_per_batch // tm, 1)
        per = D_MODEL // tn
        in_specs += [pl.BlockSpec((tm, tn), lambda j, i: (i, j)),
                     pl.BlockSpec((None, mod.shape[1], tn), lambda j, i: (i // bpb, 0, gate_col * per + j))]
        args += [res, mod]
    return pl.pallas_call(
        functools.partial(_mm_kernel, epilogue=epilogue),
        grid=(n // tn, m // tm),
        in_specs=in_specs,
        out_specs=pl.BlockSpec((tm, tn), lambda j, i: (i, j)),
        out_shape=jax.ShapeDtypeStruct((m, n), F32),
        scratch_shapes=[pltpu.VMEM((k, tn), BF16)],
        compiler_params=_cparams(("arbitrary", "arbitrary")),
        name="mm_" + epilogue,
    )(*args)


def _hgrn_gates(qp, fp, lb):
    q = _silu(qp)
    f = lb + (1.0 - lb) * _sigmoid(fp)
    logf = jnp.log(jnp.maximum(f, F_FLOOR))
    k = (1.0 - lb) * _sigmoid(-fp)
    return q, logf, k


def _head_norm_gate(o, gn, gp):
    inv = lax.rsqrt(jnp.mean(o * o, axis=-1, keepdims=True) + EPS)
    return (o * inv) * gn * _silu(gp)


def _gla_kernel(q_ref, f_ref, i_ref, g_ref, lb_ref, gn_ref, o_ref, s_out_ref, st_ref, *, n_chunks, n_hb):
    c_len, n_sub = HG_CHUNK, HG_CHUNK // HG_SUB
    t = pl.program_id(2)

    @pl.when(t == 0)
    def _():
        st_ref[...] = jnp.zeros_like(st_ref)

    row = lax.broadcasted_iota(jnp.int32, (c_len, HG_DK), 0)
    r4 = lax.broadcasted_iota(jnp.int32, (c_len, n_sub * c_len), 0)
    c4 = lax.broadcasted_iota(jnp.int32, (c_len, n_sub * c_len), 1)
    mask4 = ((r4 // HG_SUB) == (c4 // c_len)) & ((c4 % c_len) <= r4)
    nt = (((1,), (1,)), ((), ()))
    tn = (((0,), (0,)), ((), ()))

    for c, hh in [(c, hh) for c in range(n_chunks) for hh in range(n_hb)]:
        rs = pl.ds(c * c_len, c_len)
        cs = slice(hh * HG_DK, (hh + 1) * HG_DK)
        lb, gn = lb_ref[:, cs], gn_ref[:, cs]
        q, logf, k = _hgrn_gates(q_ref[rs, cs], f_ref[rs, cs], lb)
        v = i_ref[rs, cs]
        g = logf
        sh = 1
        while sh < c_len:
            g = g + jnp.where(row >= sh, pltpu.roll(g, sh, 0), 0.0)
            sh *= 2
        refs = [g[HG_SUB * i + HG_SUB // 2 - 1:HG_SUB * i + HG_SUB // 2, :] for i in range(n_sub)]
        mrows = jnp.concatenate([jnp.broadcast_to(r, (HG_SUB, HG_DK)) for r in refs], axis=0)
        qh = (q * jnp.exp(g - mrows)).astype(BF16)
        khs = [k * jnp.exp(jnp.where(row < HG_SUB * (i + 1), refs[i] - g, -jnp.inf)) for i in range(n_sub)]
        kh = jnp.concatenate(khs, axis=0).astype(BF16)
        a4 = lax.dot_general(qh, kh, nt, preferred_element_type=F32)
        a4 = jnp.where(mask4, a4, 0.0).astype(BF16)
        vb = v.astype(BF16)
        v4 = jnp.concatenate([vb] * n_sub, axis=0)
        st = st_ref[hh]
        o = jnp.dot(a4, v4, preferred_element_type=F32)
        o = o + lax.dot_general((q * jnp.exp(g)).astype(BF16), st.astype(BF16), nt,
                                preferred_element_type=F32)
        glast = g[c_len - 1:c_len, :]
        kd = (k * jnp.exp(glast - g)).astype(BF16)
        st_ref[hh] = st * jnp.exp(glast) + lax.dot_general(vb, kd, tn, preferred_element_type=F32)
        o_ref[rs, cs] = _head_norm_gate(o, gn, g_ref[rs, cs]).astype(o_ref.dtype)

    @pl.when(t == pl.num_programs(2) - 1)
    def _():
        for hh in range(n_hb):
            s_out_ref[hh] = st_ref[hh].T


def _gla_prompt(proj, lower, g_norm, batch, seq):
    tb = 512
    n_hb = 2
    nt = seq // tb
    hblk = HG_HEADS // n_hb
    wcol = n_hb * HG_DK

    def col(off):
        return pl.BlockSpec((tb, wcol), lambda b, hh, t: (b * nt + t, off * hblk + hh))

    return pl.pallas_call(
        functools.partial(_gla_kernel, n_chunks=tb // HG_CHUNK, n_hb=n_hb),
        grid=(batch, hblk, nt),
        in_specs=[col(0), col(1), col(2), col(3),
                  pl.BlockSpec((1, wcol), lambda b, hh, t: (0, hh)),
                  pl.BlockSpec((1, wcol), lambda b, hh, t: (0, hh))],
        out_specs=[pl.BlockSpec((tb, wcol), lambda b, hh, t: (b * nt + t, hh)),
                   pl.BlockSpec((None, n_hb, HG_DK, HG_DK), lambda b, hh, t: (b, hh, 0, 0))],
        out_shape=[jax.ShapeDtypeStruct((batch * seq, D_MODEL), BF16),
                   jax.ShapeDtypeStruct((batch, HG_HEADS, HG_DK, HG_DK), F32)],
        scratch_shapes=[pltpu.VMEM((n_hb, HG_DK, HG_DK), F32)],
        compiler_params=_cparams(("arbitrary", "arbitrary", "arbitrary")),
        name="hgrn_prompt",
    )(proj, proj, proj, proj, lower.reshape(1, D_MODEL), g_norm.reshape(1, D_MODEL))


def _gla_step_kernel(qt_ref, ft_ref, i_ref, g_ref, lbt_ref, gn_ref, s_ref, o_ref, s_out_ref):
    for h in range(HG_HEADS):
        cs = slice(h * HG_DK, (h + 1) * HG_DK)
        q, logf, k = _hgrn_gates(qt_ref[:, h:h + 1], ft_ref[:, h:h + 1], lbt_ref[:, h:h + 1])
        dec = jnp.exp(logf)
        s_old = s_ref[h]
        v = i_ref[:, cs]
        s_out_ref[h] = dec * s_old + k * v
        qd = _bf16_round(jnp.broadcast_to(q * dec, s_old.shape))
        o = jnp.sum(_bf16_round(s_old) * qd, axis=0, keepdims=True) + jnp.sum(q * k, axis=0, keepdims=True) * v
        o_ref[:, cs] = _head_norm_gate(o, gn_ref[:, cs], g_ref[:, cs])


def _gla_step(proj, lower, g_norm, states, layer):
    b = proj.shape[0]
    h = HG_HEADS
    pq = proj[:, :D_MODEL].reshape(b, h, HG_DK).transpose(0, 2, 1)
    pf = proj[:, D_MODEL:2 * D_MODEL].reshape(b, h, HG_DK).transpose(0, 2, 1)
    p3 = proj.reshape(b, 1, 4 * D_MODEL)
    lbt = lower.reshape(h, HG_DK).T
    vec = lambda col: pl.BlockSpec((None, 1, D_MODEL), lambda i: (i, 0, col))
    tr = pl.BlockSpec((None, HG_DK, h), lambda i: (i, 0, 0))
    o, s_new = pl.pallas_call(
        _gla_step_kernel,
        grid=(b,),
        in_specs=[tr, tr, vec(2), vec(3),
                  pl.BlockSpec((HG_DK, h), lambda i: (0, 0)),
                  pl.BlockSpec((1, D_MODEL), lambda i: (0, 0)),
                  pl.BlockSpec((None, None, h, HG_DK, HG_DK), lambda i: (layer, i, 0, 0, 0))],
        out_specs=[pl.BlockSpec((None, 1, D_MODEL), lambda i: (i, 0, 0)),
                   pl.BlockSpec((None, h, HG_DK, HG_DK), lambda i: (i, 0, 0, 0))],
        out_shape=[jax.ShapeDtypeStruct((b, 1, D_MODEL), F32),
                   jax.ShapeDtypeStruct(states.shape[1:], F32)],
        compiler_params=_cparams(("arbitrary",)),
        name="hgrn_step",
    )(pq, pf, p3, p3, lbt, g_norm.reshape(1, D_MODEL), states)
    return o.reshape(b, D_MODEL), s_new


def _conv_kernel(bg_ref, cg_ref, u_ref, hc_ref, hu_ref, w_ref, o_ref, tail_ref, *, tm):
    t = pl.program_id(1)
    z = cg_ref[...] * u_ref[...]
    hz = hc_ref[...] * hu_ref[...]
    hz = jnp.where(t == 0, 0.0, hz)
    z1p, z2p = hz[SUBLANES - 1:SUBLANES, :], hz[SUBLANES - 2:SUBLANES - 1, :]
    row = lax.broadcasted_iota(jnp.int32, z.shape, 0)
    z1 = jnp.where(row == 0, z1p, pltpu.roll(z, 1, 0))
    z2 = jnp.where(row == 0, z2p, jnp.where(row == 1, z1p, pltpu.roll(z, 2, 0)))
    y = z2 * w_ref[0:1, :] + z1 * w_ref[1:2, :] + z * w_ref[2:3, :]
    o_ref[...] = (bg_ref[...] * y).astype(o_ref.dtype)
    tail_ref[...] = z[tm - (CONV_WIDTH - 1):, :]


def _conv_prompt(proj, w_conv, batch, seq):
    tm = 256
    nt = seq // tm
    hb = tm // SUBLANES
    blk = lambda col: pl.BlockSpec((tm, D_MODEL), lambda b, t: (b * nt + t, col))
    halo = lambda col: pl.BlockSpec(
        (SUBLANES, D_MODEL), lambda b, t: (jnp.maximum((b * nt + t) * hb - 1, 0), col))
    return pl.pallas_call(
        functools.partial(_conv_kernel, tm=tm),
        grid=(batch, nt),
        in_specs=[blk(0), blk(1), blk(2), halo(1), halo(2),
                  pl.BlockSpec((CONV_WIDTH, D_MODEL), lambda b, t: (0, 0))],
        out_specs=[pl.BlockSpec((tm, D_MODEL), lambda b, t: (b * nt + t, 0)),
                   pl.BlockSpec((None, CONV_WIDTH - 1, D_MODEL), lambda b, t: (b, 0, 0))],
        out_shape=[jax.ShapeDtypeStruct((batch * seq, D_MODEL), BF16),
                   jax.ShapeDtypeStruct((batch, CONV_WIDTH - 1, D_MODEL), F32)],
        compiler_params=_cparams(("arbitrary", "arbitrary")),
        name="conv_prompt",
    )(proj, proj, proj, proj, proj, w_conv)


def _conv_step_kernel(bg_ref, cg_ref, u_ref, buf_ref, w_ref, o_ref, nb_ref):
    z = cg_ref[...] * u_ref[...]
    buf = buf_ref[...]
    y = buf[:, 0:1, :] * w_ref[0:1, :] + buf[:, 1:2, :] * w_ref[1:2, :] + z * w_ref[2:3, :]
    o_ref[...] = bg_ref[...] * y
    nb_ref[:, 0:1, :] = buf[:, 1:2, :]
    nb_ref[:, 1:2, :] = z


def _conv_step(proj, w_conv, buf):
    b = proj.shape[0]
    p3 = proj.reshape(b, 1, 3 * D_MODEL)
    vec = lambda col: pl.BlockSpec((b, 1, D_MODEL), lambda i: (0, 0, col))
    o, nb = pl.pallas_call(
        _conv_step_kernel,
        grid=(1,),
        in_specs=[vec(0), vec(1), vec(2),
                  pl.BlockSpec((b, CONV_WIDTH - 1, D_MODEL), lambda i: (0, 0, 0)),
                  pl.BlockSpec((CONV_WIDTH, D_MODEL), lambda i: (0, 0))],
        out_specs=[pl.BlockSpec((b, 1, D_MODEL), lambda i: (0, 0, 0)),
                   pl.BlockSpec((b, CONV_WIDTH - 1, D_MODEL), lambda i: (0, 0, 0))],
        out_shape=[jax.ShapeDtypeStruct((b, 1, D_MODEL), F32),
                   jax.ShapeDtypeStruct(buf.shape, F32)],
        compiler_params=_cparams(("arbitrary",)),
        name="conv_step",
    )(p3, p3, p3, buf, w_conv)
    return o.reshape(b, D_MODEL), nb


def _swa_kernel(*refs, dil, n_units, n_heads, has_prev):
    if has_prev:
        qc_ref, kp_ref, kc_ref, vp_ref, vc_ref, o_ref, l_ref = refs
    else:
        qc_ref, kc_ref, vc_ref, o_ref, l_ref = refs
    blk = SW_BAND
    n = pl.program_id(1)
    ri = lax.broadcasted_iota(jnp.int32, (blk, blk), 0)
    ci = lax.broadcasted_iota(jnp.int32, (blk, blk), 1)
    m_cur = ci <= ri
    m_prev_tri = ci >= ri
    lane = lax.broadcasted_iota(jnp.int32, (blk, LANES), 1)
    scale = SW_HEAD_DIM ** -0.5
    nt = (((1,), (1,)), ((), ()))

    for u in range(n_units):
        if dil == 1:
            rows = pl.ds(u * blk, blk)
            prev_src = None
            if has_prev:
                prev_src = (kp_ref, vp_ref, pl.ds(0, blk), True) if u == 0 else \
                    (kc_ref, vc_ref, pl.ds((u - 1) * blk, blk), False)
        else:
            rows = pl.ds(u, blk, stride=dil)
            prev_src = (kp_ref, vp_ref, rows, True) if has_prev else None

        def head(h, lse_acc, rows=rows, prev_src=prev_src):
            cs = pl.ds(0, SW_HEAD_DIM) if n_heads == 1 else \
                pl.ds(pl.multiple_of(h * SW_HEAD_DIM, SW_HEAD_DIM), SW_HEAD_DIM)
            q = qc_ref[rows, cs].astype(BF16)
            s_c = lax.dot_general(q, kc_ref[rows, cs].astype(BF16), nt, preferred_element_type=F32) * scale
            s_c = jnp.where(m_cur, s_c, MASK_VALUE)
            m = jnp.max(s_c, axis=-1, keepdims=True)
            if prev_src is not None:
                kr, vr, prow, first_only = prev_src
                s_p = lax.dot_general(q, kr[prow, cs].astype(BF16), nt, preferred_element_type=F32) * scale
                m_prev = (m_prev_tri & (n > 0)) if first_only else m_prev_tri
                s_p = jnp.where(m_prev, s_p, MASK_VALUE)
                m = jnp.maximum(m, jnp.max(s_p, axis=-1, keepdims=True))
            p_c = jnp.exp(s_c - m)
            l = jnp.sum(p_c, axis=-1, keepdims=True)
            o = jnp.dot(p_c.astype(BF16), vc_ref[rows, cs].astype(BF16), preferred_element_type=F32)
            if prev_src is not None:
                p_p = jnp.exp(s_p - m)
                l = l + jnp.sum(p_p, axis=-1, keepdims=True)
                o = o + jnp.dot(p_p.astype(BF16), vr[prow, cs].astype(BF16), preferred_element_type=F32)
            o_ref[rows, cs] = o / l
            lse = m + jnp.log(l)
            return jnp.where(lane == h, lse, lse_acc)

        lse0 = jnp.zeros((blk, LANES), F32)
        l_ref[rows, :] = head(0, lse0) if n_heads == 1 else lax.fori_loop(0, n_heads, head, lse0)


def _swa_prompt_group(qkv, gi, dil, batch, seq):
    span = SW_BAND * dil
    if dil == 1:
        span, n_units = 512, 4
    else:
        n_units = dil
    nb = seq // span
    has_prev = nb > 1
    hb = SW_HEADS if dil == 1 else 1
    n_hblk = SW_HEADS // hb
    wcol = hb * SW_HEAD_DIM
    per = SW_WIDTH // wcol
    pspan = SW_BAND if dil == 1 else span
    pmul = span // pspan

    def cur(which):
        return pl.BlockSpec((span, wcol), lambda b, n, hh: (b * nb + n, (gi * 3 + which) * per + hh))

    def prev(which):
        return pl.BlockSpec(
            (pspan, wcol),
            lambda b, n, hh: (jnp.maximum((b * nb + n) * pmul - 1, 0), (gi * 3 + which) * per + hh))

    if has_prev:
        in_specs = [cur(0), prev(1), cur(1), prev(2), cur(2)]
        args = [qkv] * 5
    else:
        in_specs = [cur(0), cur(1), cur(2)]
        args = [qkv] * 3
    return pl.pallas_call(
        functools.partial(_swa_kernel, dil=dil, n_units=n_units, n_heads=hb, has_prev=has_prev),
        grid=(batch, nb, n_hblk),
        in_specs=in_specs,
        out_specs=[pl.BlockSpec((span, wcol), lambda b, n, hh: (b * nb + n, hh)),
                   pl.BlockSpec((span, LANES), lambda b, n, hh: (b * nb + n, hh))],
        out_shape=[jax.ShapeDtypeStruct((batch * seq, SW_WIDTH), F32),
                   jax.ShapeDtypeStruct((batch * seq, n_hblk * LANES), F32)],
        compiler_params=_cparams(("arbitrary", "arbitrary", "arbitrary")),
        name=f"swa_prompt_g{gi}",
    )(*args)


def _merge_heads(o_refs, l_refs, rs, hbs):
    outs = []
    for h in range(SW_HEADS):
        cs = slice(h * SW_HEAD_DIM, (h + 1) * SW_HEAD_DIM)
        lses = []
        for l_ref, hb in zip(l_refs, hbs):
            lane = (h // hb) * LANES + h % hb
            lses.append(l_ref[rs, lane:lane + 1])
        mx = jnp.maximum(jnp.maximum(lses[0], lses[1]), lses[2])
        es = [jnp.exp(l - mx) for l in lses]
        den = es[0] + es[1] + es[2]
        acc = None
        for e, o_ref in zip(es, o_refs):
            o = o_ref[rs, cs]
            term = _bf16_round(jnp.broadcast_to(e / den, o.shape)) * _bf16_round(o)
            acc = term if acc is None else acc + term
        outs.append(acc)
    return jnp.concatenate(outs, axis=1)


def _swa_merge_kernel(o1, o2, o3, l1, l2, l3, out_ref, *, tm, hbs):
    ch = 128

    def body(c, carry):
        rs = pl.ds(pl.multiple_of(c * ch, ch), ch)
        out_ref[rs, :] = _merge_heads((o1, o2, o3), (l1, l2, l3), rs, hbs).astype(out_ref.dtype)
        return carry

    lax.fori_loop(0, tm // ch, body, 0)


def _swa_merge(os_, ls_):
    tm = 512
    m = os_[0].shape[0]
    hbs = tuple(SW_HEADS // (l.shape[1] // LANES) for l in ls_)
    row = lambda width: pl.BlockSpec((tm, width), lambda i: (i, 0))
    return pl.pallas_call(
        functools.partial(_swa_merge_kernel, tm=tm, hbs=hbs),
        grid=(m // tm,),
        in_specs=[row(SW_WIDTH)] * 3 + [row(l.shape[1]) for l in ls_],
        out_specs=row(SW_WIDTH),
        out_shape=jax.ShapeDtypeStruct((m, SW_WIDTH), BF16),
        compiler_params=_cparams(("arbitrary",)),
        name="swa_merge",
    )(*os_, *ls_)


def _swa_step_kernel(qkv_ref, c1_ref, c2_ref, c3_ref, o_ref):
    scale = SW_HEAD_DIM ** -0.5
    outs, lses = [], []
    for gi, c_ref in enumerate((c1_ref, c2_ref, c3_ref)):
        q = _bf16_round(qkv_ref[gi, 0])
        kn = _bf16_round(qkv_ref[gi, 1])
        vn = _bf16_round(qkv_ref[gi, 2])
        kc = _bf16_round(c_ref[:, 0])
        vc = _bf16_round(c_ref[:, 1])
        s = jnp.sum(kc * q[None], axis=-1, keepdims=True) * scale
        sn = jnp.sum(kn * q, axis=-1, keepdims=True) * scale
        m = jnp.maximum(jnp.max(s, axis=0), sn)
        lse = m + jnp.log(jnp.sum(jnp.exp(s - m[None]), axis=0) + jnp.exp(sn - m))
        shape = vc.shape
        p = _bf16_round(jnp.broadcast_to(jnp.exp(s - lse[None]), shape))
        pn = _bf16_round(jnp.broadcast_to(jnp.exp(sn - lse), shape[1:]))
        outs.append(jnp.sum(p * vc, axis=0) + pn * vn)
        lses.append(lse)
    mx = jnp.maximum(jnp.maximum(lses[0], lses[1]), lses[2])
    es = [jnp.exp(l - mx) for l in lses]
    den = es[0] + es[1] + es[2]
    acc = None
    for e, o in zip(es, outs):
        term = _bf16_round(jnp.broadcast_to(e / den, o.shape)) * _bf16_round(o)
        acc = term if acc is None else acc + term
    o_ref[...] = acc


def _swa_step(qkv, caches):
    b = qkv.shape[0]
    q5 = qkv.reshape(b, len(SW_GROUPS), 3, SW_HEADS, SW_HEAD_DIM)
    views, specs = [], []
    for c, (win, dil) in zip(caches, SW_GROUPS):
        views.append(c.reshape(b, win // dil, dil, 2, SW_HEADS, SW_HEAD_DIM))
        specs.append(pl.BlockSpec((None, win // dil, None, 2, SW_HEADS, SW_HEAD_DIM),
                                  lambda i: (i, 0, 0, 0, 0, 0)))
    o = pl.pallas_call(
        _swa_step_kernel,
        grid=(b,),
        in_specs=[pl.BlockSpec((None, len(SW_GROUPS), 3, SW_HEADS, SW_HEAD_DIM),
                               lambda i: (i, 0, 0, 0, 0))] + specs,
        out_specs=pl.BlockSpec((None, SW_HEADS, SW_HEAD_DIM), lambda i: (i, 0, 0)),
        out_shape=jax.ShapeDtypeStruct((b, SW_HEADS, SW_HEAD_DIM), F32),
        compiler_params=_cparams(("arbitrary",)),
        name="swa_step",
    )(q5, *views)
    return o.reshape(b, SW_WIDTH)


def _cache_shift_kernel(cur_ref, nxt_ref, knew_ref, vnew_ref, o_ref, *, wb):
    k = pl.program_id(1)
    last = pl.num_programs(1) - 1

    def row(i, carry):
        o_ref[i] = cur_ref[i + 1]
        return carry

    lax.fori_loop(0, wb - 1, row, 0, unroll=8)

    @pl.when(k < last)
    def _():
        o_ref[wb - 1] = nxt_ref[0]

    @pl.when(k == last)
    def _():
        o_ref[wb - 1, 0] = knew_ref[...]
        o_ref[wb - 1, 1] = vnew_ref[...]


def _cache_shift(qkv, caches):
    b = qkv.shape[0]
    q5 = qkv.reshape(b, len(SW_GROUPS), 3, SW_HEADS, SW_HEAD_DIM)
    outs = []
    for gi, c in enumerate(caches):
        w = c.shape[1]
        wb = min(w, 512)
        tail = (2, SW_HEADS, SW_HEAD_DIM)
        outs.append(pl.pallas_call(
            functools.partial(_cache_shift_kernel, wb=wb),
            grid=(b, w // wb),
            in_specs=[
                pl.BlockSpec((None, wb) + tail, lambda i, k: (i, k, 0, 0, 0)),
                pl.BlockSpec((None, 1) + tail, lambda i, k: (i, jnp.minimum((k + 1) * wb, w - 1), 0, 0, 0)),
                pl.BlockSpec((None, None, None, SW_HEADS, SW_HEAD_DIM), lambda i, k: (i, gi, 1, 0, 0)),
                pl.BlockSpec((None, None, None, SW_HEADS, SW_HEAD_DIM), lambda i, k: (i, gi, 2, 0, 0)),
            ],
            out_specs=pl.BlockSpec((None, wb) + tail, lambda i, k: (i, k, 0, 0, 0)),
            out_shape=jax.ShapeDtypeStruct(c.shape, F32),
            compiler_params=_cparams(("arbitrary", "arbitrary")),
            name=f"cache_shift_g{gi}",
        )(c, c, q5, q5))
    return outs


def _kv_window_kernel(k_ref, v_ref, o_ref, *, rows):
    for kv, src in enumerate((k_ref, v_ref)):
        for h in range(SW_HEADS):
            o_ref[pl.ds(kv * SW_HEADS + h, rows, stride=2 * SW_HEADS), :] = \
                src[:, h * SW_HEAD_DIM:(h + 1) * SW_HEAD_DIM]


def _kv_window(qkv, gi, win, batch, seq):
    wlen = min(win, seq)
    rows = 128
    nt = wlen // rows
    per_row = 2 * SW_HEADS
    src = lambda which: pl.BlockSpec(
        (rows, SW_WIDTH), lambda b, t: ((b * seq + seq - wlen) // rows + t, gi * 3 + which))
    out = pl.pallas_call(
        functools.partial(_kv_window_kernel, rows=rows),
        grid=(batch, nt),
        in_specs=[src(1), src(2)],
        out_specs=pl.BlockSpec((None, rows * per_row, SW_HEAD_DIM), lambda b, t: (b, t, 0)),
        out_shape=jax.ShapeDtypeStruct((batch, wlen * per_row, SW_HEAD_DIM), F32),
        compiler_params=_cparams(("arbitrary", "arbitrary")),
        name=f"kv_window_g{gi}",
    )(qkv, qkv)
    return out.reshape(batch, wlen, 2, SW_HEADS, SW_HEAD_DIM)


def _router_kernel(x_ref, gn_ref, sc_ref, sh_ref, wr_ref, br_ref, *rest, tm):
    h_ref, ids_ref, gates_ref, hf_ref = rest[-4:]
    _norm_mod_into(x_ref, gn_ref, sc_ref, sh_ref, [hf_ref], tm)
    for s in range(ROW_TILES):
        h_ref[pl.ds(s, tm, stride=ROW_TILES), :] = hf_ref[:, s * LANES:(s + 1) * LANES]
    logits = jnp.dot(hf_ref[...].astype(BF16), wr_ref[...].astype(BF16),
                     preferred_element_type=F32) + br_ref[...]
    lane = lax.broadcasted_iota(jnp.int32, logits.shape, 1)
    big = jnp.int32(1 << 20)
    is_g = (lane >= MOE_EXPERTS) & (lane < MOE_EXPERTS + MOE_GROUPS)
    glog = jnp.where(is_g, logits, -jnp.inf)
    gmax = jnp.max(glog, axis=-1, keepdims=True)
    gsel = jnp.min(jnp.where(glog == gmax, lane - MOE_EXPERTS, big), axis=-1, keepdims=True)
    gsum = jnp.sum(jnp.where(is_g, jnp.exp(glog - gmax), 0.0), axis=-1, keepdims=True)
    pg = 1.0 / gsum
    in_grp = (lane < MOE_EXPERTS) & ((lane // MOE_PER_GROUP) == gsel)
    el = jnp.where(in_grp, logits, -jnp.inf)
    v1 = jnp.max(el, axis=-1, keepdims=True)
    i1 = jnp.min(jnp.where(el == v1, lane, big), axis=-1, keepdims=True)
    el2 = jnp.where(lane == i1, -jnp.inf, el)
    v2 = jnp.max(el2, axis=-1, keepdims=True)
    i2 = jnp.min(jnp.where(el2 == v2, lane, big), axis=-1, keepdims=True)
    e2 = jnp.exp(v2 - v1)
    den = 1.0 + e2
    ids_ref[...] = jnp.where(lane == 0, i1, jnp.where(lane == 1, i2, 0))
    gates_ref[...] = jnp.where(lane == 0, pg * (1.0 / den), jnp.where(lane == 1, pg * (e2 / den), 0.0))


def _router(x, gn, mod, wr, br, hcat, n_tok, *, tm, rows_per_batch, row0):
    m = x.shape[0]
    blk0 = row0 // tm
    bpb = max(rows_per_batch // tm, 1)
    nb = m // tm
    n_steps = nb + (1 if hcat is None and n_tok > m else 0)
    cl = lambda i: jnp.minimum(i, nb - 1)
    msp = lambda col: pl.BlockSpec((None, mod.shape[1], D_MODEL), lambda i: (cl(i) // bpb, 0, col))
    in_specs = [
        pl.BlockSpec((tm, D_MODEL), lambda i: (cl(i), 0)),
        pl.BlockSpec((1, D_MODEL), lambda i: (0, 0)),
        msp(4), msp(3),
        pl.BlockSpec((D_MODEL, LANES), lambda i: (0, 0)),
        pl.BlockSpec((1, LANES), lambda i: (0, 0)),
    ]
    args = [x, gn.reshape(1, D_MODEL), mod, mod, wr, br]
    aliases = {}
    if hcat is not None:
        in_specs.append(pl.BlockSpec(memory_space=pl.ANY))
        args.append(hcat)
        aliases = {len(args) - 1: 0}
    return pl.pallas_call(
        functools.partial(_router_kernel, tm=tm),
        grid=(n_steps,),
        in_specs=in_specs,
        out_specs=[pl.BlockSpec((tm * ROW_TILES, LANES), lambda i: (blk0 + i, 0)),
                   pl.BlockSpec((tm, LANES), lambda i: (cl(i), 0)),
                   pl.BlockSpec((tm, LANES), lambda i: (cl(i), 0))],
        out_shape=[jax.ShapeDtypeStruct((n_tok * ROW_TILES, LANES), F32),
                   jax.ShapeDtypeStruct((m, LANES), jnp.int32),
                   jax.ShapeDtypeStruct((m, LANES), F32)],
        scratch_shapes=[pltpu.VMEM((tm, D_MODEL), F32)],
        input_output_aliases=aliases,
        compiler_params=_cparams(("arbitrary",)),
        name="moe_router",
    )(*args)


def _experts_kernel(tok_ref, run_ref, qlo_ref, qhi_ref, ce_ref, ck_ref, cs_ref, meta_ref,
                    h_hbm, wgu_hbm, wd_hbm, y_ref,
                    rows_ref, stg_ref, wgu_ref, wd_ref, row_sem, w_sem, *, layer):
    b = pl.program_id(0)
    n_used, q_total = meta_ref[0], meta_ref[1]
    rt = ROW_TILES

    def issue_rows(blk, slot):
        def body(g, carry):
            for u in range(DMA_UNROLL):
                r = g * DMA_UNROLL + u
                tok = tok_ref[blk * MOE_ROWS + r]
                pltpu.make_async_copy(h_hbm.at[pl.ds(pl.multiple_of(tok * rt, rt), rt)],
                                      rows_ref.at[slot, pl.ds(pl.multiple_of(r * rt, rt), rt)],
                                      row_sem.at[slot]).start()
            return carry

        lax.fori_loop(0, MOE_ROWS // DMA_UNROLL, body, 0)

    def chunk_start(q):
        e, k, st = ce_ref[q], ck_ref[q], q % W_STAGES
        row0 = lambda kk: pl.ds(pl.multiple_of(kk * W_CHUNK, W_CHUNK), W_CHUNK)

        @pl.when(k < GU_CHUNKS)
        def _():
            pltpu.make_async_copy(wgu_hbm.at[layer, e, row0(k)], stg_ref.at[st], w_sem.at[st]).start()

        @pl.when(k >= GU_CHUNKS)
        def _():
            pltpu.make_async_copy(wd_hbm.at[layer, e, row0(k - GU_CHUNKS)], stg_ref.at[st], w_sem.at[st]).start()

    def chunk_process(q):
        k, slot, st = ck_ref[q], cs_ref[q], q % W_STAGES
        pltpu.make_async_copy(wgu_hbm.at[layer, 0, pl.ds(0, W_CHUNK)], stg_ref.at[st], w_sem.at[st]).wait()
        val = stg_ref[st].astype(BF16)

        @pl.when(k < GU_CHUNKS)
        def _():
            wgu_ref[slot, pl.ds(pl.multiple_of(k * W_CHUNK, W_CHUNK), W_CHUNK), :] = val

        @pl.when(k >= GU_CHUNKS)
        def _():
            wd_ref[slot, pl.ds(pl.multiple_of((k - GU_CHUNKS) * W_CHUNK, W_CHUNK), W_CHUNK), :] = val

        @pl.when(q + W_STAGES < q_total)
        def _():
            chunk_start(q + W_STAGES)

    def process_range(lo, hi):
        def body(q, carry):
            chunk_process(q)
            return carry

        lax.fori_loop(lo, hi, body, 0)

    @pl.when(b == 0)
    def _():
        issue_rows(0, 0)
        for q in range(W_STAGES):
            chunk_start(q)
        process_range(0, RUN_CHUNKS)

    @pl.when(b + 1 < n_used)
    def _():
        issue_rows(b + 1, (b + 1) % 2)

    @pl.when(b < n_used)
    def _():
        rslot = b % 2
        pltpu.make_async_copy(h_hbm.at[pl.ds(0, MOE_ROWS * rt)], rows_ref.at[rslot], row_sem.at[rslot]).wait()
        x = jnp.concatenate(
            [rows_ref[rslot, pl.ds(s, MOE_ROWS, stride=rt), :].astype(BF16) for s in range(rt)], axis=1)
        wslot = run_ref[b] % 2
        for sl in range(2):
            @pl.when(wslot == sl)
            def _():
                au = jnp.dot(x, wgu_ref[sl], preferred_element_type=F32)
                hmid = (_silu(au[:, :MOE_FF]) * au[:, MOE_FF:]).astype(BF16)
                y = jnp.dot(hmid, wd_ref[sl], preferred_element_type=F32)
                for s in range(rt):
                    y_ref[pl.ds(s, MOE_ROWS, stride=rt), :] = y[:, s * LANES:(s + 1) * LANES]

        process_range(qlo_ref[b], qhi_ref[b])

    @pl.when(b >= n_used)
    def _():
        y_ref[...] = jnp.zeros_like(y_ref)


def _moe_experts(hcat, w_gu, w_down, layer, row_tok, blk_run, qlo, qhi, chunk_e, chunk_k, chunk_s, meta, n_blocks):
    any_spec = pl.BlockSpec(memory_space=pl.ANY)
    return pl.pallas_call(
        functools.partial(_experts_kernel, layer=layer),
        grid_spec=pltpu.PrefetchScalarGridSpec(
            num_scalar_prefetch=8,
            grid=(n_blocks,),
            in_specs=[any_spec, any_spec, any_spec],
            out_specs=pl.BlockSpec((MOE_ROWS * ROW_TILES, LANES), lambda b, *_: (b, 0)),
            scratch_shapes=[pltpu.VMEM((2, MOE_ROWS * ROW_TILES, LANES), F32),
                            pltpu.VMEM((W_STAGES, W_CHUNK, D_MODEL), F32),
                            pltpu.VMEM((2, D_MODEL, 2 * MOE_FF), BF16),
                            pltpu.VMEM((2, MOE_FF, D_MODEL), BF16),
                            pltpu.SemaphoreType.DMA((2,)),
                            pltpu.SemaphoreType.DMA((W_STAGES,))],
        ),
        out_shape=jax.ShapeDtypeStruct((n_blocks * MOE_ROWS * ROW_TILES, LANES), F32),
        compiler_params=pltpu.CompilerParams(dimension_semantics=("arbitrary",), vmem_limit_bytes=VMEM_LIMIT,
                                             disable_bounds_checks=True),
        name="moe_experts",
    )(row_tok, blk_run, qlo, qhi, chunk_e, chunk_k, chunk_s, meta, hcat, w_gu, w_down)


def _combine_kernel(dest_ref, y_hbm, x_ref, g_ref, gates_ref, o_ref, y0_ref, y1_ref, sem, *, tm, tok0):
    i = pl.program_id(0)
    rt = ROW_TILES

    def issue(tile, slot):
        def body(g, carry):
            for u in range(DMA_UNROLL // 2):
                r = g * (DMA_UNROLL // 2) + u
                a = (tok0 + tile * tm + r) * MOE_TOP_K
                for j, y_ref in enumerate((y0_ref, y1_ref)):
                    pltpu.make_async_copy(y_hbm.at[pl.ds(pl.multiple_of(dest_ref[a + j] * rt, rt), rt)],
                                          y_ref.at[slot, pl.ds(pl.multiple_of(r * rt, rt), rt)],
                                          sem.at[slot]).start(priority=j)
            return carry

        lax.fori_loop(0, tm // (DMA_UNROLL // 2), body, 0)

    @pl.when(i == 0)
    def _():
        issue(0, 0)

    @pl.when(i + 1 < pl.num_programs(0))
    def _():
        issue(i + 1, (i + 1) % 2)

    slot = i % 2
    for y_ref in (y0_ref, y1_ref):
        pltpu.make_async_copy(y_hbm.at[pl.ds(0, tm * rt)], y_ref.at[slot], sem.at[slot]).wait()
    gt = gates_ref[...]
    g0 = jnp.broadcast_to(gt[:, 0:1], (tm, LANES))
    g1 = jnp.broadcast_to(gt[:, 1:2], (tm, LANES))
    for s in range(rt):
        cs = slice(s * LANES, (s + 1) * LANES)
        rows = pl.ds(s, tm, stride=rt)
        y = y0_ref[slot, rows, :] * g0 + y1_ref[slot, rows, :] * g1
        o_ref[:, cs] = x_ref[:, cs] + g_ref[:, cs] * y


def _moe_combine(y, dest, x, mod, gates, *, tm, rows_per_batch, tok0):
    m = x.shape[0]
    bpb = max(rows_per_batch // tm, 1)
    return pl.pallas_call(
        functools.partial(_combine_kernel, tm=tm, tok0=tok0),
        grid_spec=pltpu.PrefetchScalarGridSpec(
            num_scalar_prefetch=1,
            grid=(m // tm,),
            in_specs=[pl.BlockSpec(memory_space=pl.ANY),
                      pl.BlockSpec((tm, D_MODEL), lambda i, d: (i, 0)),
                      pl.BlockSpec((None, mod.shape[1], D_MODEL), lambda i, d: (i // bpb, 0, 5)),
                      pl.BlockSpec((tm, LANES), lambda i, d: (i, 0))],
            out_specs=pl.BlockSpec((tm, D_MODEL), lambda i, d: (i, 0)),
            scratch_shapes=[pltpu.VMEM((2, tm * ROW_TILES, LANES), F32)] * 2
            + [pltpu.SemaphoreType.DMA((2,))],
        ),
        out_shape=jax.ShapeDtypeStruct((m, D_MODEL), F32),
        compiler_params=pltpu.CompilerParams(dimension_semantics=("arbitrary",), vmem_limit_bytes=VMEM_LIMIT,
                                             disable_bounds_checks=True),
        name="moe_combine",
    )(dest, y, x, mod, gates)


def _moe_layer(xp, xs, mod_p, mod_s, gn, w_group, b_group, w_router, b_router, w_gu, w_down, layer, seq):
    n_p, n_s = xp.shape[0], xs.shape[0]
    n_tok = n_p + n_s
    pad = LANES - MOE_EXPERTS - MOE_GROUPS
    wr = jnp.concatenate([w_router, w_group, jnp.zeros((D_MODEL, pad), F32)], axis=1)
    br = jnp.concatenate([b_router, b_group, jnp.zeros((pad,), F32)]).reshape(1, LANES)
    hcat, ids_p, gates_p = _router(xp, gn, mod_p, wr, br, None, n_tok, tm=256, rows_per_batch=seq, row0=0)
    hcat, ids_s, gates_s = _router(xs, gn, mod_s, wr, br, hcat, n_tok, tm=n_s, rows_per_batch=n_s, row0=n_p)

    expert = jnp.concatenate([ids_p[:, :MOE_TOP_K], ids_s[:, :MOE_TOP_K]], axis=0).reshape(-1)
    n_assign = n_tok * MOE_TOP_K
    n_blocks = -(-n_assign // MOE_ROWS) + MOE_EXPERTS
    onehot = (expert[:, None] == jnp.arange(MOE_EXPERTS, dtype=jnp.int32)[None, :]).astype(jnp.int32)
    csum = jnp.cumsum(onehot, axis=0)
    rank = jnp.sum(csum * onehot, axis=1) - 1
    counts = csum[-1]
    padded = (counts + MOE_ROWS - 1) // MOE_ROWS * MOE_ROWS
    pend = jnp.cumsum(padded)
    pstart = pend - padded
    dest = (pstart[expert] + rank).astype(jnp.int32)
    tok = jnp.arange(n_assign, dtype=jnp.int32) // MOE_TOP_K
    row_tok = jnp.zeros((n_blocks * MOE_ROWS,), jnp.int32).at[dest].set(tok, unique_indices=True)
    blk_ids = jnp.arange(n_blocks, dtype=jnp.int32)
    blk_expert = jnp.minimum(jnp.searchsorted(pend, blk_ids * MOE_ROWS, side="right"),
                             MOE_EXPERTS - 1).astype(jnp.int32)
    n_used = (pend[-1:] // MOE_ROWS).astype(jnp.int32)
    nonempty = counts > 0
    n_runs = jnp.sum(nonempty.astype(jnp.int32))
    run_of_expert = jnp.cumsum(nonempty.astype(jnp.int32)) - 1
    run_expert = jnp.argsort(jnp.where(nonempty, 0, 1), stable=True).astype(jnp.int32)
    blk_run = run_of_expert[blk_expert].astype(jnp.int32)
    j_in_run = blk_ids - pstart[blk_expert] // MOE_ROWS
    n_in_run = jnp.maximum(padded[blk_expert] // MOE_ROWS, 1)
    q_total = n_runs * RUN_CHUNKS
    streams = (blk_ids < n_used[0]) & (blk_run + 1 < n_runs)
    base = (blk_run + 1) * RUN_CHUNKS
    qlo = jnp.where(streams, base + (RUN_CHUNKS * j_in_run) // n_in_run, q_total).astype(jnp.int32)
    qhi = jnp.where(streams, base + (RUN_CHUNKS * (j_in_run + 1)) // n_in_run, q_total).astype(jnp.int32)
    chunk_ids = jnp.arange(MOE_EXPERTS * RUN_CHUNKS, dtype=jnp.int32)
    chunk_e = run_expert[chunk_ids // RUN_CHUNKS]
    chunk_k = chunk_ids % RUN_CHUNKS
    chunk_s = (chunk_ids // RUN_CHUNKS) % 2
    meta = jnp.concatenate([n_used, q_total[None].astype(jnp.int32)])

    y = _moe_experts(hcat, w_gu, w_down, layer, row_tok, blk_run, qlo, qhi, chunk_e, chunk_k, chunk_s, meta,
                     n_blocks)
    xp_new = _moe_combine(y, dest, xp, mod_p, gates_p, tm=256, rows_per_batch=seq, tok0=0)
    xs_new = _moe_combine(y, dest, xs, mod_s, gates_s, tm=n_s, rows_per_batch=n_s, tok0=n_p)
    return xp_new, xs_new


def _final_norm_kernel(x_ref, g_ref, o_ref):
    x = x_ref[...]
    inv = lax.rsqrt(jnp.mean(x * x, axis=-1, keepdims=True) + EPS)
    o_ref[...] = (x * inv) * g_ref[...]


def _final_norm(x, g, tm):
    m = x.shape[0]
    return pl.pallas_call(
        _final_norm_kernel,
        grid=(m // tm,),
        in_specs=[pl.BlockSpec((tm, D_MODEL), lambda i: (i, 0)), pl.BlockSpec((1, D_MODEL), lambda i: (0, 0))],
        out_specs=pl.BlockSpec((tm, D_MODEL), lambda i: (i, 0)),
        out_shape=jax.ShapeDtypeStruct((m, D_MODEL), F32),
        compiler_params=_cparams(("arbitrary",)),
        name="final_norm",
    )(x, g.reshape(1, D_MODEL))


def _rope_tables(pos):
    half = SW_HEAD_DIM // 2
    inv_freq = ROPE_THETA ** (-jnp.arange(half, dtype=F32) / half)
    ang = pos.astype(F32)[:, None] * inv_freq[None, :]
    cos, sin = jnp.cos(ang), jnp.sin(ang)
    return jnp.concatenate([cos, cos], axis=1), jnp.concatenate([-sin, sin], axis=1)


def kernel(x_prompt, x_sample, state_hgrn, state_conv, cache_swa_g1, cache_swa_g2, cache_swa_g3, c_prompt, c_sample, ada_w, ada_b, norm_mix, norm_ffn, norm_final, hg_w_in, hg_w_out, hg_norm, hg_lower, cv_w_in, cv_w_conv, cv_w_out, sw_w_in, sw_w_out, moe_w_group, moe_b_group, moe_w_router, moe_b_router, moe_w_gu, moe_w_down):
    bp, seq, d = x_prompt.shape
    bs = x_sample.shape[0]
    n_p = bp * seq
    xp = x_prompt.reshape(n_p, d)
    xs = x_sample.reshape(bs, d)

    mod_all = _ada_mod(jnp.concatenate([c_prompt, c_sample], axis=0), ada_w, ada_b)
    sm = jax.nn.softmax(hg_lower.astype(F32), axis=0)
    lower = jnp.cumsum(sm, axis=0) - sm[0]
    rope_p = _rope_tables(jnp.arange(seq))
    rope_s = _rope_tables(jnp.full((bs,), PAST_LEN, jnp.int32))

    tm_p, tn_p = 1024, 512
    hg_p, hg_s, cv_p, cv_s = [], [], [], []
    sw_p, sw_s = None, None
    for i in range(DEPTH):
        kind, j = i % N_MIXERS, i // N_MIXERS
        mod_p = mod_all[i, :bp].reshape(bp, 1, 6 * d)
        mod_s = mod_all[i, bp:].reshape(1, bs, 6 * d)
        pp = dict(tm=tm_p, tn=tn_p)
        pp_in = dict(tm=tm_p, tn=1024)
        ps = dict(tm=bs, tn=1024)
        hp = _norm_mod(xp, norm_mix[i], mod_p, 0, 1, tm=512, rows_per_batch=seq)
        hs = _norm_mod(xs, norm_mix[i], mod_s, 0, 1, tm=bs, rows_per_batch=bs)
        if kind == 0:
            proj_p = _mm(hp, hg_w_in, j, **pp_in)
            proj_s = _mm(hs, hg_w_in, j, **ps)
            lhs_p, st_p = _gla_prompt(proj_p, lower[j], hg_norm[j], bp, seq)
            lhs_s, st_s = _gla_step(proj_s, lower[j], hg_norm[j], state_hgrn, j)
            hg_p.append(st_p)
            hg_s.append(st_s)
            w_out = hg_w_out
        elif kind == 1:
            proj_p = _mm(hp, cv_w_in, j, **pp_in)
            proj_s = _mm(hs, cv_w_in, j, **ps)
            lhs_p, tail_p = _conv_prompt(proj_p, cv_w_conv[j], bp, seq)
            lhs_s, tail_s = _conv_step(proj_s, cv_w_conv[j], state_conv[j])
            cv_p.append(tail_p)
            cv_s.append(tail_s)
            w_out = cv_w_out
        else:
            qkv_p = _mm(hp, sw_w_in, j, tm=512, tn=SW_WIDTH, rope=rope_p)
            qkv_s = _mm(hs, sw_w_in, j, rope=rope_s, **ps)
            os_, ls_ = [], []
            for gi, (win, dil) in enumerate(SW_GROUPS):
                o_g, l_g = _swa_prompt_group(qkv_p, gi, dil, bp, seq)
                os_.append(o_g)
                ls_.append(l_g)
            lhs_p = _swa_merge(os_, ls_)
            caches = (cache_swa_g1[j], cache_swa_g2[j], cache_swa_g3[j])
            lhs_s = _swa_step(qkv_s, caches)
            w_out = sw_w_out
            sw_p = [_kv_window(qkv_p, gi, win, bp, seq)[None] for gi, (win, _) in enumerate(SW_GROUPS)]
            sw_s = [c[None] for c in _cache_shift(qkv_s, caches)]
        xp = _mm(lhs_p, w_out, j, resgate=(xp, mod_p, 2, seq), **pp)
        xs = _mm(lhs_s, w_out, j, resgate=(xs, mod_s, 2, bs), **ps)
        xp, xs = _moe_layer(xp, xs, mod_p, mod_s, norm_ffn[i], moe_w_group[i], moe_b_group[i],
                            moe_w_router[i], moe_b_router[i], moe_w_gu, moe_w_down, i, seq)

    y_p = _final_norm(xp, norm_final, 512).reshape(bp, seq, d)
    y_s = _final_norm(xs, norm_final, bs).reshape(bs, 1, d)
    return (y_p, y_s, jnp.stack(hg_p), jnp.stack(hg_s), jnp.stack(cv_p), jnp.stack(cv_s),
            sw_p[0], sw_s[0], sw_p[1], sw_s[1], sw_p[2], sw_s[2])
```

```python
import functools

import jax
import jax.numpy as jnp
from jax import lax
from jax.experimental import pallas as pl
from jax.experimental.pallas import tpu as pltpu

F32 = jnp.float32
BF16 = jnp.bfloat16

D_MODEL = 2048
DEPTH = 4
N_MIXERS = 3
EPS = 1e-6
MASK_VALUE = -1e30
F_FLOOR = 1e-30
HG_DK = 128
HG_HEADS = D_MODEL // HG_DK
HG_CHUNK = 64
HG_SUB = 16
CONV_WIDTH = 3
SW_GROUPS = ((128, 1), (512, 4), (2048, 16))
SW_HEADS = 8
SW_HEAD_DIM = 128
SW_WIDTH = SW_HEADS * SW_HEAD_DIM
SW_BAND = 128
ROPE_THETA = 10000.0
MOE_GROUPS = 4
MOE_PER_GROUP = 8
MOE_EXPERTS = MOE_GROUPS * MOE_PER_GROUP
MOE_TOP_K = 2
MOE_FF = 1024
PAST_LEN = 16384

LANES = 128
SUBLANES = 8
VMEM_LIMIT = 52 * 1024 * 1024
MOE_ROWS = 256
ROW_TILES = D_MODEL // LANES
DMA_UNROLL = 8
W_CHUNK = 256
GU_CHUNKS = D_MODEL // W_CHUNK
RUN_CHUNKS = GU_CHUNKS + MOE_FF // W_CHUNK
W_STAGES = 4


def _cparams(sem):
    return pltpu.CompilerParams(dimension_semantics=sem, vmem_limit_bytes=VMEM_LIMIT)


def _sigmoid(x):
    return 1.0 / (1.0 + jnp.exp(-x))


def _silu(x):
    return x * _sigmoid(x)


def _bf16_round(x):
    return x.astype(BF16).astype(F32)


def _ada_kernel(c_ref, w_ref, b_ref, o_ref):
    c = c_ref[...]
    cond = _silu(c).astype(BF16)
    o_ref[...] = jnp.dot(cond, w_ref[...].astype(BF16), preferred_element_type=F32) + b_ref[...]


def _ada_mod(c_all, ada_w, ada_b):
    rows = c_all.shape[0]
    n = ada_w.shape[-1]
    tn = 1024
    return pl.pallas_call(
        _ada_kernel,
        grid=(DEPTH, n // tn),
        in_specs=[
            pl.BlockSpec((rows, D_MODEL), lambda l, j: (0, 0)),
            pl.BlockSpec((None, D_MODEL, tn), lambda l, j: (l, 0, j)),
            pl.BlockSpec((None, 1, tn), lambda l, j: (l, 0, j)),
        ],
        out_specs=pl.BlockSpec((None, rows, tn), lambda l, j: (l, 0, j)),
        out_shape=jax.ShapeDtypeStruct((DEPTH, rows, n), F32),
        compiler_params=_cparams(("arbitrary", "arbitrary")),
        name="ada_mod",
    )(c_all, ada_w, ada_b.reshape(DEPTH, 1, n))


def _norm_mod_rows(x, gn, sc, sh):
    inv = lax.rsqrt(jnp.mean(x * x, axis=-1, keepdims=True) + EPS)
    return (x * inv) * gn * (1.0 + sc) + sh


def _norm_mod_into(x_ref, gn_ref, sc_ref, sh_ref, dst_refs, tm):
    ch = min(tm, 128)
    per_row = sc_ref.shape[0] != 1

    def body(c, carry):
        rs = pl.ds(pl.multiple_of(c * ch, ch), ch)
        sc = sc_ref[rs, :] if per_row else sc_ref[...]
        sh = sh_ref[rs, :] if per_row else sh_ref[...]
        h = _norm_mod_rows(x_ref[rs, :], gn_ref[...], sc, sh)
        for d in dst_refs:
            d[rs, :] = h.astype(d.dtype)
        return carry

    lax.fori_loop(0, tm // ch, body, 0)


def _norm_mod_kernel(x_ref, gn_ref, sc_ref, sh_ref, o_ref, *, tm):
    _norm_mod_into(x_ref, gn_ref, sc_ref, sh_ref, [o_ref], tm)


def _norm_mod(x, gn, mod, shift_col, scale_col, *, tm, rows_per_batch):
    m = x.shape[0]
    bpb = max(rows_per_batch // tm, 1)
    msp = lambda col: pl.BlockSpec((None, mod.shape[1], D_MODEL), lambda i: (i // bpb, 0, col))
    return pl.pallas_call(
        functools.partial(_norm_mod_kernel, tm=tm),
        grid=(m // tm,),
        in_specs=[pl.BlockSpec((tm, D_MODEL), lambda i: (i, 0)),
                  pl.BlockSpec((1, D_MODEL), lambda i: (0, 0)),
                  msp(scale_col), msp(shift_col)],
        out_specs=pl.BlockSpec((tm, D_MODEL), lambda i: (i, 0)),
        out_shape=jax.ShapeDtypeStruct((m, D_MODEL), BF16),
        compiler_params=_cparams(("arbitrary",)),
        name="norm_mod",
    )(x, gn.reshape(1, D_MODEL), mod, mod)


def _rope_tile(acc, cos, sin_signed):
    outs = []
    for h in range(acc.shape[1] // SW_HEAD_DIM):
        xh = acc[:, h * SW_HEAD_DIM:(h + 1) * SW_HEAD_DIM]
        outs.append(xh * cos + pltpu.roll(xh, SW_HEAD_DIM // 2, 1) * sin_signed)
    return jnp.concatenate(outs, axis=1)


def _mm_kernel(x_ref, w_ref, *rest, epilogue):
    o_ref, wb_ref = rest[-2:]
    j, i = pl.program_id(0), pl.program_id(1)

    @pl.when(i == 0)
    def _():
        wb_ref[...] = w_ref[...].astype(BF16)

    acc = jnp.dot(x_ref[...].astype(BF16), wb_ref[...], preferred_element_type=F32)
    if epilogue == "rope":
        cos_ref, sin_ref = rest[:2]
        is_qk = (j % 3) < 2
        cos = jnp.where(is_qk, cos_ref[...], 1.0)
        sin = jnp.where(is_qk, sin_ref[...], 0.0)
        o_ref[...] = _rope_tile(acc, cos, sin)
    elif epilogue == "resgate":
        res_ref, g_ref = rest[:2]
        o_ref[...] = res_ref[...] + g_ref[...] * acc
    else:
        o_ref[...] = acc


def _mm(x, w, wl, *, tm, tn, rope=None, resgate=None):
    m, k = x.shape
    n = w.shape[2]
    in_specs = [pl.BlockSpec((tm, k), lambda j, i: (i, 0)),
                pl.BlockSpec((None, k, tn), lambda j, i: (wl, 0, j))]
    args = [x, w]
    epilogue = "none"
    if rope is not None:
        epilogue = "rope"
        nblk = rope[0].shape[0] // tm
        in_specs += [pl.BlockSpec((tm, SW_HEAD_DIM), lambda j, i: (i % nblk, 0))] * 2
        args += list(rope)
    if resgate is not None:
        epilogue = "resgate"
        res, mod, gate_col, rows_per_batch = resgate
        bpb = max(rows_per_batch // tm, 1)
        per = D_MODEL // tn
        in_specs += [pl.BlockSpec((tm, tn), lambda j, i: (i, j)),
                     pl.BlockSpec((None, mod.shape[1], tn), lambda j, i: (i // bpb, 0, gate_col * per + j))]
        args += [res, mod]
    return pl.pallas_call(
        functools.partial(_mm_kernel, epilogue=epilogue),
        grid=(n // tn, m // tm),
        in_specs=in_specs,
        out_specs=pl.BlockSpec((tm, tn), lambda j, i: (i, j)),
        out_shape=jax.ShapeDtypeStruct((m, n), F32),
        scratch_shapes=[pltpu.VMEM((k, tn), BF16)],
        compiler_params=_cparams(("arbitrary", "arbitrary")),
        name="mm_" + epilogue,
    )(*args)


def _hgrn_gates(qp, fp, lb):
    q = _silu(qp)
    f = lb + (1.0 - lb) * _sigmoid(fp)
    logf = jnp.log(jnp.maximum(f, F_FLOOR))
    k = (1.0 - lb) * _sigmoid(-fp)
    return q, logf, k


def _head_norm_gate(o, gn, gp):
    inv = lax.rsqrt(jnp.mean(o * o, axis=-1, keepdims=True) + EPS)
    return (o * inv) * gn * _silu(gp)


def _gla_kernel(q_ref, f_ref, i_ref, g_ref, lb_ref, gn_ref, o_ref, s_out_ref, st_ref, *, n_chunks, n_hb):
    c_len, n_sub = HG_CHUNK, HG_CHUNK // HG_SUB
    t = pl.program_id(2)

    @pl.when(t == 0)
    def _():
        st_ref[...] = jnp.zeros_like(st_ref)

    row = lax.broadcasted_iota(jnp.int32, (c_len, HG_DK), 0)
    r4 = lax.broadcasted_iota(jnp.int32, (c_len, n_sub * c_len), 0)
    c4 = lax.broadcasted_iota(jnp.int32, (c_len, n_sub * c_len), 1)
    mask4 = ((r4 // HG_SUB) == (c4 // c_len)) & ((c4 % c_len) <= r4)
    nt = (((1,), (1,)), ((), ()))
    tn = (((0,), (0,)), ((), ()))

    for c, hh in [(c, hh) for c in range(n_chunks) for hh in range(n_hb)]:
        rs = pl.ds(c * c_len, c_len)
        cs = slice(hh * HG_DK, (hh + 1) * HG_DK)
        lb, gn = lb_ref[:, cs], gn_ref[:, cs]
        q, logf, k = _hgrn_gates(q_ref[rs, cs], f_ref[rs, cs], lb)
        v = i_ref[rs, cs]
        g = logf
        sh = 1
        while sh < c_len:
            g = g + jnp.where(row >= sh, pltpu.roll(g, sh, 0), 0.0)
            sh *= 2
        refs = [g[HG_SUB * i + HG_SUB // 2 - 1:HG_SUB * i + HG_SUB // 2, :] for i in range(n_sub)]
        mrows = jnp.concatenate([jnp.broadcast_to(r, (HG_SUB, HG_DK)) for r in refs], axis=0)
        qh = (q * jnp.exp(g - mrows)).astype(BF16)
        khs = [k * jnp.exp(jnp.where(row < HG_SUB * (i + 1), refs[i] - g, -jnp.inf)) for i in range(n_sub)]
        kh = jnp.concatenate(khs, axis=0).astype(BF16)
        a4 = lax.dot_general(qh, kh, nt, preferred_element_type=F32)
        a4 = jnp.where(mask4, a4, 0.0).astype(BF16)
        vb = v.astype(BF16)
        v4 = jnp.concatenate([vb] * n_sub, axis=0)
        st = st_ref[hh]
        o = jnp.dot(a4, v4, preferred_element_type=F32)
        o = o + lax.dot_general((q * jnp.exp(g)).astype(BF16), st.astype(BF16), nt,
                                preferred_element_type=F32)
        glast = g[c_len - 1:c_len, :]
        kd = (k * jnp.exp(glast - g)).astype(BF16)
        st_ref[hh] = st * jnp.exp(glast) + lax.dot_general(vb, kd, tn, preferred_element_type=F32)
        o_ref[rs, cs] = _head_norm_gate(o, gn, g_ref[rs, cs]).astype(o_ref.dtype)

    @pl.when(t == pl.num_programs(2) - 1)
    def _():
        for hh in range(n_hb):
            s_out_ref[hh] = st_ref[hh].T


def _gla_prompt(proj, lower, g_norm, batch, seq):
    tb = 512
    n_hb = 2
    nt = seq // tb
    hblk = HG_HEADS // n_hb
    wcol = n_hb * HG_DK

    def col(off):
        return pl.BlockSpec((tb, wcol), lambda b, hh, t: (b * nt + t, off * hblk + hh))

    return pl.pallas_call(
        functools.partial(_gla_kernel, n_chunks=tb // HG_CHUNK, n_hb=n_hb),
        grid=(batch, hblk, nt),
        in_specs=[col(0), col(1), col(2), col(3),
                  pl.BlockSpec((1, wcol), lambda b, hh, t: (0, hh)),
                  pl.BlockSpec((1, wcol), lambda b, hh, t: (0, hh))],
        out_specs=[pl.BlockSpec((tb, wcol), lambda b, hh, t: (b * nt + t, hh)),
                   pl.BlockSpec((None, n_hb, HG_DK, HG_DK), lambda b, hh, t: (b, hh, 0, 0))],
        out_shape=[jax.ShapeDtypeStruct((batch * seq, D_MODEL), BF16),
                   jax.ShapeDtypeStruct((batch, HG_HEADS, HG_DK, HG_DK), F32)],
        scratch_shapes=[pltpu.VMEM((n_hb, HG_DK, HG_DK), F32)],
        compiler_params=_cparams(("arbitrary", "arbitrary", "arbitrary")),
        name="hgrn_prompt",
    )(proj, proj, proj, proj, lower.reshape(1, D_MODEL), g_norm.reshape(1, D_MODEL))


def _gla_step_kernel(qt_ref, ft_ref, i_ref, g_ref, lbt_ref, gn_ref, s_ref, o_ref, s_out_ref):
    for h in range(HG_HEADS):
        cs = slice(h * HG_DK, (h + 1) * HG_DK)
        q, logf, k = _hgrn_gates(qt_ref[:, h:h + 1], ft_ref[:, h:h + 1], lbt_ref[:, h:h + 1])
        dec = jnp.exp(logf)
        s_old = s_ref[h]
        v = i_ref[:, cs]
        s_out_ref[h] = dec * s_old + k * v
        qd = _bf16_round(jnp.broadcast_to(q * dec, s_old.shape))
        o = jnp.sum(_bf16_round(s_old) * qd, axis=0, keepdims=True) + jnp.sum(q * k, axis=0, keepdims=True) * v
        o_ref[:, cs] = _head_norm_gate(o, gn_ref[:, cs], g_ref[:, cs])


def _gla_step(proj, lower, g_norm, states, layer):
    b = proj.shape[0]
    h = HG_HEADS
    pq = proj[:, :D_MODEL].reshape(b, h, HG_DK).transpose(0, 2, 1)
    pf = proj[:, D_MODEL:2 * D_MODEL].reshape(b, h, HG_DK).transpose(0, 2, 1)
    p3 = proj.reshape(b, 1, 4 * D_MODEL)
    lbt = lower.reshape(h, HG_DK).T
    vec = lambda col: pl.BlockSpec((None, 1, D_MODEL), lambda i: (i, 0, col))
    tr = pl.BlockSpec((None, HG_DK, h), lambda i: (i, 0, 0))
    o, s_new = pl.pallas_call(
        _gla_step_kernel,
        grid=(b,),
        in_specs=[tr, tr, vec(2), vec(3),
                  pl.BlockSpec((HG_DK, h), lambda i: (0, 0)),
                  pl.BlockSpec((1, D_MODEL), lambda i: (0, 0)),
                  pl.BlockSpec((None, None, h, HG_DK, HG_DK), lambda i: (layer, i, 0, 0, 0))],
        out_specs=[pl.BlockSpec((None, 1, D_MODEL), lambda i: (i, 0, 0)),
                   pl.BlockSpec((None, h, HG_DK, HG_DK), lambda i: (i, 0, 0, 0))],
        out_shape=[jax.ShapeDtypeStruct((b, 1, D_MODEL), F32),
                   jax.ShapeDtypeStruct(states.shape[1:], F32)],
        compiler_params=_cparams(("arbitrary",)),
        name="hgrn_step",
    )(pq, pf, p3, p3, lbt, g_norm.reshape(1, D_MODEL), states)
    return o.reshape(b, D_MODEL), s_new


def _conv_kernel(bg_ref, cg_ref, u_ref, hc_ref, hu_ref, w_ref, o_ref, tail_ref, *, tm):
    t = pl.program_id(1)
    z = cg_ref[...] * u_ref[...]
    hz = hc_ref[...] * hu_ref[...]
    hz = jnp.where(t == 0, 0.0, hz)
    z1p, z2p = hz[SUBLANES - 1:SUBLANES, :], hz[SUBLANES - 2:SUBLANES - 1, :]
    row = lax.broadcasted_iota(jnp.int32, z.shape, 0)
    z1 = jnp.where(row == 0, z1p, pltpu.roll(z, 1, 0))
    z2 = jnp.where(row == 0, z2p, jnp.where(row == 1, z1p, pltpu.roll(z, 2, 0)))
    y = z2 * w_ref[0:1, :] + z1 * w_ref[1:2, :] + z * w_ref[2:3, :]
    o_ref[...] = (bg_ref[...] * y).astype(o_ref.dtype)
    tail_ref[...] = z[tm - (CONV_WIDTH - 1):, :]


def _conv_prompt(proj, w_conv, batch, seq):
    tm = 256
    nt = seq // tm
    hb = tm // SUBLANES
    blk = lambda col: pl.BlockSpec((tm, D_MODEL), lambda b, t: (b * nt + t, col))
    halo = lambda col: pl.BlockSpec(
        (SUBLANES, D_MODEL), lambda b, t: (jnp.maximum((b * nt + t) * hb - 1, 0), col))
    return pl.pallas_call(
        functools.partial(_conv_kernel, tm=tm),
        grid=(batch, nt),
        in_specs=[blk(0), blk(1), blk(2), halo(1), halo(2),
                  pl.BlockSpec((CONV_WIDTH, D_MODEL), lambda b, t: (0, 0))],
        out_specs=[pl.BlockSpec((tm, D_MODEL), lambda b, t: (b * nt + t, 0)),
                   pl.BlockSpec((None, CONV_WIDTH - 1, D_MODEL), lambda b, t: (b, 0, 0))],
        out_shape=[jax.ShapeDtypeStruct((batch * seq, D_MODEL), BF16),
                   jax.ShapeDtypeStruct((batch, CONV_WIDTH - 1, D_MODEL), F32)],
        compiler_params=_cparams(("arbitrary", "arbitrary")),
        name="conv_prompt",
    )(proj, proj, proj, proj, proj, w_conv)


def _conv_step_kernel(bg_ref, cg_ref, u_ref, buf_ref, w_ref, o_ref, nb_ref):
    z = cg_ref[...] * u_ref[...]
    buf = buf_ref[...]
    y = buf[:, 0:1, :] * w_ref[0:1, :] + buf[:, 1:2, :] * w_ref[1:2, :] + z * w_ref[2:3, :]
    o_ref[...] = bg_ref[...] * y
    nb_ref[:, 0:1, :] = buf[:, 1:2, :]
    nb_ref[:, 1:2, :] = z


def _conv_step(proj, w_conv, buf):
    b = proj.shape[0]
    p3 = proj.reshape(b, 1, 3 * D_MODEL)
    vec = lambda col: pl.BlockSpec((b, 1, D_MODEL), lambda i: (0, 0, col))
    o, nb = pl.pallas_call(
        _conv_step_kernel,
        grid=(1,),
        in_specs=[vec(0), vec(1), vec(2),
                  pl.BlockSpec((b, CONV_WIDTH - 1, D_MODEL), lambda i: (0, 0, 0)),
                  pl.BlockSpec((CONV_WIDTH, D_MODEL), lambda i: (0, 0))],
        out_specs=[pl.BlockSpec((b, 1, D_MODEL), lambda i: (0, 0, 0)),
                   pl.BlockSpec((b, CONV_WIDTH - 1, D_MODEL), lambda i: (0, 0, 0))],
        out_shape=[jax.ShapeDtypeStruct((b, 1, D_MODEL), F32),
                   jax.ShapeDtypeStruct(buf.shape, F32)],
        compiler_params=_cparams(("arbitrary",)),
        name="conv_step",
    )(p3, p3, p3, buf, w_conv)
    return o.reshape(b, D_MODEL), nb


def _swa_kernel(*refs, dil, n_units, n_heads, has_prev):
    if has_prev:
        qc_ref, kp_ref, kc_ref, vp_ref, vc_ref, o_ref, l_ref = refs
    else:
        qc_ref, kc_ref, vc_ref, o_ref, l_ref = refs
    blk = SW_BAND
    n = pl.program_id(1)
    ri = lax.broadcasted_iota(jnp.int32, (blk, blk), 0)
    ci = lax.broadcasted_iota(jnp.int32, (blk, blk), 1)
    m_cur = ci <= ri
    m_prev_tri = ci >= ri
    lane = lax.broadcasted_iota(jnp.int32, (blk, LANES), 1)
    scale = SW_HEAD_DIM ** -0.5
    nt = (((1,), (1,)), ((), ()))

    for u in range(n_units):
        if dil == 1:
            rows = pl.ds(u * blk, blk)
            prev_src = None
            if has_prev:
                prev_src = (kp_ref, vp_ref, pl.ds(0, blk), True) if u == 0 else \
                    (kc_ref, vc_ref, pl.ds((u - 1) * blk, blk), False)
        else:
            rows = pl.ds(u, blk, stride=dil)
            prev_src = (kp_ref, vp_ref, rows, True) if has_prev else None

        def head(h, lse_acc, rows=rows, prev_src=prev_src):
            cs = pl.ds(0, SW_HEAD_DIM) if n_heads == 1 else \
                pl.ds(pl.multiple_of(h * SW_HEAD_DIM, SW_HEAD_DIM), SW_HEAD_DIM)
            q = qc_ref[rows, cs].astype(BF16)
            s_c = lax.dot_general(q, kc_ref[rows, cs].astype(BF16), nt, preferred_element_type=F32) * scale
            s_c = jnp.where(m_cur, s_c, MASK_VALUE)
            m = jnp.max(s_c, axis=-1, keepdims=True)
            if prev_src is not None:
                kr, vr, prow, first_only = prev_src
                s_p = lax.dot_general(q, kr[prow, cs].astype(BF16), nt, preferred_element_type=F32) * scale
                m_prev = (m_prev_tri & (n > 0)) if first_only else m_prev_tri
                s_p = jnp.where(m_prev, s_p, MASK_VALUE)
                m = jnp.maximum(m, jnp.max(s_p, axis=-1, keepdims=True))
            p_c = jnp.exp(s_c - m)
            l = jnp.sum(p_c, axis=-1, keepdims=True)
            o = jnp.dot(p_c.astype(BF16), vc_ref[rows, cs].astype(BF16), preferred_element_type=F32)
            if prev_src is not None:
                p_p = jnp.exp(s_p - m)
                l = l + jnp.sum(p_p, axis=-1, keepdims=True)
                o = o + jnp.dot(p_p.astype(BF16), vr[prow, cs].astype(BF16), preferred_element_type=F32)
            o_ref[rows, cs] = o / l
            lse = m + jnp.log(l)
            return jnp.where(lane == h, lse, lse_acc)

        lse0 = jnp.zeros((blk, LANES), F32)
        l_ref[rows, :] = head(0, lse0) if n_heads == 1 else lax.fori_loop(0, n_heads, head, lse0, unroll=2)


def _swa_prompt_group(qkv, gi, dil, batch, seq):
    span = SW_BAND * dil
    if dil == 1:
        span, n_units = 512, 4
    else:
        n_units = dil
    nb = seq // span
    has_prev = nb > 1
    hb = SW_HEADS if dil == 1 else 1
    n_hblk = SW_HEADS // hb
    wcol = hb * SW_HEAD_DIM
    per = SW_WIDTH // wcol
    pspan = SW_BAND if dil == 1 else span
    pmul = span // pspan

    def cur(which):
        return pl.BlockSpec((span, wcol), lambda b, n, hh: (b * nb + n, (gi * 3 + which) * per + hh))

    def prev(which):
        return pl.BlockSpec(
            (pspan, wcol),
            lambda b, n, hh: (jnp.maximum((b * nb + n) * pmul - 1, 0), (gi * 3 + which) * per + hh))

    if has_prev:
        in_specs = [cur(0), prev(1), cur(1), prev(2), cur(2)]
        args = [qkv] * 5
    else:
        in_specs = [cur(0), cur(1), cur(2)]
        args = [qkv] * 3
    return pl.pallas_call(
        functools.partial(_swa_kernel, dil=dil, n_units=n_units, n_heads=hb, has_prev=has_prev),
        grid=(batch, nb, n_hblk),
        in_specs=in_specs,
        out_specs=[pl.BlockSpec((span, wcol), lambda b, n, hh: (b * nb + n, hh)),
                   pl.BlockSpec((span, LANES), lambda b, n, hh: (b * nb + n, hh))],
        out_shape=[jax.ShapeDtypeStruct((batch * seq, SW_WIDTH), F32),
                   jax.ShapeDtypeStruct((batch * seq, n_hblk * LANES), F32)],
        compiler_params=_cparams(("arbitrary", "arbitrary", "arbitrary")),
        name=f"swa_prompt_g{gi}",
    )(*args)


def _merge_heads(o_refs, l_refs, rs, hbs):
    outs = []
    for h in range(SW_HEADS):
        cs = slice(h * SW_HEAD_DIM, (h + 1) * SW_HEAD_DIM)
        lses = []
        for l_ref, hb in zip(l_refs, hbs):
            lane = (h // hb) * LANES + h % hb
            lses.append(l_ref[rs, lane:lane + 1])
        mx = jnp.maximum(jnp.maximum(lses[0], lses[1]), lses[2])
        es = [jnp.exp(l - mx) for l in lses]
        den = es[0] + es[1] + es[2]
        acc = None
        for e, o_ref in zip(es, o_refs):
            o = o_ref[rs, cs]
            term = _bf16_round(jnp.broadcast_to(e / den, o.shape)) * _bf16_round(o)
            acc = term if acc is None else acc + term
        outs.append(acc)
    return jnp.concatenate(outs, axis=1)


def _swa_merge_kernel(o1, o2, o3, l1, l2, l3, out_ref, *, tm, hbs):
    ch = 128

    def body(c, carry):
        rs = pl.ds(pl.multiple_of(c * ch, ch), ch)
        out_ref[rs, :] = _merge_heads((o1, o2, o3), (l1, l2, l3), rs, hbs).astype(out_ref.dtype)
        return carry

    lax.fori_loop(0, tm // ch, body, 0)


def _swa_merge(os_, ls_):
    tm = 512
    m = os_[0].shape[0]
    hbs = tuple(SW_HEADS // (l.shape[1] // LANES) for l in ls_)
    row = lambda width: pl.BlockSpec((tm, width), lambda i: (i, 0))
    return pl.pallas_call(
        functools.partial(_swa_merge_kernel, tm=tm, hbs=hbs),
        grid=(m // tm,),
        in_specs=[row(SW_WIDTH)] * 3 + [row(l.shape[1]) for l in ls_],
        out_specs=row(SW_WIDTH),
        out_shape=jax.ShapeDtypeStruct((m, SW_WIDTH), BF16),
        compiler_params=_cparams(("arbitrary",)),
        name="swa_merge",
    )(*os_, *ls_)


def _swa_step_kernel(qkv_ref, c1_ref, c2_ref, c3_ref, o_ref):
    scale = SW_HEAD_DIM ** -0.5
    outs, lses = [], []
    for gi, c_ref in enumerate((c1_ref, c2_ref, c3_ref)):
        q = _bf16_round(qkv_ref[gi, 0])
        kn = _bf16_round(qkv_ref[gi, 1])
        vn = _bf16_round(qkv_ref[gi, 2])
        kc = _bf16_round(c_ref[:, 0])
        vc = _bf16_round(c_ref[:, 1])
        s = jnp.sum(kc * q[None], axis=-1, keepdims=True) * scale
        sn = jnp.sum(kn * q, axis=-1, keepdims=True) * scale
        m = jnp.maximum(jnp.max(s, axis=0), sn)
        lse = m + jnp.log(jnp.sum(jnp.exp(s - m[None]), axis=0) + jnp.exp(sn - m))
        shape = vc.shape
        p = _bf16_round(jnp.broadcast_to(jnp.exp(s - lse[None]), shape))
        pn = _bf16_round(jnp.broadcast_to(jnp.exp(sn - lse), shape[1:]))
        outs.append(jnp.sum(p * vc, axis=0) + pn * vn)
        lses.append(lse)
    mx = jnp.maximum(jnp.maximum(lses[0], lses[1]), lses[2])
    es = [jnp.exp(l - mx) for l in lses]
    den = es[0] + es[1] + es[2]
    acc = None
    for e, o in zip(es, outs):
        term = _bf16_round(jnp.broadcast_to(e / den, o.shape)) * _bf16_round(o)
        acc = term if acc is None else acc + term
    o_ref[...] = acc


def _swa_step(qkv, caches):
    b = qkv.shape[0]
    q5 = qkv.reshape(b, len(SW_GROUPS), 3, SW_HEADS, SW_HEAD_DIM)
    views, specs = [], []
    for c, (win, dil) in zip(caches, SW_GROUPS):
        views.append(c.reshape(b, win // dil, dil, 2, SW_HEADS, SW_HEAD_DIM))
        specs.append(pl.BlockSpec((None, win // dil, None, 2, SW_HEADS, SW_HEAD_DIM),
                                  lambda i: (i, 0, 0, 0, 0, 0)))
    o = pl.pallas_call(
        _swa_step_kernel,
        grid=(b,),
        in_specs=[pl.BlockSpec((None, len(SW_GROUPS), 3, SW_HEADS, SW_HEAD_DIM),
                               lambda i: (i, 0, 0, 0, 0))] + specs,
        out_specs=pl.BlockSpec((None, SW_HEADS, SW_HEAD_DIM), lambda i: (i, 0, 0)),
        out_shape=jax.ShapeDtypeStruct((b, SW_HEADS, SW_HEAD_DIM), F32),
        compiler_params=_cparams(("arbitrary",)),
        name="swa_step",
    )(q5, *views)
    return o.reshape(b, SW_WIDTH)


def _cache_shift_kernel(cur_ref, nxt_ref, knew_ref, vnew_ref, o_ref, *, wb):
    k = pl.program_id(1)
    last = pl.num_programs(1) - 1

    def row(i, carry):
        o_ref[i] = cur_ref[i + 1]
        return carry

    lax.fori_loop(0, wb - 1, row, 0, unroll=8)

    @pl.when(k < last)
    def _():
        o_ref[wb - 1] = nxt_ref[0]

    @pl.when(k == last)
    def _():
        o_ref[wb - 1, 0] = knew_ref[...]
        o_ref[wb - 1, 1] = vnew_ref[...]


def _cache_shift(qkv, caches):
    b = qkv.shape[0]
    q5 = qkv.reshape(b, len(SW_GROUPS), 3, SW_HEADS, SW_HEAD_DIM)
    outs = []
    for gi, c in enumerate(caches):
        w = c.shape[1]
        wb = min(w, 512)
        tail = (2, SW_HEADS, SW_HEAD_DIM)
        outs.append(pl.pallas_call(
            functools.partial(_cache_shift_kernel, wb=wb),
            grid=(b, w // wb),
            in_specs=[
                pl.BlockSpec((None, wb) + tail, lambda i, k: (i, k, 0, 0, 0)),
                pl.BlockSpec((None, 1) + tail, lambda i, k: (i, jnp.minimum((k + 1) * wb, w - 1), 0, 0, 0)),
                pl.BlockSpec((None, None, None, SW_HEADS, SW_HEAD_DIM), lambda i, k: (i, gi, 1, 0, 0)),
                pl.BlockSpec((None, None, None, SW_HEADS, SW_HEAD_DIM), lambda i, k: (i, gi, 2, 0, 0)),
            ],
            out_specs=pl.BlockSpec((None, wb) + tail, lambda i, k: (i, k, 0, 0, 0)),
            out_shape=jax.ShapeDtypeStruct(c.shape, F32),
            compiler_params=_cparams(("arbitrary", "arbitrary")),
            name=f"cache_shift_g{gi}",
        )(c, c, q5, q5))
    return outs


def _kv_window_kernel(k_ref, v_ref, o_ref, *, rows):
    for kv, src in enumerate((k_ref, v_ref)):
        for h in range(SW_HEADS):
            o_ref[pl.ds(kv * SW_HEADS + h, rows, stride=2 * SW_HEADS), :] = \
                src[:, h * SW_HEAD_DIM:(h + 1) * SW_HEAD_DIM]


def _kv_window(qkv, gi, win, batch, seq):
    wlen = min(win, seq)
    rows = 128
    nt = wlen // rows
    per_row = 2 * SW_HEADS
    src = lambda which: pl.BlockSpec(
        (rows, SW_WIDTH), lambda b, t: ((b * seq + seq - wlen) // rows + t, gi * 3 + which))
    out = pl.pallas_call(
        functools.partial(_kv_window_kernel, rows=rows),
        grid=(batch, nt),
        in_specs=[src(1), src(2)],
        out_specs=pl.BlockSpec((None, rows * per_row, SW_HEAD_DIM), lambda b, t: (b, t, 0)),
        out_shape=jax.ShapeDtypeStruct((batch, wlen * per_row, SW_HEAD_DIM), F32),
        compiler_params=_cparams(("arbitrary", "arbitrary")),
        name=f"kv_window_g{gi}",
    )(qkv, qkv)
    return out.reshape(batch, wlen, 2, SW_HEADS, SW_HEAD_DIM)


def _router_kernel(x_ref, gn_ref, sc_ref, sh_ref, wr_ref, br_ref, *rest, tm):
    h_ref, ids_ref, gates_ref, hf_ref = rest[-4:]
    _norm_mod_into(x_ref, gn_ref, sc_ref, sh_ref, [hf_ref], tm)
    for s in range(ROW_TILES):
        h_ref[pl.ds(s, tm, stride=ROW_TILES), :] = hf_ref[:, s * LANES:(s + 1) * LANES]
    logits = jnp.dot(hf_ref[...].astype(BF16), wr_ref[...].astype(BF16),
                     preferred_element_type=F32) + br_ref[...]
    lane = lax.broadcasted_iota(jnp.int32, logits.shape, 1)
    big = jnp.int32(1 << 20)
    is_g = (lane >= MOE_EXPERTS) & (lane < MOE_EXPERTS + MOE_GROUPS)
    glog = jnp.where(is_g, logits, -jnp.inf)
    gmax = jnp.max(glog, axis=-1, keepdims=True)
    gsel = jnp.min(jnp.where(glog == gmax, lane - MOE_EXPERTS, big), axis=-1, keepdims=True)
    gsum = jnp.sum(jnp.where(is_g, jnp.exp(glog - gmax), 0.0), axis=-1, keepdims=True)
    pg = 1.0 / gsum
    in_grp = (lane < MOE_EXPERTS) & ((lane // MOE_PER_GROUP) == gsel)
    el = jnp.where(in_grp, logits, -jnp.inf)
    v1 = jnp.max(el, axis=-1, keepdims=True)
    i1 = jnp.min(jnp.where(el == v1, lane, big), axis=-1, keepdims=True)
    el2 = jnp.where(lane == i1, -jnp.inf, el)
    v2 = jnp.max(el2, axis=-1, keepdims=True)
    i2 = jnp.min(jnp.where(el2 == v2, lane, big), axis=-1, keepdims=True)
    e2 = jnp.exp(v2 - v1)
    den = 1.0 + e2
    ids_ref[...] = jnp.where(lane == 0, i1, jnp.where(lane == 1, i2, 0))
    gates_ref[...] = jnp.where(lane == 0, pg * (1.0 / den), jnp.where(lane == 1, pg * (e2 / den), 0.0))


def _router(x, gn, mod, wr, br, hcat, n_tok, *, tm, rows_per_batch, row0):
    m = x.shape[0]
    blk0 = row0 // tm
    bpb = max(rows_per_batch // tm, 1)
    nb = m // tm
    n_steps = nb + (1 if hcat is None and n_tok > m else 0)
    cl = lambda i: jnp.minimum(i, nb - 1)
    msp = lambda col: pl.BlockSpec((None, mod.shape[1], D_MODEL), lambda i: (cl(i) // bpb, 0, col))
    in_specs = [
        pl.BlockSpec((tm, D_MODEL), lambda i: (cl(i), 0)),
        pl.BlockSpec((1, D_MODEL), lambda i: (0, 0)),
        msp(4), msp(3),
        pl.BlockSpec((D_MODEL, LANES), lambda i: (0, 0)),
        pl.BlockSpec((1, LANES), lambda i: (0, 0)),
    ]
    args = [x, gn.reshape(1, D_MODEL), mod, mod, wr, br]
    aliases = {}
    if hcat is not None:
        in_specs.append(pl.BlockSpec(memory_space=pl.ANY))
        args.append(hcat)
        aliases = {len(args) - 1: 0}
    return pl.pallas_call(
        functools.partial(_router_kernel, tm=tm),
        grid=(n_steps,),
        in_specs=in_specs,
        out_specs=[pl.BlockSpec((tm * ROW_TILES, LANES), lambda i: (blk0 + i, 0)),
                   pl.BlockSpec((tm, LANES), lambda i: (cl(i), 0)),
                   pl.BlockSpec((tm, LANES), lambda i: (cl(i), 0))],
        out_shape=[jax.ShapeDtypeStruct((n_tok * ROW_TILES, LANES), F32),
                   jax.ShapeDtypeStruct((m, LANES), jnp.int32),
                   jax.ShapeDtypeStruct((m, LANES), F32)],
        scratch_shapes=[pltpu.VMEM((tm, D_MODEL), F32)],
        input_output_aliases=aliases,
        compiler_params=_cparams(("arbitrary",)),
        name="moe_router",
    )(*args)


def _experts_kernel(tok_ref, run_ref, ng_ref, qlo_ref, qhi_ref, ce_ref, ck_ref, cs_ref, meta_ref,
                    h_hbm, wgu_hbm, wd_hbm, y_ref,
                    rows_ref, stg_ref, wgu_ref, wd_ref, row_sem, w_sem, *, layer):
    b = pl.program_id(0)
    n_used, q_total = meta_ref[0], meta_ref[1]
    rt = ROW_TILES
    grp = DMA_UNROLL * rt

    def issue_rows(blk, slot):
        def body(g, carry):
            for u in range(DMA_UNROLL):
                r = g * DMA_UNROLL + u
                tok = tok_ref[blk * MOE_ROWS + r]
                pltpu.make_async_copy(h_hbm.at[pl.ds(pl.multiple_of(tok * rt, rt), rt)],
                                      rows_ref.at[slot, pl.ds(pl.multiple_of(r * rt, rt), rt)],
                                      row_sem.at[slot]).start()
            return carry

        lax.fori_loop(0, ng_ref[blk], body, 0)

    def wait_rows(blk, slot):
        def body(g, carry):
            pltpu.make_async_copy(h_hbm.at[pl.ds(0, grp)], rows_ref.at[slot, pl.ds(0, grp)], row_sem.at[slot]).wait()
            return carry

        lax.fori_loop(0, ng_ref[blk], body, 0)

    def chunk_start(q):
        e, k, st = ce_ref[q], ck_ref[q], q % W_STAGES
        row0 = lambda kk: pl.ds(pl.multiple_of(kk * W_CHUNK, W_CHUNK), W_CHUNK)

        @pl.when(k < GU_CHUNKS)
        def _():
            pltpu.make_async_copy(wgu_hbm.at[layer, e, row0(k)], stg_ref.at[st], w_sem.at[st]).start(priority=1)

        @pl.when(k >= GU_CHUNKS)
        def _():
            pltpu.make_async_copy(wd_hbm.at[layer, e, row0(k - GU_CHUNKS)], stg_ref.at[st],
                                  w_sem.at[st]).start(priority=1)

    def chunk_process(q):
        k, slot, st = ck_ref[q], cs_ref[q], q % W_STAGES
        pltpu.make_async_copy(wgu_hbm.at[layer, 0, pl.ds(0, W_CHUNK)], stg_ref.at[st], w_sem.at[st]).wait()
        val = stg_ref[st].astype(BF16)

        @pl.when(k < GU_CHUNKS)
        def _():
            wgu_ref[slot, pl.ds(pl.multiple_of(k * W_CHUNK, W_CHUNK), W_CHUNK), :] = val

        @pl.when(k >= GU_CHUNKS)
        def _():
            wd_ref[slot, pl.ds(pl.multiple_of((k - GU_CHUNKS) * W_CHUNK, W_CHUNK), W_CHUNK), :] = val

        @pl.when(q + W_STAGES < q_total)
        def _():
            chunk_start(q + W_STAGES)

    def process_range(lo, hi):
        def body(q, carry):
            chunk_process(q)
            return carry

        lax.fori_loop(lo, hi, body, 0)

    @pl.when(b == 0)
    def _():
        rows_ref[...] = jnp.zeros_like(rows_ref)
        issue_rows(0, 0)
        for q in range(W_STAGES):
            chunk_start(q)
        process_range(0, RUN_CHUNKS)

    @pl.when(b + 1 < n_used)
    def _():
        issue_rows(b + 1, (b + 1) % 2)

    @pl.when(b < n_used)
    def _():
        rslot = b % 2
        wait_rows(b, rslot)
        x = jnp.concatenate(
            [rows_ref[rslot, pl.ds(s, MOE_ROWS, stride=rt), :].astype(BF16) for s in range(rt)], axis=1)
        wslot = run_ref[b] % 2
        for sl in range(2):
            @pl.when(wslot == sl)
            def _():
                au = jnp.dot(x, wgu_ref[sl], preferred_element_type=F32)
                hmid = (_silu(au[:, :MOE_FF]) * au[:, MOE_FF:]).astype(BF16)
                y = jnp.dot(hmid, wd_ref[sl], preferred_element_type=F32)
                for s in range(rt):
                    y_ref[pl.ds(s, MOE_ROWS, stride=rt), :] = y[:, s * LANES:(s + 1) * LANES]

        process_range(qlo_ref[b], qhi_ref[b])

    @pl.when(b >= n_used)
    def _():
        y_ref[...] = jnp.zeros_like(y_ref)


def _moe_experts(hcat, w_gu, w_down, layer, row_tok, blk_run, blk_groups, qlo, qhi, chunk_e, chunk_k, chunk_s, meta,
                 n_blocks):
    any_spec = pl.BlockSpec(memory_space=pl.ANY)
    return pl.pallas_call(
        functools.partial(_experts_kernel, layer=layer),
        grid_spec=pltpu.PrefetchScalarGridSpec(
            num_scalar_prefetch=9,
            grid=(n_blocks,),
            in_specs=[any_spec, any_spec, any_spec],
            out_specs=pl.BlockSpec((MOE_ROWS * ROW_TILES, LANES), lambda b, *_: (b, 0)),
            scratch_shapes=[pltpu.VMEM((2, MOE_ROWS * ROW_TILES, LANES), F32),
                            pltpu.VMEM((W_STAGES, W_CHUNK, D_MODEL), F32),
                            pltpu.VMEM((2, D_MODEL, 2 * MOE_FF), BF16),
                            pltpu.VMEM((2, MOE_FF, D_MODEL), BF16),
                            pltpu.SemaphoreType.DMA((2,)),
                            pltpu.SemaphoreType.DMA((W_STAGES,))],
        ),
        out_shape=jax.ShapeDtypeStruct((n_blocks * MOE_ROWS * ROW_TILES, LANES), F32),
        compiler_params=pltpu.CompilerParams(dimension_semantics=("arbitrary",), vmem_limit_bytes=VMEM_LIMIT,
                                             disable_bounds_checks=True),
        name="moe_experts",
    )(row_tok, blk_run, blk_groups, qlo, qhi, chunk_e, chunk_k, chunk_s, meta, hcat, w_gu, w_down)


def _combine_kernel(dest_ref, y_hbm, x_ref, g_ref, gates_ref, o_ref, y0_ref, y1_ref, sem, *, tm, tok0):
    i = pl.program_id(0)
    rt = ROW_TILES

    def issue(tile, slot):
        def body(g, carry):
            for u in range(DMA_UNROLL // 2):
                r = g * (DMA_UNROLL // 2) + u
                a = (tok0 + tile * tm + r) * MOE_TOP_K
                for j, y_ref in enumerate((y0_ref, y1_ref)):
                    pltpu.make_async_copy(y_hbm.at[pl.ds(pl.multiple_of(dest_ref[a + j] * rt, rt), rt)],
                                          y_ref.at[slot, pl.ds(pl.multiple_of(r * rt, rt), rt)],
                                          sem.at[slot]).start(priority=j)
            return carry

        lax.fori_loop(0, tm // (DMA_UNROLL // 2), body, 0)

    @pl.when(i == 0)
    def _():
        issue(0, 0)

    @pl.when(i + 1 < pl.num_programs(0))
    def _():
        issue(i + 1, (i + 1) % 2)

    slot = i % 2
    for y_ref in (y0_ref, y1_ref):
        pltpu.make_async_copy(y_hbm.at[pl.ds(0, tm * rt)], y_ref.at[slot], sem.at[slot]).wait()
    gt = gates_ref[...]
    g0 = jnp.broadcast_to(gt[:, 0:1], (tm, LANES))
    g1 = jnp.broadcast_to(gt[:, 1:2], (tm, LANES))
    for s in range(rt):
        cs = slice(s * LANES, (s + 1) * LANES)
        rows = pl.ds(s, tm, stride=rt)
        y = y0_ref[slot, rows, :] * g0 + y1_ref[slot, rows, :] * g1
        o_ref[:, cs] = x_ref[:, cs] + g_ref[:, cs] * y


def _moe_combine(y, dest, x, mod, gates, *, tm, rows_per_batch, tok0):
    m = x.shape[0]
    bpb = max(rows_per_batch // tm, 1)
    return pl.pallas_call(
        functools.partial(_combine_kernel, tm=tm, tok0=tok0),
        grid_spec=pltpu.PrefetchScalarGridSpec(
            num_scalar_prefetch=1,
            grid=(m // tm,),
            in_specs=[pl.BlockSpec(memory_space=pl.ANY),
                      pl.BlockSpec((tm, D_MODEL), lambda i, d: (i, 0)),
                      pl.BlockSpec((None, mod.shape[1], D_MODEL), lambda i, d: (i // bpb, 0, 5)),
                      pl.BlockSpec((tm, LANES), lambda i, d: (i, 0))],
            out_specs=pl.BlockSpec((tm, D_MODEL), lambda i, d: (i, 0)),
            scratch_shapes=[pltpu.VMEM((2, tm * ROW_TILES, LANES), F32)] * 2
            + [pltpu.SemaphoreType.DMA((2,))],
        ),
        out_shape=jax.ShapeDtypeStruct((m, D_MODEL), F32),
        compiler_params=pltpu.CompilerParams(dimension_semantics=("arbitrary",), vmem_limit_bytes=VMEM_LIMIT,
                                             disable_bounds_checks=True),
        name="moe_combine",
    )(dest, y, x, mod, gates)


def _moe_layer(xp, xs, mod_p, mod_s, gn, w_group, b_group, w_router, b_router, w_gu, w_down, layer, seq):
    n_p, n_s = xp.shape[0], xs.shape[0]
    n_tok = n_p + n_s
    pad = LANES - MOE_EXPERTS - MOE_GROUPS
    wr = jnp.concatenate([w_router, w_group, jnp.zeros((D_MODEL, pad), F32)], axis=1)
    br = jnp.concatenate([b_router, b_group, jnp.zeros((pad,), F32)]).reshape(1, LANES)
    hcat, ids_p, gates_p = _router(xp, gn, mod_p, wr, br, None, n_tok, tm=256, rows_per_batch=seq, row0=0)
    hcat, ids_s, gates_s = _router(xs, gn, mod_s, wr, br, hcat, n_tok, tm=n_s, rows_per_batch=n_s, row0=n_p)

    expert = jnp.concatenate([ids_p[:, :MOE_TOP_K], ids_s[:, :MOE_TOP_K]], axis=0).reshape(-1)
    n_assign = n_tok * MOE_TOP_K
    n_blocks = -(-n_assign // MOE_ROWS) + MOE_EXPERTS
    onehot = (expert[:, None] == jnp.arange(MOE_EXPERTS, dtype=jnp.int32)[None, :]).astype(jnp.int32)
    csum = jnp.cumsum(onehot, axis=0)
    rank = jnp.sum(csum * onehot, axis=1) - 1
    counts = csum[-1]
    padded = (counts + MOE_ROWS - 1) // MOE_ROWS * MOE_ROWS
    pend = jnp.cumsum(padded)
    pstart = pend - padded
    dest = (pstart[expert] + rank).astype(jnp.int32)
    tok = jnp.arange(n_assign, dtype=jnp.int32) // MOE_TOP_K
    row_tok = jnp.zeros((n_blocks * MOE_ROWS,), jnp.int32).at[dest].set(tok, unique_indices=True)
    blk_ids = jnp.arange(n_blocks, dtype=jnp.int32)
    blk_expert = jnp.minimum(jnp.searchsorted(pend, blk_ids * MOE_ROWS, side="right"),
                             MOE_EXPERTS - 1).astype(jnp.int32)
    n_used = (pend[-1:] // MOE_ROWS).astype(jnp.int32)
    nonempty = counts > 0
    n_runs = jnp.sum(nonempty.astype(jnp.int32))
    run_of_expert = jnp.cumsum(nonempty.astype(jnp.int32)) - 1
    run_expert = jnp.argsort(jnp.where(nonempty, 0, 1), stable=True).astype(jnp.int32)
    blk_run = run_of_expert[blk_expert].astype(jnp.int32)
    j_in_run = blk_ids - pstart[blk_expert] // MOE_ROWS
    n_in_run = jnp.maximum(padded[blk_expert] // MOE_ROWS, 1)
    q_total = n_runs * RUN_CHUNKS
    streams = (blk_ids < n_used[0]) & (blk_run + 1 < n_runs)
    base = (blk_run + 1) * RUN_CHUNKS
    qlo = jnp.where(streams, base + (RUN_CHUNKS * j_in_run) // n_in_run, q_total).astype(jnp.int32)
    qhi = jnp.where(streams, base + (RUN_CHUNKS * (j_in_run + 1)) // n_in_run, q_total).astype(jnp.int32)
    chunk_ids = jnp.arange(MOE_EXPERTS * RUN_CHUNKS, dtype=jnp.int32)
    chunk_e = run_expert[chunk_ids // RUN_CHUNKS]
    chunk_k = chunk_ids % RUN_CHUNKS
    chunk_s = (chunk_ids // RUN_CHUNKS) % 2
    meta = jnp.concatenate([n_used, q_total[None].astype(jnp.int32)])
    valid_rows = jnp.clip(counts[blk_expert] - j_in_run * MOE_ROWS, 0, MOE_ROWS)
    blk_groups = jnp.where(blk_ids < n_used[0], (valid_rows + DMA_UNROLL - 1) // DMA_UNROLL, 0).astype(jnp.int32)

    y = _moe_experts(hcat, w_gu, w_down, layer, row_tok, blk_run, blk_groups, qlo, qhi, chunk_e, chunk_k, chunk_s,
                     meta, n_blocks)
    xp_new = _moe_combine(y, dest, xp, mod_p, gates_p, tm=256, rows_per_batch=seq, tok0=0)
    xs_new = _moe_combine(y, dest, xs, mod_s, gates_s, tm=n_s, rows_per_batch=n_s, tok0=n_p)
    return xp_new, xs_new


def _final_norm_kernel(x_ref, g_ref, o_ref):
    x = x_ref[...]
    inv = lax.rsqrt(jnp.mean(x * x, axis=-1, keepdims=True) + EPS)
    o_ref[...] = (x * inv) * g_ref[...]


def _final_norm(x, g, tm):
    m = x.shape[0]
    return pl.pallas_call(
        _final_norm_kernel,
        grid=(m // tm,),
        in_specs=[pl.BlockSpec((tm, D_MODEL), lambda i: (i, 0)), pl.BlockSpec((1, D_MODEL), lambda i: (0, 0))],
        out_specs=pl.BlockSpec((tm, D_MODEL), lambda i: (i, 0)),
        out_shape=jax.ShapeDtypeStruct((m, D_MODEL), F32),
        compiler_params=_cparams(("arbitrary",)),
        name="final_norm",
    )(x, g.reshape(1, D_MODEL))


def _rope_tables(pos):
    half = SW_HEAD_DIM // 2
    inv_freq = ROPE_THETA ** (-jnp.arange(half, dtype=F32) / half)
    ang = pos.astype(F32)[:, None] * inv_freq[None, :]
    cos, sin = jnp.cos(ang), jnp.sin(ang)
    return jnp.concatenate([cos, cos], axis=1), jnp.concatenate([-sin, sin], axis=1)


def kernel(x_prompt, x_sample, state_hgrn, state_conv, cache_swa_g1, cache_swa_g2, cache_swa_g3, c_prompt, c_sample, ada_w, ada_b, norm_mix, norm_ffn, norm_final, hg_w_in, hg_w_out, hg_norm, hg_lower, cv_w_in, cv_w_conv, cv_w_out, sw_w_in, sw_w_out, moe_w_group, moe_b_group, moe_w_router, moe_b_router, moe_w_gu, moe_w_down):
    bp, seq, d = x_prompt.shape
    bs = x_sample.shape[0]
    n_p = bp * seq
    xp = x_prompt.reshape(n_p, d)
    xs = x_sample.reshape(bs, d)

    mod_all = _ada_mod(jnp.concatenate([c_prompt, c_sample], axis=0), ada_w, ada_b)
    sm = jax.nn.softmax(hg_lower.astype(F32), axis=0)
    lower = jnp.cumsum(sm, axis=0) - sm[0]
    rope_p = _rope_tables(jnp.arange(seq))
    rope_s = _rope_tables(jnp.full((bs,), PAST_LEN, jnp.int32))

    tm_p, tn_p = 1024, 512
    hg_p, hg_s, cv_p, cv_s = [], [], [], []
    sw_p, sw_s = None, None
    for i in range(DEPTH):
        kind, j = i % N_MIXERS, i // N_MIXERS
        mod_p = mod_all[i, :bp].reshape(bp, 1, 6 * d)
        mod_s = mod_all[i, bp:].reshape(1, bs, 6 * d)
        pp = dict(tm=tm_p, tn=tn_p)
        pp_in = dict(tm=tm_p, tn=1024)
        ps = dict(tm=bs, tn=1024)
        hp = _norm_mod(xp, norm_mix[i], mod_p, 0, 1, tm=512, rows_per_batch=seq)
        hs = _norm_mod(xs, norm_mix[i], mod_s, 0, 1, tm=bs, rows_per_batch=bs)
        if kind == 0:
            proj_p = _mm(hp, hg_w_in, j, **pp_in)
            proj_s = _mm(hs, hg_w_in, j, **ps)
            lhs_p, st_p = _gla_prompt(proj_p, lower[j], hg_norm[j], bp, seq)
            lhs_s, st_s = _gla_step(proj_s, lower[j], hg_norm[j], state_hgrn, j)
            hg_p.append(st_p)
            hg_s.append(st_s)
            w_out = hg_w_out
        elif kind == 1:
            proj_p = _mm(hp, cv_w_in, j, **pp_in)
            proj_s = _mm(hs, cv_w_in, j, **ps)
            lhs_p, tail_p = _conv_prompt(proj_p, cv_w_conv[j], bp, seq)
            lhs_s, tail_s = _conv_step(proj_s, cv_w_conv[j], state_conv[j])
            cv_p.append(tail_p)
            cv_s.append(tail_s)
            w_out = cv_w_out
        else:
            qkv_p = _mm(hp, sw_w_in, j, tm=512, tn=SW_WIDTH, rope=rope_p)
            qkv_s = _mm(hs, sw_w_in, j, rope=rope_s, **ps)
            os_, ls_ = [], []
            for gi, (win, dil) in enumerate(SW_GROUPS):
                o_g, l_g = _swa_prompt_group(qkv_p, gi, dil, bp, seq)
                os_.append(o_g)
                ls_.append(l_g)
            lhs_p = _swa_merge(os_, ls_)
            caches = (cache_swa_g1[j], cache_swa_g2[j], cache_swa_g3[j])
            lhs_s = _swa_step(qkv_s, caches)
            w_out = sw_w_out
            sw_p = [_kv_window(qkv_p, gi, win, bp, seq)[None] for gi, (win, _) in enumerate(SW_GROUPS)]
            sw_s = [c[None] for c in _cache_shift(qkv_s, caches)]
        xp = _mm(lhs_p, w_out, j, resgate=(xp, mod_p, 2, seq), **pp)
        xs = _mm(lhs_s, w_out, j, resgate=(xs, mod_s, 2, bs), **ps)
        xp, xs = _moe_layer(xp, xs, mod_p, mod_s, norm_ffn[i], moe_w_group[i], moe_b_group[i],
                            moe_w_router[i], moe_b_router[i], moe_w_gu, moe_w_down, i, seq)

    y_p = _final_norm(xp, norm_final, 512).reshape(bp, seq, d)
    y_s = _final_norm(xs, norm_final, bs).reshape(bs, 1, d)
    return (y_p, y_s, jnp.stack(hg_p), jnp.stack(hg_s), jnp.stack(cv_p), jnp.stack(cv_s),
            sw_p[0], sw_s[0], sw_p[1], sw_s[1], sw_p[2], sw_s[2])
```

```python
import functools

import jax
import jax.numpy as jnp
from jax import lax
from jax.experimental import pallas as pl
from jax.experimental.pallas import tpu as pltpu

F32 = jnp.float32
BF16 = jnp.bfloat16

D_MODEL = 2048
DEPTH = 4
N_MIXERS = 3
EPS = 1e-6
MASK_VALUE = -1e30
F_FLOOR = 1e-30
HG_DK = 128
HG_HEADS = D_MODEL // HG_DK
HG_CHUNK = 64
HG_SUB = 16
CONV_WIDTH = 3
SW_GROUPS = ((128, 1), (512, 4), (2048, 16))
SW_HEADS = 8
SW_HEAD_DIM = 128
SW_WIDTH = SW_HEADS * SW_HEAD_DIM
SW_BAND = 128
ROPE_THETA = 10000.0
MOE_GROUPS = 4
MOE_PER_GROUP = 8
MOE_EXPERTS = MOE_GROUPS * MOE_PER_GROUP
MOE_TOP_K = 2
MOE_FF = 1024
PAST_LEN = 16384

LANES = 128
SUBLANES = 8
VMEM_LIMIT = 52 * 1024 * 1024
MOE_ROWS = 256
ROW_TILES = D_MODEL // LANES
DMA_UNROLL = 8
W_CHUNK = 256
GU_CHUNKS = D_MODEL // W_CHUNK
RUN_CHUNKS = GU_CHUNKS + MOE_FF // W_CHUNK
W_STAGES = 4


def _cparams(sem):
    return pltpu.CompilerParams(dimension_semantics=sem, vmem_limit_bytes=VMEM_LIMIT)


def _sigmoid(x):
    return 1.0 / (1.0 + jnp.exp(-x))


def _silu(x):
    return x * _sigmoid(x)


def _bf16_round(x):
    return x.astype(BF16).astype(F32)


def _ada_kernel(c_ref, w_ref, b_ref, o_ref):
    c = c_ref[...]
    cond = _silu(c).astype(BF16)
    o_ref[...] = jnp.dot(cond, w_ref[...].astype(BF16), preferred_element_type=F32) + b_ref[...]


def _ada_mod(c_all, ada_w, ada_b):
    rows = c_all.shape[0]
    n = ada_w.shape[-1]
    tn = 1024
    return pl.pallas_call(
        _ada_kernel,
        grid=(DEPTH, n // tn),
        in_specs=[
            pl.BlockSpec((rows, D_MODEL), lambda l, j: (0, 0)),
            pl.BlockSpec((None, D_MODEL, tn), lambda l, j: (l, 0, j)),
            pl.BlockSpec((None, 1, tn), lambda l, j: (l, 0, j)),
        ],
        out_specs=pl.BlockSpec((None, rows, tn), lambda l, j: (l, 0, j)),
        out_shape=jax.ShapeDtypeStruct((DEPTH, rows, n), F32),
        compiler_params=_cparams(("arbitrary", "arbitrary")),
        name="ada_mod",
    )(c_all, ada_w, ada_b.reshape(DEPTH, 1, n))


def _norm_mod_rows(x, gn, sc, sh):
    inv = lax.rsqrt(jnp.mean(x * x, axis=-1, keepdims=True) + EPS)
    return (x * inv) * gn * (1.0 + sc) + sh


def _norm_mod_into(x_ref, gn_ref, sc_ref, sh_ref, dst_refs, tm):
    ch = min(tm, 128)
    per_row = sc_ref.shape[0] != 1

    def body(c, carry):
        rs = pl.ds(pl.multiple_of(c * ch, ch), ch)
        sc = sc_ref[rs, :] if per_row else sc_ref[...]
        sh = sh_ref[rs, :] if per_row else sh_ref[...]
        h = _norm_mod_rows(x_ref[rs, :], gn_ref[...], sc, sh)
        for d in dst_refs:
            d[rs, :] = h.astype(d.dtype)
        return carry

    lax.fori_loop(0, tm // ch, body, 0)


def _norm_mod_kernel(x_ref, gn_ref, sc_ref, sh_ref, o_ref, *, tm):
    _norm_mod_into(x_ref, gn_ref, sc_ref, sh_ref, [o_ref], tm)


def _norm_mod(x, gn, mod, shift_col, scale_col, *, tm, rows_per_batch):
    m = x.shape[0]
    bpb = max(rows_per_batch // tm, 1)
    msp = lambda col: pl.BlockSpec((None, mod.shape[1], D_MODEL), lambda i: (i // bpb, 0, col))
    return pl.pallas_call(
        functools.partial(_norm_mod_kernel, tm=tm),
        grid=(m // tm,),
        in_specs=[pl.BlockSpec((tm, D_MODEL), lambda i: (i, 0)),
                  pl.BlockSpec((1, D_MODEL), lambda i: (0, 0)),
                  msp(scale_col), msp(shift_col)],
        out_specs=pl.BlockSpec((tm, D_MODEL), lambda i: (i, 0)),
        out_shape=jax.ShapeDtypeStruct((m, D_MODEL), BF16),
        compiler_params=_cparams(("arbitrary",)),
        name="norm_mod",
    )(x, gn.reshape(1, D_MODEL), mod, mod)


def _rope_tile(acc, cos, sin_signed):
    outs = []
    for h in range(acc.shape[1] // SW_HEAD_DIM):
        xh = acc[:, h * SW_HEAD_DIM:(h + 1) * SW_HEAD_DIM]
        outs.append(xh * cos + pltpu.roll(xh, SW_HEAD_DIM // 2, 1) * sin_signed)
    return jnp.concatenate(outs, axis=1)


def _mm_kernel(x_ref, w_ref, *rest, epilogue):
    o_ref, wb_ref = rest[-2:]
    j, i = pl.program_id(0), pl.program_id(1)

    @pl.when(i == 0)
    def _():
        wb_ref[...] = w_ref[...].astype(BF16)

    acc = jnp.dot(x_ref[...].astype(BF16), wb_ref[...], preferred_element_type=F32)
    if epilogue == "rope":
        cos_ref, sin_ref = rest[:2]
        is_qk = (j % 3) < 2
        cos = jnp.where(is_qk, cos_ref[...], 1.0)
        sin = jnp.where(is_qk, sin_ref[...], 0.0)
        o_ref[...] = _rope_tile(acc, cos, sin)
    elif epilogue == "resgate":
        res_ref, g_ref = rest[:2]
        o_ref[...] = res_ref[...] + g_ref[...] * acc
    else:
        o_ref[...] = acc


def _mm(x, w, wl, *, tm, tn, rope=None, resgate=None):
    m, k = x.shape
    n = w.shape[2]
    in_specs = [pl.BlockSpec((tm, k), lambda j, i: (i, 0)),
                pl.BlockSpec((None, k, tn), lambda j, i: (wl, 0, j))]
    args = [x, w]
    epilogue = "none"
    if rope is not None:
        epilogue = "rope"
        nblk = rope[0].shape[0] // tm
        in_specs += [pl.BlockSpec((tm, SW_HEAD_DIM), lambda j, i: (i % nblk, 0))] * 2
        args += list(rope)
    if resgate is not None:
        epilogue = "resgate"
        res, mod, gate_col, rows_per_batch = resgate
        bpb = max(rows_per_batch // tm, 1)
        per = D_MODEL // tn
        in_specs += [pl.BlockSpec((tm, tn), lambda j, i: (i, j)),
                     pl.BlockSpec((None, mod.shape[1], tn), lambda j, i: (i // bpb, 0, gate_col * per + j))]
        args += [res, mod]
    return pl.pallas_call(
        functools.partial(_mm_kernel, epilogue=epilogue),
        grid=(n // tn, m // tm),
        in_specs=in_specs,
        out_specs=pl.BlockSpec((tm, tn), lambda j, i: (i, j)),
        out_shape=jax.ShapeDtypeStruct((m, n), F32),
        scratch_shapes=[pltpu.VMEM((k, tn), BF16)],
        compiler_params=_cparams(("arbitrary", "arbitrary")),
        name="mm_" + epilogue,
    )(*args)


def _hgrn_gates(qp, fp, lb):
    q = _silu(qp)
    f = lb + (1.0 - lb) * _sigmoid(fp)
    logf = jnp.log(jnp.maximum(f, F_FLOOR))
    k = (1.0 - lb) * _sigmoid(-fp)
    return q, logf, k


def _head_norm_gate(o, gn, gp):
    inv = lax.rsqrt(jnp.mean(o * o, axis=-1, keepdims=True) + EPS)
    return (o * inv) * gn * _silu(gp)


def _gla_kernel(q_ref, f_ref, i_ref, g_ref, lb_ref, gn_ref, o_ref, s_out_ref, st_ref, *, n_chunks, n_hb):
    c_len, n_sub = HG_CHUNK, HG_CHUNK // HG_SUB
    t = pl.program_id(2)

    @pl.when(t == 0)
    def _():
        st_ref[...] = jnp.zeros_like(st_ref)

    row = lax.broadcasted_iota(jnp.int32, (c_len, HG_DK), 0)
    r4 = lax.broadcasted_iota(jnp.int32, (c_len, n_sub * c_len), 0)
    c4 = lax.broadcasted_iota(jnp.int32, (c_len, n_sub * c_len), 1)
    mask4 = ((r4 // HG_SUB) == (c4 // c_len)) & ((c4 % c_len) <= r4)
    nt = (((1,), (1,)), ((), ()))
    tn = (((0,), (0,)), ((), ()))

    for c, hh in [(c, hh) for c in range(n_chunks) for hh in range(n_hb)]:
        rs = pl.ds(c * c_len, c_len)
        cs = slice(hh * HG_DK, (hh + 1) * HG_DK)
        lb, gn = lb_ref[:, cs], gn_ref[:, cs]
        q, logf, k = _hgrn_gates(q_ref[rs, cs], f_ref[rs, cs], lb)
        v = i_ref[rs, cs]
        g = logf
        sh = 1
        while sh < c_len:
            g = g + jnp.where(row >= sh, pltpu.roll(g, sh, 0), 0.0)
            sh *= 2
        refs = [g[HG_SUB * i + HG_SUB // 2 - 1:HG_SUB * i + HG_SUB // 2, :] for i in range(n_sub)]
        mrows = jnp.concatenate([jnp.broadcast_to(r, (HG_SUB, HG_DK)) for r in refs], axis=0)
        qh = (q * jnp.exp(g - mrows)).astype(BF16)
        khs = [k * jnp.exp(jnp.where(row < HG_SUB * (i + 1), refs[i] - g, -jnp.inf)) for i in range(n_sub)]
        kh = jnp.concatenate(khs, axis=0).astype(BF16)
        a4 = lax.dot_general(qh, kh, nt, preferred_element_type=F32)
        a4 = jnp.where(mask4, a4, 0.0).astype(BF16)
        vb = v.astype(BF16)
        v4 = jnp.concatenate([vb] * n_sub, axis=0)
        st = st_ref[hh]
        o = jnp.dot(a4, v4, preferred_element_type=F32)
        o = o + lax.dot_general((q * jnp.exp(g)).astype(BF16), st.astype(BF16), nt,
                                preferred_element_type=F32)
        glast = g[c_len - 1:c_len, :]
        kd = (k * jnp.exp(glast - g)).astype(BF16)
        st_ref[hh] = st * jnp.exp(glast) + lax.dot_general(vb, kd, tn, preferred_element_type=F32)
        o_ref[rs, cs] = _head_norm_gate(o, gn, g_ref[rs, cs]).astype(o_ref.dtype)

    @pl.when(t == pl.num_programs(2) - 1)
    def _():
        for hh in range(n_hb):
            s_out_ref[hh] = st_ref[hh].T


def _gla_prompt(proj, lower, g_norm, batch, seq):
    tb = 512
    n_hb = 2
    nt = seq // tb
    hblk = HG_HEADS // n_hb
    wcol = n_hb * HG_DK

    def col(off):
        return pl.BlockSpec((tb, wcol), lambda b, hh, t: (b * nt + t, off * hblk + hh))

    return pl.pallas_call(
        functools.partial(_gla_kernel, n_chunks=tb // HG_CHUNK, n_hb=n_hb),
        grid=(batch, hblk, nt),
        in_specs=[col(0), col(1), col(2), col(3),
                  pl.BlockSpec((1, wcol), lambda b, hh, t: (0, hh)),
                  pl.BlockSpec((1, wcol), lambda b, hh, t: (0, hh))],
        out_specs=[pl.BlockSpec((tb, wcol), lambda b, hh, t: (b * nt + t, hh)),
                   pl.BlockSpec((None, n_hb, HG_DK, HG_DK), lambda b, hh, t: (b, hh, 0, 0))],
        out_shape=[jax.ShapeDtypeStruct((batch * seq, D_MODEL), BF16),
                   jax.ShapeDtypeStruct((batch, HG_HEADS, HG_DK, HG_DK), F32)],
        scratch_shapes=[pltpu.VMEM((n_hb, HG_DK, HG_DK), F32)],
        compiler_params=_cparams(("arbitrary", "arbitrary", "arbitrary")),
        name="hgrn_prompt",
    )(proj, proj, proj, proj, lower.reshape(1, D_MODEL), g_norm.reshape(1, D_MODEL))


def _gla_step_kernel(qt_ref, ft_ref, i_ref, g_ref, lbt_ref, gn_ref, s_ref, o_ref, s_out_ref):
    for h in range(HG_HEADS):
        cs = slice(h * HG_DK, (h + 1) * HG_DK)
        q, logf, k = _hgrn_gates(qt_ref[:, h:h + 1], ft_ref[:, h:h + 1], lbt_ref[:, h:h + 1])
        dec = jnp.exp(logf)
        s_old = s_ref[h]
        v = i_ref[:, cs]
        s_out_ref[h] = dec * s_old + k * v
        qd = _bf16_round(jnp.broadcast_to(q * dec, s_old.shape))
        o = jnp.sum(_bf16_round(s_old) * qd, axis=0, keepdims=True) + jnp.sum(q * k, axis=0, keepdims=True) * v
        o_ref[:, cs] = _head_norm_gate(o, gn_ref[:, cs], g_ref[:, cs])


def _gla_step(proj, lower, g_norm, states, layer):
    b = proj.shape[0]
    h = HG_HEADS
    pq = proj[:, :D_MODEL].reshape(b, h, HG_DK).transpose(0, 2, 1)
    pf = proj[:, D_MODEL:2 * D_MODEL].reshape(b, h, HG_DK).transpose(0, 2, 1)
    p3 = proj.reshape(b, 1, 4 * D_MODEL)
    lbt = lower.reshape(h, HG_DK).T
    vec = lambda col: pl.BlockSpec((None, 1, D_MODEL), lambda i: (i, 0, col))
    tr = pl.BlockSpec((None, HG_DK, h), lambda i: (i, 0, 0))
    o, s_new = pl.pallas_call(
        _gla_step_kernel,
        grid=(b,),
        in_specs=[tr, tr, vec(2), vec(3),
                  pl.BlockSpec((HG_DK, h), lambda i: (0, 0)),
                  pl.BlockSpec((1, D_MODEL), lambda i: (0, 0)),
                  pl.BlockSpec((None, None, h, HG_DK, HG_DK), lambda i: (layer, i, 0, 0, 0))],
        out_specs=[pl.BlockSpec((None, 1, D_MODEL), lambda i: (i, 0, 0)),
                   pl.BlockSpec((None, h, HG_DK, HG_DK), lambda i: (i, 0, 0, 0))],
        out_shape=[jax.ShapeDtypeStruct((b, 1, D_MODEL), F32),
                   jax.ShapeDtypeStruct(states.shape[1:], F32)],
        compiler_params=_cparams(("arbitrary",)),
        name="hgrn_step",
    )(pq, pf, p3, p3, lbt, g_norm.reshape(1, D_MODEL), states)
    return o.reshape(b, D_MODEL), s_new


def _conv_kernel(bg_ref, cg_ref, u_ref, hc_ref, hu_ref, w_ref, o_ref, tail_ref, *, tm):
    t = pl.program_id(1)
    z = cg_ref[...] * u_ref[...]
    hz = hc_ref[...] * hu_ref[...]
    hz = jnp.where(t == 0, 0.0, hz)
    z1p, z2p = hz[SUBLANES - 1:SUBLANES, :], hz[SUBLANES - 2:SUBLANES - 1, :]
    row = lax.broadcasted_iota(jnp.int32, z.shape, 0)
    z1 = jnp.where(row == 0, z1p, pltpu.roll(z, 1, 0))
    z2 = jnp.where(row == 0, z2p, jnp.where(row == 1, z1p, pltpu.roll(z, 2, 0)))
    y = z2 * w_ref[0:1, :] + z1 * w_ref[1:2, :] + z * w_ref[2:3, :]
    o_ref[...] = (bg_ref[...] * y).astype(o_ref.dtype)
    tail_ref[...] = z[tm - (CONV_WIDTH - 1):, :]


def _conv_prompt(proj, w_conv, batch, seq):
    tm = 256
    nt = seq // tm
    hb = tm // SUBLANES
    blk = lambda col: pl.BlockSpec((tm, D_MODEL), lambda b, t: (b * nt + t, col))
    halo = lambda col: pl.BlockSpec(
        (SUBLANES, D_MODEL), lambda b, t: (jnp.maximum((b * nt + t) * hb - 1, 0), col))
    return pl.pallas_call(
        functools.partial(_conv_kernel, tm=tm),
        grid=(batch, nt),
        in_specs=[blk(0), blk(1), blk(2), halo(1), halo(2),
                  pl.BlockSpec((CONV_WIDTH, D_MODEL), lambda b, t: (0, 0))],
        out_specs=[pl.BlockSpec((tm, D_MODEL), lambda b, t: (b * nt + t, 0)),
                   pl.BlockSpec((None, CONV_WIDTH - 1, D_MODEL), lambda b, t: (b, 0, 0))],
        out_shape=[jax.ShapeDtypeStruct((batch * seq, D_MODEL), BF16),
                   jax.ShapeDtypeStruct((batch, CONV_WIDTH - 1, D_MODEL), F32)],
        compiler_params=_cparams(("arbitrary", "arbitrary")),
        name="conv_prompt",
    )(proj, proj, proj, proj, proj, w_conv)


def _conv_step_kernel(bg_ref, cg_ref, u_ref, buf_ref, w_ref, o_ref, nb_ref):
    z = cg_ref[...] * u_ref[...]
    buf = buf_ref[...]
    y = buf[:, 0:1, :] * w_ref[0:1, :] + buf[:, 1:2, :] * w_ref[1:2, :] + z * w_ref[2:3, :]
    o_ref[...] = bg_ref[...] * y
    nb_ref[:, 0:1, :] = buf[:, 1:2, :]
    nb_ref[:, 1:2, :] = z


def _conv_step(proj, w_conv, buf):
    b = proj.shape[0]
    p3 = proj.reshape(b, 1, 3 * D_MODEL)
    vec = lambda col: pl.BlockSpec((b, 1, D_MODEL), lambda i: (0, 0, col))
    o, nb = pl.pallas_call(
        _conv_step_kernel,
        grid=(1,),
        in_specs=[vec(0), vec(1), vec(2),
                  pl.BlockSpec((b, CONV_WIDTH - 1, D_MODEL), lambda i: (0, 0, 0)),
                  pl.BlockSpec((CONV_WIDTH, D_MODEL), lambda i: (0, 0))],
        out_specs=[pl.BlockSpec((b, 1, D_MODEL), lambda i: (0, 0, 0)),
                   pl.BlockSpec((b, CONV_WIDTH - 1, D_MODEL), lambda i: (0, 0, 0))],
        out_shape=[jax.ShapeDtypeStruct((b, 1, D_MODEL), F32),
                   jax.ShapeDtypeStruct(buf.shape, F32)],
        compiler_params=_cparams(("arbitrary",)),
        name="conv_step",
    )(p3, p3, p3, buf, w_conv)
    return o.reshape(b, D_MODEL), nb


def _swa_kernel(*refs, dil, n_units, n_heads, has_prev):
    if has_prev:
        qc_ref, kp_ref, kc_ref, vp_ref, vc_ref, o_ref, l_ref = refs
    else:
        qc_ref, kc_ref, vc_ref, o_ref, l_ref = refs
    blk = SW_BAND
    n = pl.program_id(1)
    ri = lax.broadcasted_iota(jnp.int32, (blk, blk), 0)
    ci = lax.broadcasted_iota(jnp.int32, (blk, blk), 1)
    m_cur = ci <= ri
    m_prev_tri = ci >= ri
    lane = lax.broadcasted_iota(jnp.int32, (blk, LANES), 1)
    scale = SW_HEAD_DIM ** -0.5
    nt = (((1,), (1,)), ((), ()))

    for u in range(n_units):
        if dil == 1:
            rows = pl.ds(u * blk, blk)
            prev_src = None
            if has_prev:
                prev_src = (kp_ref, vp_ref, pl.ds(0, blk), True) if u == 0 else \
                    (kc_ref, vc_ref, pl.ds((u - 1) * blk, blk), False)
        else:
            rows = pl.ds(u, blk, stride=dil)
            prev_src = (kp_ref, vp_ref, rows, True) if has_prev else None

        def head(h, lse_acc, rows=rows, prev_src=prev_src):
            cs = pl.ds(0, SW_HEAD_DIM) if n_heads == 1 else \
                pl.ds(pl.multiple_of(h * SW_HEAD_DIM, SW_HEAD_DIM), SW_HEAD_DIM)
            q = qc_ref[rows, cs].astype(BF16)
            s_c = lax.dot_general(q, kc_ref[rows, cs].astype(BF16), nt, preferred_element_type=F32) * scale
            s_c = jnp.where(m_cur, s_c, MASK_VALUE)
            m = jnp.max(s_c, axis=-1, keepdims=True)
            if prev_src is not None:
                kr, vr, prow, first_only = prev_src
                s_p = lax.dot_general(q, kr[prow, cs].astype(BF16), nt, preferred_element_type=F32) * scale
                m_prev = (m_prev_tri & (n > 0)) if first_only else m_prev_tri
                s_p = jnp.where(m_prev, s_p, MASK_VALUE)
                m = jnp.maximum(m, jnp.max(s_p, axis=-1, keepdims=True))
            p_c = jnp.exp(s_c - m)
            l = jnp.sum(p_c, axis=-1, keepdims=True)
            o = jnp.dot(p_c.astype(BF16), vc_ref[rows, cs].astype(BF16), preferred_element_type=F32)
            if prev_src is not None:
                p_p = jnp.exp(s_p - m)
                l = l + jnp.sum(p_p, axis=-1, keepdims=True)
                o = o + jnp.dot(p_p.astype(BF16), vr[prow, cs].astype(BF16), preferred_element_type=F32)
            o_ref[rows, cs] = o / l
            lse = m + jnp.log(l)
            return jnp.where(lane == h, lse, lse_acc)

        lse0 = jnp.zeros((blk, LANES), F32)
        l_ref[rows, :] = head(0, lse0) if n_heads == 1 else lax.fori_loop(0, n_heads, head, lse0, unroll=2)


def _swa_prompt_group(qkv, gi, dil, batch, seq):
    span = SW_BAND * dil
    if dil == 1:
        span, n_units = 512, 4
    else:
        n_units = dil
    nb = seq // span
    has_prev = nb > 1
    hb = SW_HEADS if dil == 1 else 1
    n_hblk = SW_HEADS // hb
    wcol = hb * SW_HEAD_DIM
    per = SW_WIDTH // wcol
    pspan = SW_BAND if dil == 1 else span
    pmul = span // pspan

    def cur(which):
        return pl.BlockSpec((span, wcol), lambda b, n, hh: (b * nb + n, (gi * 3 + which) * per + hh))

    def prev(which):
        return pl.BlockSpec(
            (pspan, wcol),
            lambda b, n, hh: (jnp.maximum((b * nb + n) * pmul - 1, 0), (gi * 3 + which) * per + hh))

    if has_prev:
        in_specs = [cur(0), prev(1), cur(1), prev(2), cur(2)]
        args = [qkv] * 5
    else:
        in_specs = [cur(0), cur(1), cur(2)]
        args = [qkv] * 3
    return pl.pallas_call(
        functools.partial(_swa_kernel, dil=dil, n_units=n_units, n_heads=hb, has_prev=has_prev),
        grid=(batch, nb, n_hblk),
        in_specs=in_specs,
        out_specs=[pl.BlockSpec((span, wcol), lambda b, n, hh: (b * nb + n, hh)),
                   pl.BlockSpec((span, LANES), lambda b, n, hh: (b * nb + n, hh))],
        out_shape=[jax.ShapeDtypeStruct((batch * seq, SW_WIDTH), F32),
                   jax.ShapeDtypeStruct((batch * seq, n_hblk * LANES), F32)],
        compiler_params=_cparams(("arbitrary", "arbitrary", "arbitrary")),
        name=f"swa_prompt_g{gi}",
    )(*args)


def _merge_heads(o_refs, l_refs, rs, hbs):
    outs = []
    for h in range(SW_HEADS):
        cs = slice(h * SW_HEAD_DIM, (h + 1) * SW_HEAD_DIM)
        lses = []
        for l_ref, hb in zip(l_refs, hbs):
            lane = (h // hb) * LANES + h % hb
            lses.append(l_ref[rs, lane:lane + 1])
        mx = jnp.maximum(jnp.maximum(lses[0], lses[1]), lses[2])
        es = [jnp.exp(l - mx) for l in lses]
        den = es[0] + es[1] + es[2]
        acc = None
        for e, o_ref in zip(es, o_refs):
            o = o_ref[rs, cs]
            term = _bf16_round(jnp.broadcast_to(e / den, o.shape)) * _bf16_round(o)
            acc = term if acc is None else acc + term
        outs.append(acc)
    return jnp.concatenate(outs, axis=1)


def _swa_merge_kernel(o1, o2, o3, l1, l2, l3, out_ref, *, tm, hbs):
    ch = 128

    def body(c, carry):
        rs = pl.ds(pl.multiple_of(c * ch, ch), ch)
        out_ref[rs, :] = _merge_heads((o1, o2, o3), (l1, l2, l3), rs, hbs).astype(out_ref.dtype)
        return carry

    lax.fori_loop(0, tm // ch, body, 0)


def _swa_merge(os_, ls_):
    tm = 512
    m = os_[0].shape[0]
    hbs = tuple(SW_HEADS // (l.shape[1] // LANES) for l in ls_)
    row = lambda width: pl.BlockSpec((tm, width), lambda i: (i, 0))
    return pl.pallas_call(
        functools.partial(_swa_merge_kernel, tm=tm, hbs=hbs),
        grid=(m // tm,),
        in_specs=[row(SW_WIDTH)] * 3 + [row(l.shape[1]) for l in ls_],
        out_specs=row(SW_WIDTH),
        out_shape=jax.ShapeDtypeStruct((m, SW_WIDTH), BF16),
        compiler_params=_cparams(("arbitrary",)),
        name="swa_merge",
    )(*os_, *ls_)


def _swa_step_kernel(qkv_ref, c1_ref, c2_ref, c3_ref, o_ref):
    scale = SW_HEAD_DIM ** -0.5
    outs, lses = [], []
    for gi, c_ref in enumerate((c1_ref, c2_ref, c3_ref)):
        q = _bf16_round(qkv_ref[gi, 0])
        kn = _bf16_round(qkv_ref[gi, 1])
        vn = _bf16_round(qkv_ref[gi, 2])
        kc = _bf16_round(c_ref[:, 0])
        vc = _bf16_round(c_ref[:, 1])
        s = jnp.sum(kc * q[None], axis=-1, keepdims=True) * scale
        sn = jnp.sum(kn * q, axis=-1, keepdims=True) * scale
        m = jnp.maximum(jnp.max(s, axis=0), sn)
        lse = m + jnp.log(jnp.sum(jnp.exp(s - m[None]), axis=0) + jnp.exp(sn - m))
        shape = vc.shape
        p = _bf16_round(jnp.broadcast_to(jnp.exp(s - lse[None]), shape))
        pn = _bf16_round(jnp.broadcast_to(jnp.exp(sn - lse), shape[1:]))
        outs.append(jnp.sum(p * vc, axis=0) + pn * vn)
        lses.append(lse)
    mx = jnp.maximum(jnp.maximum(lses[0], lses[1]), lses[2])
    es = [jnp.exp(l - mx) for l in lses]
    den = es[0] + es[1] + es[2]
    acc = None
    for e, o in zip(es, outs):
        term = _bf16_round(jnp.broadcast_to(e / den, o.shape)) * _bf16_round(o)
        acc = term if acc is None else acc + term
    o_ref[...] = acc


def _swa_step(qkv, caches):
    b = qkv.shape[0]
    q5 = qkv.reshape(b, len(SW_GROUPS), 3, SW_HEADS, SW_HEAD_DIM)
    views, specs = [], []
    for c, (win, dil) in zip(caches, SW_GROUPS):
        views.append(c.reshape(b, win // dil, dil, 2, SW_HEADS, SW_HEAD_DIM))
        specs.append(pl.BlockSpec((None, win // dil, None, 2, SW_HEADS, SW_HEAD_DIM),
                                  lambda i: (i, 0, 0, 0, 0, 0)))
    o = pl.pallas_call(
        _swa_step_kernel,
        grid=(b,),
        in_specs=[pl.BlockSpec((None, len(SW_GROUPS), 3, SW_HEADS, SW_HEAD_DIM),
                               lambda i: (i, 0, 0, 0, 0))] + specs,
        out_specs=pl.BlockSpec((None, SW_HEADS, SW_HEAD_DIM), lambda i: (i, 0, 0)),
        out_shape=jax.ShapeDtypeStruct((b, SW_HEADS, SW_HEAD_DIM), F32),
        compiler_params=_cparams(("arbitrary",)),
        name="swa_step",
    )(q5, *views)
    return o.reshape(b, SW_WIDTH)


def _cache_shift_kernel(cur_ref, nxt_ref, knew_ref, vnew_ref, o_ref, *, wb):
    k = pl.program_id(1)
    last = pl.num_programs(1) - 1

    def row(i, carry):
        o_ref[i] = cur_ref[i + 1]
        return carry

    lax.fori_loop(0, wb - 1, row, 0, unroll=8)

    @pl.when(k < last)
    def _():
        o_ref[wb - 1] = nxt_ref[0]

    @pl.when(k == last)
    def _():
        o_ref[wb - 1, 0] = knew_ref[...]
        o_ref[wb - 1, 1] = vnew_ref[...]


def _cache_shift(qkv, caches):
    b = qkv.shape[0]
    q5 = qkv.reshape(b, len(SW_GROUPS), 3, SW_HEADS, SW_HEAD_DIM)
    outs = []
    for gi, c in enumerate(caches):
        w = c.shape[1]
        wb = min(w, 512)
        tail = (2, SW_HEADS, SW_HEAD_DIM)
        outs.append(pl.pallas_call(
            functools.partial(_cache_shift_kernel, wb=wb),
            grid=(b, w // wb),
            in_specs=[
                pl.BlockSpec((None, wb) + tail, lambda i, k: (i, k, 0, 0, 0)),
                pl.BlockSpec((None, 1) + tail, lambda i, k: (i, jnp.minimum((k + 1) * wb, w - 1), 0, 0, 0)),
                pl.BlockSpec((None, None, None, SW_HEADS, SW_HEAD_DIM), lambda i, k: (i, gi, 1, 0, 0)),
                pl.BlockSpec((None, None, None, SW_HEADS, SW_HEAD_DIM), lambda i, k: (i, gi, 2, 0, 0)),
            ],
            out_specs=pl.BlockSpec((None, wb) + tail, lambda i, k: (i, k, 0, 0, 0)),
            out_shape=jax.ShapeDtypeStruct(c.shape, F32),
            compiler_params=_cparams(("arbitrary", "arbitrary")),
            name=f"cache_shift_g{gi}",
        )(c, c, q5, q5))
    return outs


def _kv_window_kernel(k_ref, v_ref, o_ref, *, rows):
    for kv, src in enumerate((k_ref, v_ref)):
        for h in range(SW_HEADS):
            o_ref[pl.ds(kv * SW_HEADS + h, rows, stride=2 * SW_HEADS), :] = \
                src[:, h * SW_HEAD_DIM:(h + 1) * SW_HEAD_DIM]


def _kv_window(qkv, gi, win, batch, seq):
    wlen = min(win, seq)
    rows = 128
    nt = wlen // rows
    per_row = 2 * SW_HEADS
    src = lambda which: pl.BlockSpec(
        (rows, SW_WIDTH), lambda b, t: ((b * seq + seq - wlen) // rows + t, gi * 3 + which))
    out = pl.pallas_call(
        functools.partial(_kv_window_kernel, rows=rows),
        grid=(batch, nt),
        in_specs=[src(1), src(2)],
        out_specs=pl.BlockSpec((None, rows * per_row, SW_HEAD_DIM), lambda b, t: (b, t, 0)),
        out_shape=jax.ShapeDtypeStruct((batch, wlen * per_row, SW_HEAD_DIM), F32),
        compiler_params=_cparams(("arbitrary", "arbitrary")),
        name=f"kv_window_g{gi}",
    )(qkv, qkv)
    return out.reshape(batch, wlen, 2, SW_HEADS, SW_HEAD_DIM)


def _router_kernel(x_ref, gn_ref, sc_ref, sh_ref, wr_ref, br_ref, *rest, tm):
    h_ref, ids_ref, gates_ref, hf_ref = rest[-4:]
    _norm_mod_into(x_ref, gn_ref, sc_ref, sh_ref, [hf_ref], tm)
    for s in range(ROW_TILES):
        h_ref[pl.ds(s, tm, stride=ROW_TILES), :] = hf_ref[:, s * LANES:(s + 1) * LANES]
    logits = jnp.dot(hf_ref[...].astype(BF16), wr_ref[...].astype(BF16),
                     preferred_element_type=F32) + br_ref[...]
    lane = lax.broadcasted_iota(jnp.int32, logits.shape, 1)
    big = jnp.int32(1 << 20)
    is_g = (lane >= MOE_EXPERTS) & (lane < MOE_EXPERTS + MOE_GROUPS)
    glog = jnp.where(is_g, logits, -jnp.inf)
    gmax = jnp.max(glog, axis=-1, keepdims=True)
    gsel = jnp.min(jnp.where(glog == gmax, lane - MOE_EXPERTS, big), axis=-1, keepdims=True)
    gsum = jnp.sum(jnp.where(is_g, jnp.exp(glog - gmax), 0.0), axis=-1, keepdims=True)
    pg = 1.0 / gsum
    in_grp = (lane < MOE_EXPERTS) & ((lane // MOE_PER_GROUP) == gsel)
    el = jnp.where(in_grp, logits, -jnp.inf)
    v1 = jnp.max(el, axis=-1, keepdims=True)
    i1 = jnp.min(jnp.where(el == v1, lane, big), axis=-1, keepdims=True)
    el2 = jnp.where(lane == i1, -jnp.inf, el)
    v2 = jnp.max(el2, axis=-1, keepdims=True)
    i2 = jnp.min(jnp.where(el2 == v2, lane, big), axis=-1, keepdims=True)
    e2 = jnp.exp(v2 - v1)
    den = 1.0 + e2
    ids_ref[...] = jnp.where(lane == 0, i1, jnp.where(lane == 1, i2, 0))
    gates_ref[...] = jnp.where(lane == 0, pg * (1.0 / den), jnp.where(lane == 1, pg * (e2 / den), 0.0))


def _router(x, gn, mod, wr, br, hcat, n_tok, *, tm, rows_per_batch, row0):
    m = x.shape[0]
    blk0 = row0 // tm
    bpb = max(rows_per_batch // tm, 1)
    nb = m // tm
    n_steps = nb + (1 if hcat is None and n_tok > m else 0)
    cl = lambda i: jnp.minimum(i, nb - 1)
    msp = lambda col: pl.BlockSpec((None, mod.shape[1], D_MODEL), lambda i: (cl(i) // bpb, 0, col))
    in_specs = [
        pl.BlockSpec((tm, D_MODEL), lambda i: (cl(i), 0)),
        pl.BlockSpec((1, D_MODEL), lambda i: (0, 0)),
        msp(4), msp(3),
        pl.BlockSpec((D_MODEL, LANES), lambda i: (0, 0)),
        pl.BlockSpec((1, LANES), lambda i: (0, 0)),
    ]
    args = [x, gn.reshape(1, D_MODEL), mod, mod, wr, br]
    aliases = {}
    if hcat is not None:
        in_specs.append(pl.BlockSpec(memory_space=pl.ANY))
        args.append(hcat)
        aliases = {len(args) - 1: 0}
    return pl.pallas_call(
        functools.partial(_router_kernel, tm=tm),
        grid=(n_steps,),
        in_specs=in_specs,
        out_specs=[pl.BlockSpec((tm * ROW_TILES, LANES), lambda i: (blk0 + i, 0)),
                   pl.BlockSpec((tm, LANES), lambda i: (cl(i), 0)),
                   pl.BlockSpec((tm, LANES), lambda i: (cl(i), 0))],
        out_shape=[jax.ShapeDtypeStruct((n_tok * ROW_TILES, LANES), F32),
                   jax.ShapeDtypeStruct((m, LANES), jnp.int32),
                   jax.ShapeDtypeStruct((m, LANES), F32)],
        scratch_shapes=[pltpu.VMEM((tm, D_MODEL), F32)],
        input_output_aliases=aliases,
        compiler_params=_cparams(("arbitrary",)),
        name="moe_router",
    )(*args)


def _experts_kernel(tok_ref, run_ref, ng_ref, qlo_ref, qhi_ref, ce_ref, ck_ref, cs_ref, meta_ref,
                    h_hbm, wgu_hbm, wd_hbm, y_ref,
                    rows_ref, stg_ref, wgu_ref, wd_ref, row_sem, w_sem, *, layer):
    b = pl.program_id(0)
    n_used, q_total = meta_ref[0], meta_ref[1]
    rt = ROW_TILES
    grp = DMA_UNROLL * rt

    def issue_rows(blk, slot):
        def body(g, carry):
            for u in range(DMA_UNROLL):
                r = g * DMA_UNROLL + u
                tok = tok_ref[blk * MOE_ROWS + r]
                pltpu.make_async_copy(h_hbm.at[pl.ds(pl.multiple_of(tok * rt, rt), rt)],
                                      rows_ref.at[slot, pl.ds(pl.multiple_of(r * rt, rt), rt)],
                                      row_sem.at[slot]).start()
            return carry

        lax.fori_loop(0, ng_ref[blk], body, 0)

    def wait_rows(blk, slot):
        def body(g, carry):
            pltpu.make_async_copy(h_hbm.at[pl.ds(0, grp)], rows_ref.at[slot, pl.ds(0, grp)], row_sem.at[slot]).wait()
            return carry

        lax.fori_loop(0, ng_ref[blk], body, 0)

    def chunk_start(q):
        e, k, st = ce_ref[q], ck_ref[q], q % W_STAGES
        row0 = lambda kk: pl.ds(pl.multiple_of(kk * W_CHUNK, W_CHUNK), W_CHUNK)

        @pl.when(k < GU_CHUNKS)
        def _():
            pltpu.make_async_copy(wgu_hbm.at[layer, e, row0(k)], stg_ref.at[st], w_sem.at[st]).start(priority=1)

        @pl.when(k >= GU_CHUNKS)
        def _():
            pltpu.make_async_copy(wd_hbm.at[layer, e, row0(k - GU_CHUNKS)], stg_ref.at[st],
                                  w_sem.at[st]).start(priority=1)

    def chunk_process(q):
        k, slot, st = ck_ref[q], cs_ref[q], q % W_STAGES
        pltpu.make_async_copy(wgu_hbm.at[layer, 0, pl.ds(0, W_CHUNK)], stg_ref.at[st], w_sem.at[st]).wait()
        sub = 32

        @pl.when(k < GU_CHUNKS)
        def _():
            for i in range(W_CHUNK // sub):
                dst = pl.ds(pl.multiple_of(k * W_CHUNK + i * sub, sub), sub)
                wgu_ref[slot, dst, :] = stg_ref[st, pl.ds(i * sub, sub), :].astype(BF16)

        @pl.when(k >= GU_CHUNKS)
        def _():
            for i in range(W_CHUNK // sub):
                dst = pl.ds(pl.multiple_of((k - GU_CHUNKS) * W_CHUNK + i * sub, sub), sub)
                wd_ref[slot, dst, :] = stg_ref[st, pl.ds(i * sub, sub), :].astype(BF16)

        @pl.when(q + W_STAGES < q_total)
        def _():
            chunk_start(q + W_STAGES)

    def process_range(lo, hi):
        def body(q, carry):
            chunk_process(q)
            return carry

        lax.fori_loop(lo, hi, body, 0)

    @pl.when(b == 0)
    def _():
        rows_ref[...] = jnp.zeros_like(rows_ref)
        issue_rows(0, 0)
        for q in range(W_STAGES):
            chunk_start(q)
        process_range(0, RUN_CHUNKS)

    @pl.when(b + 1 < n_used)
    def _():
        issue_rows(b + 1, (b + 1) % 2)

    @pl.when(b < n_used)
    def _():
        rslot = b % 2
        q_lo, q_hi = qlo_ref[b], qhi_ref[b]
        q_mid = (q_lo + q_hi) // 2
        process_range(q_lo, q_mid)
        wait_rows(b, rslot)
        x = jnp.concatenate(
            [rows_ref[rslot, pl.ds(s, MOE_ROWS, stride=rt), :].astype(BF16) for s in range(rt)], axis=1)
        wslot = run_ref[b] % 2
        for sl in range(2):
            @pl.when(wslot == sl)
            def _():
                au = jnp.dot(x, wgu_ref[sl], preferred_element_type=F32)
                hmid = (_silu(au[:, :MOE_FF]) * au[:, MOE_FF:]).astype(BF16)
                y = jnp.dot(hmid, wd_ref[sl], preferred_element_type=F32)
                for s in range(rt):
                    y_ref[pl.ds(s, MOE_ROWS, stride=rt), :] = y[:, s * LANES:(s + 1) * LANES]

        process_range(q_mid, q_hi)

    @pl.when(b >= n_used)
    def _():
        y_ref[...] = jnp.zeros_like(y_ref)


def _moe_experts(hcat, w_gu, w_down, layer, row_tok, blk_run, blk_groups, qlo, qhi, chunk_e, chunk_k, chunk_s, meta,
                 n_blocks):
    any_spec = pl.BlockSpec(memory_space=pl.ANY)
    return pl.pallas_call(
        functools.partial(_experts_kernel, layer=layer),
        grid_spec=pltpu.PrefetchScalarGridSpec(
            num_scalar_prefetch=9,
            grid=(n_blocks,),
            in_specs=[any_spec, any_spec, any_spec],
            out_specs=pl.BlockSpec((MOE_ROWS * ROW_TILES, LANES), lambda b, *_: (b, 0)),
            scratch_shapes=[pltpu.VMEM((2, MOE_ROWS * ROW_TILES, LANES), F32),
                            pltpu.VMEM((W_STAGES, W_CHUNK, D_MODEL), F32),
                            pltpu.VMEM((2, D_MODEL, 2 * MOE_FF), BF16),
                            pltpu.VMEM((2, MOE_FF, D_MODEL), BF16),
                            pltpu.SemaphoreType.DMA((2,)),
                            pltpu.SemaphoreType.DMA((W_STAGES,))],
        ),
        out_shape=jax.ShapeDtypeStruct((n_blocks * MOE_ROWS * ROW_TILES, LANES), F32),
        compiler_params=pltpu.CompilerParams(dimension_semantics=("arbitrary",), vmem_limit_bytes=VMEM_LIMIT,
                                             disable_bounds_checks=True),
        name="moe_experts",
    )(row_tok, blk_run, blk_groups, qlo, qhi, chunk_e, chunk_k, chunk_s, meta, hcat, w_gu, w_down)


def _combine_kernel(dest_ref, y_hbm, x_ref, g_ref, gates_ref, o_ref, y0_ref, y1_ref, sem, *, tm, tok0):
    i = pl.program_id(0)
    rt = ROW_TILES

    def issue(tile, slot):
        def body(g, carry):
            for u in range(DMA_UNROLL // 2):
                r = g * (DMA_UNROLL // 2) + u
                a = (tok0 + tile * tm + r) * MOE_TOP_K
                for j, y_ref in enumerate((y0_ref, y1_ref)):
                    pltpu.make_async_copy(y_hbm.at[pl.ds(pl.multiple_of(dest_ref[a + j] * rt, rt), rt)],
                                          y_ref.at[slot, pl.ds(pl.multiple_of(r * rt, rt), rt)],
                                          sem.at[slot]).start(priority=j)
            return carry

        lax.fori_loop(0, tm // (DMA_UNROLL // 2), body, 0)

    @pl.when(i == 0)
    def _():
        issue(0, 0)

    @pl.when(i + 1 < pl.num_programs(0))
    def _():
        issue(i + 1, (i + 1) % 2)

    slot = i % 2
    for y_ref in (y0_ref, y1_ref):
        pltpu.make_async_copy(y_hbm.at[pl.ds(0, tm * rt)], y_ref.at[slot], sem.at[slot]).wait()
    gt = gates_ref[...]
    g0 = jnp.broadcast_to(gt[:, 0:1], (tm, LANES))
    g1 = jnp.broadcast_to(gt[:, 1:2], (tm, LANES))
    for s in range(rt):
        cs = slice(s * LANES, (s + 1) * LANES)
        rows = pl.ds(s, tm, stride=rt)
        y = y0_ref[slot, rows, :] * g0 + y1_ref[slot, rows, :] * g1
        o_ref[:, cs] = x_ref[:, cs] + g_ref[:, cs] * y


def _moe_combine(y, dest, x, mod, gates, *, tm, rows_per_batch, tok0):
    m = x.shape[0]
    bpb = max(rows_per_batch // tm, 1)
    return pl.pallas_call(
        functools.partial(_combine_kernel, tm=tm, tok0=tok0),
        grid_spec=pltpu.PrefetchScalarGridSpec(
            num_scalar_prefetch=1,
            grid=(m // tm,),
            in_specs=[pl.BlockSpec(memory_space=pl.ANY),
                      pl.BlockSpec((tm, D_MODEL), lambda i, d: (i, 0)),
                      pl.BlockSpec((None, mod.shape[1], D_MODEL), lambda i, d: (i // bpb, 0, 5)),
                      pl.BlockSpec((tm, LANES), lambda i, d: (i, 0))],
            out_specs=pl.BlockSpec((tm, D_MODEL), lambda i, d: (i, 0)),
            scratch_shapes=[pltpu.VMEM((2, tm * ROW_TILES, LANES), F32)] * 2
            + [pltpu.SemaphoreType.DMA((2,))],
        ),
        out_shape=jax.ShapeDtypeStruct((m, D_MODEL), F32),
        compiler_params=pltpu.CompilerParams(dimension_semantics=("arbitrary",), vmem_limit_bytes=VMEM_LIMIT,
                                             disable_bounds_checks=True),
        name="moe_combine",
    )(dest, y, x, mod, gates)


def _moe_layer(xp, xs, mod_p, mod_s, gn, w_group, b_group, w_router, b_router, w_gu, w_down, layer, seq):
    n_p, n_s = xp.shape[0], xs.shape[0]
    n_tok = n_p + n_s
    pad = LANES - MOE_EXPERTS - MOE_GROUPS
    wr = jnp.concatenate([w_router, w_group, jnp.zeros((D_MODEL, pad), F32)], axis=1)
    br = jnp.concatenate([b_router, b_group, jnp.zeros((pad,), F32)]).reshape(1, LANES)
    hcat, ids_p, gates_p = _router(xp, gn, mod_p, wr, br, None, n_tok, tm=256, rows_per_batch=seq, row0=0)
    hcat, ids_s, gates_s = _router(xs, gn, mod_s, wr, br, hcat, n_tok, tm=n_s, rows_per_batch=n_s, row0=n_p)

    expert = jnp.concatenate([ids_p[:, :MOE_TOP_K], ids_s[:, :MOE_TOP_K]], axis=0).reshape(-1)
    n_assign = n_tok * MOE_TOP_K
    n_blocks = -(-n_assign // MOE_ROWS) + MOE_EXPERTS
    onehot = (expert[:, None] == jnp.arange(MOE_EXPERTS, dtype=jnp.int32)[None, :]).astype(jnp.int32)
    csum = jnp.cumsum(onehot, axis=0)
    rank = jnp.sum(csum * onehot, axis=1) - 1
    counts = csum[-1]
    padded = (counts + MOE_ROWS - 1) // MOE_ROWS * MOE_ROWS
    pend = jnp.cumsum(padded)
    pstart = pend - padded
    dest = (pstart[expert] + rank).astype(jnp.int32)
    tok = jnp.arange(n_assign, dtype=jnp.int32) // MOE_TOP_K
    row_tok = jnp.zeros((n_blocks * MOE_ROWS,), jnp.int32).at[dest].set(tok, unique_indices=True)
    blk_ids = jnp.arange(n_blocks, dtype=jnp.int32)
    blk_expert = jnp.minimum(jnp.searchsorted(pend, blk_ids * MOE_ROWS, side="right"),
                             MOE_EXPERTS - 1).astype(jnp.int32)
    n_used = (pend[-1:] // MOE_ROWS).astype(jnp.int32)
    nonempty = counts > 0
    n_runs = jnp.sum(nonempty.astype(jnp.int32))
    run_of_expert = jnp.cumsum(nonempty.astype(jnp.int32)) - 1
    run_expert = jnp.argsort(jnp.where(nonempty, 0, 1), stable=True).astype(jnp.int32)
    blk_run = run_of_expert[blk_expert].astype(jnp.int32)
    j_in_run = blk_ids - pstart[blk_expert] // MOE_ROWS
    n_in_run = jnp.maximum(padded[blk_expert] // MOE_ROWS, 1)
    q_total = n_runs * RUN_CHUNKS
    streams = (blk_ids < n_used[0]) & (blk_run + 1 < n_runs)
    base = (blk_run + 1) * RUN_CHUNKS
    qlo = jnp.where(streams, base + (RUN_CHUNKS * j_in_run) // n_in_run, q_total).astype(jnp.int32)
    qhi = jnp.where(streams, base + (RUN_CHUNKS * (j_in_run + 1)) // n_in_run, q_total).astype(jnp.int32)
    chunk_ids = jnp.arange(MOE_EXPERTS * RUN_CHUNKS, dtype=jnp.int32)
    chunk_e = run_expert[chunk_ids // RUN_CHUNKS]
    chunk_k = chunk_ids % RUN_CHUNKS
    chunk_s = (chunk_ids // RUN_CHUNKS) % 2
    meta = jnp.concatenate([n_used, q_total[None].astype(jnp.int32)])
    valid_rows = jnp.clip(counts[blk_expert] - j_in_run * MOE_ROWS, 0, MOE_ROWS)
    blk_groups = jnp.where(blk_ids < n_used[0], (valid_rows + DMA_UNROLL - 1) // DMA_UNROLL, 0).astype(jnp.int32)

    y = _moe_experts(hcat, w_gu, w_down, layer, row_tok, blk_run, blk_groups, qlo, qhi, chunk_e, chunk_k, chunk_s,
                     meta, n_blocks)
    xp_new = _moe_combine(y, dest, xp, mod_p, gates_p, tm=256, rows_per_batch=seq, tok0=0)
    xs_new = _moe_combine(y, dest, xs, mod_s, gates_s, tm=n_s, rows_per_batch=n_s, tok0=n_p)
    return xp_new, xs_new


def _final_norm_kernel(x_ref, g_ref, o_ref):
    x = x_ref[...]
    inv = lax.rsqrt(jnp.mean(x * x, axis=-1, keepdims=True) + EPS)
    o_ref[...] = (x * inv) * g_ref[...]


def _final_norm(x, g, tm):
    m = x.shape[0]
    return pl.pallas_call(
        _final_norm_kernel,
        grid=(m // tm,),
        in_specs=[pl.BlockSpec((tm, D_MODEL), lambda i: (i, 0)), pl.BlockSpec((1, D_MODEL), lambda i: (0, 0))],
        out_specs=pl.BlockSpec((tm, D_MODEL), lambda i: (i, 0)),
        out_shape=jax.ShapeDtypeStruct((m, D_MODEL), F32),
        compiler_params=_cparams(("arbitrary",)),
        name="final_norm",
    )(x, g.reshape(1, D_MODEL))


def _rope_tables(pos):
    half = SW_HEAD_DIM // 2
    inv_freq = ROPE_THETA ** (-jnp.arange(half, dtype=F32) / half)
    ang = pos.astype(F32)[:, None] * inv_freq[None, :]
    cos, sin = jnp.cos(ang), jnp.sin(ang)
    return jnp.concatenate([cos, cos], axis=1), jnp.concatenate([-sin, sin], axis=1)


def kernel(x_prompt, x_sample, state_hgrn, state_conv, cache_swa_g1, cache_swa_g2, cache_swa_g3, c_prompt, c_sample, ada_w, ada_b, norm_mix, norm_ffn, norm_final, hg_w_in, hg_w_out, hg_norm, hg_lower, cv_w_in, cv_w_conv, cv_w_out, sw_w_in, sw_w_out, moe_w_group, moe_b_group, moe_w_router, moe_b_router, moe_w_gu, moe_w_down):
    bp, seq, d = x_prompt.shape
    bs = x_sample.shape[0]
    n_p = bp * seq
    xp = x_prompt.reshape(n_p, d)
    xs = x_sample.reshape(bs, d)

    mod_all = _ada_mod(jnp.concatenate([c_prompt, c_sample], axis=0), ada_w, ada_b)
    sm = jax.nn.softmax(hg_lower.astype(F32), axis=0)
    lower = jnp.cumsum(sm, axis=0) - sm[0]
    rope_p = _rope_tables(jnp.arange(seq))
    rope_s = _rope_tables(jnp.full((bs,), PAST_LEN, jnp.int32))

    tm_p, tn_p = 1024, 512
    hg_p, hg_s, cv_p, cv_s = [], [], [], []
    sw_p, sw_s = None, None
    for i in range(DEPTH):
        kind, j = i % N_MIXERS, i // N_MIXERS
        mod_p = mod_all[i, :bp].reshape(bp, 1, 6 * d)
        mod_s = mod_all[i, bp:].reshape(1, bs, 6 * d)
        pp = dict(tm=tm_p, tn=tn_p)
        pp_in = dict(tm=tm_p, tn=1024)
        ps = dict(tm=bs, tn=1024)
        hp = _norm_mod(xp, norm_mix[i], mod_p, 0, 1, tm=512, rows_per_batch=seq)
        hs = _norm_mod(xs, norm_mix[i], mod_s, 0, 1, tm=bs, rows_per_batch=bs)
        if kind == 0:
            proj_p = _mm(hp, hg_w_in, j, **pp_in)
            proj_s = _mm(hs, hg_w_in, j, **ps)
            lhs_p, st_p = _gla_prompt(proj_p, lower[j], hg_norm[j], bp, seq)
            lhs_s, st_s = _gla_step(proj_s, lower[j], hg_norm[j], state_hgrn, j)
            hg_p.append(st_p)
            hg_s.append(st_s)
            w_out = hg_w_out
        elif kind == 1:
            proj_p = _mm(hp, cv_w_in, j, **pp_in)
            proj_s = _mm(hs, cv_w_in, j, **ps)
            lhs_p, tail_p = _conv_prompt(proj_p, cv_w_conv[j], bp, seq)
            lhs_s, tail_s = _conv_step(proj_s, cv_w_conv[j], state_conv[j])
            cv_p.append(tail_p)
            cv_s.append(tail_s)
            w_out = cv_w_out
        else:
            qkv_p = _mm(hp, sw_w_in, j, tm=512, tn=SW_WIDTH, rope=rope_p)
            qkv_s = _mm(hs, sw_w_in, j, rope=rope_s, **ps)
            os_, ls_ = [], []
            for gi, (win, dil) in enumerate(SW_GROUPS):
                o_g, l_g = _swa_prompt_group(qkv_p, gi, dil, bp, seq)
                os_.append(o_g)
                ls_.append(l_g)
            lhs_p = _swa_merge(os_, ls_)
            caches = (cache_swa_g1[j], cache_swa_g2[j], cache_swa_g3[j])
            lhs_s = _swa_step(qkv_s, caches)
            w_out = sw_w_out
            sw_p = [_kv_window(qkv_p, gi, win, bp, seq)[None] for gi, (win, _) in enumerate(SW_GROUPS)]
            sw_s = [c[None] for c in _cache_shift(qkv_s, caches)]
        xp = _mm(lhs_p, w_out, j, resgate=(xp, mod_p, 2, seq), **pp)
        xs = _mm(lhs_s, w_out, j, resgate=(xs, mod_s, 2, bs), **ps)
        xp, xs = _moe_layer(xp, xs, mod_p, mod_s, norm_ffn[i], moe_w_group[i], moe_b_group[i],
                            moe_w_router[i], moe_b_router[i], moe_w_gu, moe_w_down, i, seq)

    y_p = _final_norm(xp, norm_final, 512).reshape(bp, seq, d)
    y_s = _final_norm(xs, norm_final, bs).reshape(bs, 1, d)
    return (y_p, y_s, jnp.stack(hg_p), jnp.stack(hg_s), jnp.stack(cv_p), jnp.stack(cv_s),
            sw_p[0], sw_s[0], sw_p[1], sw_s[1], sw_p[2], sw_s[2])
```

```python
import functools

import jax
import jax.numpy as jnp
from jax import lax
from jax.experimental import pallas as pl
from jax.experimental.pallas import tpu as pltpu

F32 = jnp.float32
BF16 = jnp.bfloat16

D_MODEL = 2048
DEPTH = 4
N_MIXERS = 3
EPS = 1e-6
MASK_VALUE = -1e30
F_FLOOR = 1e-30
HG_DK = 128
HG_HEADS = D_MODEL // HG_DK
HG_CHUNK = 64
HG_SUB = 16
CONV_WIDTH = 3
SW_GROUPS = ((128, 1), (512, 4), (2048, 16))
SW_HEADS = 8
SW_HEAD_DIM = 128
SW_WIDTH = SW_HEADS * SW_HEAD_DIM
SW_BAND = 128
ROPE_THETA = 10000.0
MOE_GROUPS = 4
MOE_PER_GROUP = 8
MOE_EXPERTS = MOE_GROUPS * MOE_PER_GROUP
MOE_TOP_K = 2
MOE_FF = 1024
PAST_LEN = 16384

LANES = 128
SUBLANES = 8
VMEM_LIMIT = 52 * 1024 * 1024
MOE_ROWS = 256
ROW_TILES = D_MODEL // LANES
DMA_UNROLL = 8
W_CHUNK = 256
GU_CHUNKS = D_MODEL // W_CHUNK
RUN_CHUNKS = GU_CHUNKS + MOE_FF // W_CHUNK
W_STAGES = 4


def _cparams(sem):
    return pltpu.CompilerParams(dimension_semantics=sem, vmem_limit_bytes=VMEM_LIMIT)


def _sigmoid(x):
    return 1.0 / (1.0 + jnp.exp(-x))


def _silu(x):
    return x * _sigmoid(x)


def _bf16_round(x):
    return x.astype(BF16).astype(F32)


def _ada_kernel(c_ref, w_ref, b_ref, o_ref):
    c = c_ref[...]
    cond = _silu(c).astype(BF16)
    o_ref[...] = jnp.dot(cond, w_ref[...].astype(BF16), preferred_element_type=F32) + b_ref[...]


def _ada_mod(c_all, ada_w, ada_b):
    rows = c_all.shape[0]
    n = ada_w.shape[-1]
    tn = 1024
    return pl.pallas_call(
        _ada_kernel,
        grid=(DEPTH, n // tn),
        in_specs=[
            pl.BlockSpec((rows, D_MODEL), lambda l, j: (0, 0)),
            pl.BlockSpec((None, D_MODEL, tn), lambda l, j: (l, 0, j)),
            pl.BlockSpec((None, 1, tn), lambda l, j: (l, 0, j)),
        ],
        out_specs=pl.BlockSpec((None, rows, tn), lambda l, j: (l, 0, j)),
        out_shape=jax.ShapeDtypeStruct((DEPTH, rows, n), F32),
        compiler_params=_cparams(("arbitrary", "arbitrary")),
        name="ada_mod",
    )(c_all, ada_w, ada_b.reshape(DEPTH, 1, n))


def _norm_mod_rows(x, gn, sc, sh):
    inv = lax.rsqrt(jnp.mean(x * x, axis=-1, keepdims=True) + EPS)
    return (x * inv) * gn * (1.0 + sc) + sh


def _norm_mod_into(x_ref, gn_ref, sc_ref, sh_ref, dst_refs, tm):
    ch = min(tm, 128)
    per_row = sc_ref.shape[0] != 1

    def body(c, carry):
        rs = pl.ds(pl.multiple_of(c * ch, ch), ch)
        sc = sc_ref[rs, :] if per_row else sc_ref[...]
        sh = sh_ref[rs, :] if per_row else sh_ref[...]
        h = _norm_mod_rows(x_ref[rs, :], gn_ref[...], sc, sh)
        for d in dst_refs:
            d[rs, :] = h.astype(d.dtype)
        return carry

    lax.fori_loop(0, tm // ch, body, 0)


def _norm_mod_kernel(x_ref, gn_ref, sc_ref, sh_ref, o_ref, *, tm):
    _norm_mod_into(x_ref, gn_ref, sc_ref, sh_ref, [o_ref], tm)


def _norm_mod(x, gn, mod, shift_col, scale_col, *, tm, rows_per_batch):
    m = x.shape[0]
    bpb = max(rows_per_batch // tm, 1)
    msp = lambda col: pl.BlockSpec((None, mod.shape[1], D_MODEL), lambda i: (i // bpb, 0, col))
    return pl.pallas_call(
        functools.partial(_norm_mod_kernel, tm=tm),
        grid=(m // tm,),
        in_specs=[pl.BlockSpec((tm, D_MODEL), lambda i: (i, 0)),
                  pl.BlockSpec((1, D_MODEL), lambda i: (0, 0)),
                  msp(scale_col), msp(shift_col)],
        out_specs=pl.BlockSpec((tm, D_MODEL), lambda i: (i, 0)),
        out_shape=jax.ShapeDtypeStruct((m, D_MODEL), BF16),
        compiler_params=_cparams(("arbitrary",)),
        name="norm_mod",
    )(x, gn.reshape(1, D_MODEL), mod, mod)


def _rope_tile(acc, cos, sin_signed):
    outs = []
    for h in range(acc.shape[1] // SW_HEAD_DIM):
        xh = acc[:, h * SW_HEAD_DIM:(h + 1) * SW_HEAD_DIM]
        outs.append(xh * cos + pltpu.roll(xh, SW_HEAD_DIM // 2, 1) * sin_signed)
    return jnp.concatenate(outs, axis=1)


def _mm_kernel(x_ref, w_ref, *rest, epilogue):
    o_ref, wb_ref = rest[-2:]
    j, i = pl.program_id(0), pl.program_id(1)

    @pl.when(i == 0)
    def _():
        wb_ref[...] = w_ref[...].astype(BF16)

    if epilogue == "rope":
        cos_ref, sin_ref = rest[:2]
        is_qk = (j % 3) < 2
        cos = jnp.where(is_qk, cos_ref[...], 1.0)
        sin = jnp.where(is_qk, sin_ref[...], 0.0)
        xb = x_ref[...].astype(BF16)
        piece = 2 * SW_HEAD_DIM
        for c in range(o_ref.shape[1] // piece):
            cs = slice(c * piece, (c + 1) * piece)
            acc = jnp.dot(xb, wb_ref[:, cs], preferred_element_type=F32)
            o_ref[:, cs] = _rope_tile(acc, cos, sin)
        return
    acc = jnp.dot(x_ref[...].astype(BF16), wb_ref[...], preferred_element_type=F32)
    if epilogue == "resgate":
        res_ref, g_ref = rest[:2]
        o_ref[...] = res_ref[...] + g_ref[...] * acc
    else:
        o_ref[...] = acc


def _mm(x, w, wl, *, tm, tn, rope=None, resgate=None):
    m, k = x.shape
    n = w.shape[2]
    in_specs = [pl.BlockSpec((tm, k), lambda j, i: (i, 0)),
                pl.BlockSpec((None, k, tn), lambda j, i: (wl, 0, j))]
    args = [x, w]
    epilogue = "none"
    if rope is not None:
        epilogue = "rope"
        nblk = rope[0].shape[0] // tm
        in_specs += [pl.BlockSpec((tm, SW_HEAD_DIM), lambda j, i: (i % nblk, 0))] * 2
        args += list(rope)
    if resgate is not None:
        epilogue = "resgate"
        res, mod, gate_col, rows_per_batch = resgate
        bpb = max(rows_per_batch // tm, 1)
        per = D_MODEL // tn
        in_specs += [pl.BlockSpec((tm, tn), lambda j, i: (i, j)),
                     pl.BlockSpec((None, mod.shape[1], tn), lambda j, i: (i // bpb, 0, gate_col * per + j))]
        args += [res, mod]
    return pl.pallas_call(
        functools.partial(_mm_kernel, epilogue=epilogue),
        grid=(n // tn, m // tm),
        in_specs=in_specs,
        out_specs=pl.BlockSpec((tm, tn), lambda j, i: (i, j)),
        out_shape=jax.ShapeDtypeStruct((m, n), F32),
        scratch_shapes=[pltpu.VMEM((k, tn), BF16)],
        compiler_params=_cparams(("arbitrary", "arbitrary")),
        name="mm_" + epilogue,
    )(*args)


def _hgrn_gates(qp, fp, lb):
    q = _silu(qp)
    f = lb + (1.0 - lb) * _sigmoid(fp)
    logf = jnp.log(jnp.maximum(f, F_FLOOR))
    k = (1.0 - lb) * _sigmoid(-fp)
    return q, logf, k


def _head_norm_gate(o, gn, gp):
    inv = lax.rsqrt(jnp.mean(o * o, axis=-1, keepdims=True) + EPS)
    return (o * inv) * gn * _silu(gp)


def _gla_kernel(q_ref, f_ref, i_ref, g_ref, lb_ref, gn_ref, o_ref, s_out_ref, st_ref, *, n_chunks, n_hb):
    c_len, n_sub = HG_CHUNK, HG_CHUNK // HG_SUB
    t = pl.program_id(2)

    @pl.when(t == 0)
    def _():
        st_ref[...] = jnp.zeros_like(st_ref)

    row = lax.broadcasted_iota(jnp.int32, (c_len, HG_DK), 0)
    r4 = lax.broadcasted_iota(jnp.int32, (c_len, n_sub * c_len), 0)
    c4 = lax.broadcasted_iota(jnp.int32, (c_len, n_sub * c_len), 1)
    mask4 = ((r4 // HG_SUB) == (c4 // c_len)) & ((c4 % c_len) <= r4)
    nt = (((1,), (1,)), ((), ()))
    tn = (((0,), (0,)), ((), ()))

    for c, hh in [(c, hh) for c in range(n_chunks) for hh in range(n_hb)]:
        rs = pl.ds(c * c_len, c_len)
        cs = slice(hh * HG_DK, (hh + 1) * HG_DK)
        lb, gn = lb_ref[:, cs], gn_ref[:, cs]
        q, logf, k = _hgrn_gates(q_ref[rs, cs], f_ref[rs, cs], lb)
        v = i_ref[rs, cs]
        g = logf
        sh = 1
        while sh < c_len:
            g = g + jnp.where(row >= sh, pltpu.roll(g, sh, 0), 0.0)
            sh *= 2
        refs = [g[HG_SUB * i + HG_SUB // 2 - 1:HG_SUB * i + HG_SUB // 2, :] for i in range(n_sub)]
        mrows = jnp.concatenate([jnp.broadcast_to(r, (HG_SUB, HG_DK)) for r in refs], axis=0)
        qh = (q * jnp.exp(g - mrows)).astype(BF16)
        khs = [k * jnp.exp(jnp.where(row < HG_SUB * (i + 1), refs[i] - g, -jnp.inf)) for i in range(n_sub)]
        kh = jnp.concatenate(khs, axis=0).astype(BF16)
        a4 = lax.dot_general(qh, kh, nt, preferred_element_type=F32)
        a4 = jnp.where(mask4, a4, 0.0).astype(BF16)
        vb = v.astype(BF16)
        v4 = jnp.concatenate([vb] * n_sub, axis=0)
        st = st_ref[hh]
        o = jnp.dot(a4, v4, preferred_element_type=F32)
        o = o + lax.dot_general((q * jnp.exp(g)).astype(BF16), st.astype(BF16), nt,
                                preferred_element_type=F32)
        glast = g[c_len - 1:c_len, :]
        kd = (k * jnp.exp(glast - g)).astype(BF16)
        st_ref[hh] = st * jnp.exp(glast) + lax.dot_general(vb, kd, tn, preferred_element_type=F32)
        o_ref[rs, cs] = _head_norm_gate(o, gn, g_ref[rs, cs]).astype(o_ref.dtype)

    @pl.when(t == pl.num_programs(2) - 1)
    def _():
        for hh in range(n_hb):
            s_out_ref[hh] = st_ref[hh].T


def _gla_prompt(proj, lower, g_norm, batch, seq):
    tb = 512
    n_hb = 2
    nt = seq // tb
    hblk = HG_HEADS // n_hb
    wcol = n_hb * HG_DK

    def col(off):
        return pl.BlockSpec((tb, wcol), lambda b, hh, t: (b * nt + t, off * hblk + hh))

    return pl.pallas_call(
        functools.partial(_gla_kernel, n_chunks=tb // HG_CHUNK, n_hb=n_hb),
        grid=(batch, hblk, nt),
        in_specs=[col(0), col(1), col(2), col(3),
                  pl.BlockSpec((1, wcol), lambda b, hh, t: (0, hh)),
                  pl.BlockSpec((1, wcol), lambda b, hh, t: (0, hh))],
        out_specs=[pl.BlockSpec((tb, wcol), lambda b, hh, t: (b * nt + t, hh)),
                   pl.BlockSpec((None, n_hb, HG_DK, HG_DK), lambda b, hh, t: (b, hh, 0, 0))],
        out_shape=[jax.ShapeDtypeStruct((batch * seq, D_MODEL), BF16),
                   jax.ShapeDtypeStruct((batch, HG_HEADS, HG_DK, HG_DK), F32)],
        scratch_shapes=[pltpu.VMEM((n_hb, HG_DK, HG_DK), F32)],
        compiler_params=_cparams(("arbitrary", "arbitrary", "arbitrary")),
        name="hgrn_prompt",
    )(proj, proj, proj, proj, lower.reshape(1, D_MODEL), g_norm.reshape(1, D_MODEL))


def _gla_step_kernel(qt_ref, ft_ref, i_ref, g_ref, lbt_ref, gn_ref, s_ref, o_ref, s_out_ref):
    for h in range(HG_HEADS):
        cs = slice(h * HG_DK, (h + 1) * HG_DK)
        q, logf, k = _hgrn_gates(qt_ref[:, h:h + 1], ft_ref[:, h:h + 1], lbt_ref[:, h:h + 1])
        dec = jnp.exp(logf)
        s_old = s_ref[h]
        v = i_ref[:, cs]
        s_out_ref[h] = dec * s_old + k * v
        qd = _bf16_round(jnp.broadcast_to(q * dec, s_old.shape))
        o = jnp.sum(_bf16_round(s_old) * qd, axis=0, keepdims=True) + jnp.sum(q * k, axis=0, keepdims=True) * v
        o_ref[:, cs] = _head_norm_gate(o, gn_ref[:, cs], g_ref[:, cs])


def _gla_step(proj, lower, g_norm, states, layer):
    b = proj.shape[0]
    h = HG_HEADS
    pq = proj[:, :D_MODEL].reshape(b, h, HG_DK).transpose(0, 2, 1)
    pf = proj[:, D_MODEL:2 * D_MODEL].reshape(b, h, HG_DK).transpose(0, 2, 1)
    p3 = proj.reshape(b, 1, 4 * D_MODEL)
    lbt = lower.reshape(h, HG_DK).T
    vec = lambda col: pl.BlockSpec((None, 1, D_MODEL), lambda i: (i, 0, col))
    tr = pl.BlockSpec((None, HG_DK, h), lambda i: (i, 0, 0))
    o, s_new = pl.pallas_call(
        _gla_step_kernel,
        grid=(b,),
        in_specs=[tr, tr, vec(2), vec(3),
                  pl.BlockSpec((HG_DK, h), lambda i: (0, 0)),
                  pl.BlockSpec((1, D_MODEL), lambda i: (0, 0)),
                  pl.BlockSpec((None, None, h, HG_DK, HG_DK), lambda i: (layer, i, 0, 0, 0))],
        out_specs=[pl.BlockSpec((None, 1, D_MODEL), lambda i: (i, 0, 0)),
                   pl.BlockSpec((None, h, HG_DK, HG_DK), lambda i: (i, 0, 0, 0))],
        out_shape=[jax.ShapeDtypeStruct((b, 1, D_MODEL), F32),
                   jax.ShapeDtypeStruct(states.shape[1:], F32)],
        compiler_params=_cparams(("arbitrary",)),
        name="hgrn_step",
    )(pq, pf, p3, p3, lbt, g_norm.reshape(1, D_MODEL), states)
    return o.reshape(b, D_MODEL), s_new


def _conv_kernel(bg_ref, cg_ref, u_ref, hc_ref, hu_ref, w_ref, o_ref, tail_ref, *, tm):
    t = pl.program_id(1)
    z = cg_ref[...] * u_ref[...]
    hz = hc_ref[...] * hu_ref[...]
    hz = jnp.where(t == 0, 0.0, hz)
    z1p, z2p = hz[SUBLANES - 1:SUBLANES, :], hz[SUBLANES - 2:SUBLANES - 1, :]
    row = lax.broadcasted_iota(jnp.int32, z.shape, 0)
    z1 = jnp.where(row == 0, z1p, pltpu.roll(z, 1, 0))
    z2 = jnp.where(row == 0, z2p, jnp.where(row == 1, z1p, pltpu.roll(z, 2, 0)))
    y = z2 * w_ref[0:1, :] + z1 * w_ref[1:2, :] + z * w_ref[2:3, :]
    o_ref[...] = (bg_ref[...] * y).astype(o_ref.dtype)
    tail_ref[...] = z[tm - (CONV_WIDTH - 1):, :]


def _conv_prompt(proj, w_conv, batch, seq):
    tm = 256
    nt = seq // tm
    hb = tm // SUBLANES
    blk = lambda col: pl.BlockSpec((tm, D_MODEL), lambda b, t: (b * nt + t, col))
    halo = lambda col: pl.BlockSpec(
        (SUBLANES, D_MODEL), lambda b, t: (jnp.maximum((b * nt + t) * hb - 1, 0), col))
    return pl.pallas_call(
        functools.partial(_conv_kernel, tm=tm),
        grid=(batch, nt),
        in_specs=[blk(0), blk(1), blk(2), halo(1), halo(2),
                  pl.BlockSpec((CONV_WIDTH, D_MODEL), lambda b, t: (0, 0))],
        out_specs=[pl.BlockSpec((tm, D_MODEL), lambda b, t: (b * nt + t, 0)),
                   pl.BlockSpec((None, CONV_WIDTH - 1, D_MODEL), lambda b, t: (b, 0, 0))],
        out_shape=[jax.ShapeDtypeStruct((batch * seq, D_MODEL), BF16),
                   jax.ShapeDtypeStruct((batch, CONV_WIDTH - 1, D_MODEL), F32)],
        compiler_params=_cparams(("arbitrary", "arbitrary")),
        name="conv_prompt",
    )(proj, proj, proj, proj, proj, w_conv)


def _conv_step_kernel(bg_ref, cg_ref, u_ref, buf_ref, w_ref, o_ref, nb_ref):
    z = cg_ref[...] * u_ref[...]
    buf = buf_ref[...]
    y = buf[:, 0:1, :] * w_ref[0:1, :] + buf[:, 1:2, :] * w_ref[1:2, :] + z * w_ref[2:3, :]
    o_ref[...] = bg_ref[...] * y
    nb_ref[:, 0:1, :] = buf[:, 1:2, :]
    nb_ref[:, 1:2, :] = z


def _conv_step(proj, w_conv, buf):
    b = proj.shape[0]
    p3 = proj.reshape(b, 1, 3 * D_MODEL)
    vec = lambda col: pl.BlockSpec((b, 1, D_MODEL), lambda i: (0, 0, col))
    o, nb = pl.pallas_call(
        _conv_step_kernel,
        grid=(1,),
        in_specs=[vec(0), vec(1), vec(2),
                  pl.BlockSpec((b, CONV_WIDTH - 1, D_MODEL), lambda i: (0, 0, 0)),
                  pl.BlockSpec((CONV_WIDTH, D_MODEL), lambda i: (0, 0))],
        out_specs=[pl.BlockSpec((b, 1, D_MODEL), lambda i: (0, 0, 0)),
                   pl.BlockSpec((b, CONV_WIDTH - 1, D_MODEL), lambda i: (0, 0, 0))],
        out_shape=[jax.ShapeDtypeStruct((b, 1, D_MODEL), F32),
                   jax.ShapeDtypeStruct(buf.shape, F32)],
        compiler_params=_cparams(("arbitrary",)),
        name="conv_step",
    )(p3, p3, p3, buf, w_conv)
    return o.reshape(b, D_MODEL), nb


def _swa_kernel(*refs, dil, n_units, n_heads, has_prev):
    if has_prev:
        qc_ref, kp_ref, kc_ref, vp_ref, vc_ref, o_ref, l_ref = refs
    else:
        qc_ref, kc_ref, vc_ref, o_ref, l_ref = refs
    blk = SW_BAND
    n = pl.program_id(1)
    ri = lax.broadcasted_iota(jnp.int32, (blk, blk), 0)
    ci = lax.broadcasted_iota(jnp.int32, (blk, blk), 1)
    m_cur = ci <= ri
    m_prev_tri = ci >= ri
    lane = lax.broadcasted_iota(jnp.int32, (blk, LANES), 1)
    scale = SW_HEAD_DIM ** -0.5
    nt = (((1,), (1,)), ((), ()))

    for u in range(n_units):
        if dil == 1:
            rows = pl.ds(u * blk, blk)
            prev_src = None
            if has_prev:
                prev_src = (kp_ref, vp_ref, pl.ds(0, blk), True) if u == 0 else \
                    (kc_ref, vc_ref, pl.ds((u - 1) * blk, blk), False)
        else:
            rows = pl.ds(u, blk, stride=dil)
            prev_src = (kp_ref, vp_ref, rows, True) if has_prev else None

        def head(h, lse_acc, rows=rows, prev_src=prev_src):
            cs = pl.ds(0, SW_HEAD_DIM) if n_heads == 1 else \
                pl.ds(pl.multiple_of(h * SW_HEAD_DIM, SW_HEAD_DIM), SW_HEAD_DIM)
            q = qc_ref[rows, cs].astype(BF16)
            s_c = lax.dot_general(q, kc_ref[rows, cs].astype(BF16), nt, preferred_element_type=F32) * scale
            s_c = jnp.where(m_cur, s_c, MASK_VALUE)
            m = jnp.max(s_c, axis=-1, keepdims=True)
            if prev_src is not None:
                kr, vr, prow, first_only = prev_src
                s_p = lax.dot_general(q, kr[prow, cs].astype(BF16), nt, preferred_element_type=F32) * scale
                m_prev = (m_prev_tri & (n > 0)) if first_only else m_prev_tri
                s_p = jnp.where(m_prev, s_p, MASK_VALUE)
                m = jnp.maximum(m, jnp.max(s_p, axis=-1, keepdims=True))
            p_c = jnp.exp(s_c - m)
            l = jnp.sum(p_c, axis=-1, keepdims=True)
            o = jnp.dot(p_c.astype(BF16), vc_ref[rows, cs].astype(BF16), preferred_element_type=F32)
            if prev_src is not None:
                p_p = jnp.exp(s_p - m)
                l = l + jnp.sum(p_p, axis=-1, keepdims=True)
                o = o + jnp.dot(p_p.astype(BF16), vr[prow, cs].astype(BF16), preferred_element_type=F32)
            o_ref[rows, cs] = o / l
            lse = m + jnp.log(l)
            return jnp.where(lane == h, lse, lse_acc)

        lse0 = jnp.zeros((blk, LANES), F32)
        l_ref[rows, :] = head(0, lse0) if n_heads == 1 else lax.fori_loop(0, n_heads, head, lse0, unroll=2)


def _swa_prompt_group(qkv, gi, dil, batch, seq):
    span = SW_BAND * dil
    if dil == 1:
        span, n_units = 512, 4
    else:
        n_units = dil
    nb = seq // span
    has_prev = nb > 1
    hb = SW_HEADS if dil == 1 else 1
    n_hblk = SW_HEADS // hb
    wcol = hb * SW_HEAD_DIM
    per = SW_WIDTH // wcol
    pspan = SW_BAND if dil == 1 else span
    pmul = span // pspan

    def cur(which):
        return pl.BlockSpec((span, wcol), lambda b, n, hh: (b * nb + n, (gi * 3 + which) * per + hh))

    def prev(which):
        return pl.BlockSpec(
            (pspan, wcol),
            lambda b, n, hh: (jnp.maximum((b * nb + n) * pmul - 1, 0), (gi * 3 + which) * per + hh))

    if has_prev:
        in_specs = [cur(0), prev(1), cur(1), prev(2), cur(2)]
        args = [qkv] * 5
    else:
        in_specs = [cur(0), cur(1), cur(2)]
        args = [qkv] * 3
    return pl.pallas_call(
        functools.partial(_swa_kernel, dil=dil, n_units=n_units, n_heads=hb, has_prev=has_prev),
        grid=(batch, nb, n_hblk),
        in_specs=in_specs,
        out_specs=[pl.BlockSpec((span, wcol), lambda b, n, hh: (b * nb + n, hh)),
                   pl.BlockSpec((span, LANES), lambda b, n, hh: (b * nb + n, hh))],
        out_shape=[jax.ShapeDtypeStruct((batch * seq, SW_WIDTH), F32),
                   jax.ShapeDtypeStruct((batch * seq, n_hblk * LANES), F32)],
        compiler_params=_cparams(("arbitrary", "arbitrary", "arbitrary")),
        name=f"swa_prompt_g{gi}",
    )(*args)


def _merge_heads(o_refs, l_refs, rs, hbs):
    outs = []
    for h in range(SW_HEADS):
        cs = slice(h * SW_HEAD_DIM, (h + 1) * SW_HEAD_DIM)
        lses = []
        for l_ref, hb in zip(l_refs, hbs):
            lane = (h // hb) * LANES + h % hb
            lses.append(l_ref[rs, lane:lane + 1])
        mx = jnp.maximum(jnp.maximum(lses[0], lses[1]), lses[2])
        es = [jnp.exp(l - mx) for l in lses]
        den = es[0] + es[1] + es[2]
        acc = None
        for e, o_ref in zip(es, o_refs):
            o = o_ref[rs, cs]
            term = _bf16_round(jnp.broadcast_to(e / den, o.shape)) * _bf16_round(o)
            acc = term if acc is None else acc + term
        outs.append(acc)
    return jnp.concatenate(outs, axis=1)


def _swa_merge_kernel(o1, o2, o3, l1, l2, l3, out_ref, *, tm, hbs):
    ch = 128

    def body(c, carry):
        rs = pl.ds(pl.multiple_of(c * ch, ch), ch)
        out_ref[rs, :] = _merge_heads((o1, o2, o3), (l1, l2, l3), rs, hbs).astype(out_ref.dtype)
        return carry

    lax.fori_loop(0, tm // ch, body, 0)


def _swa_merge(os_, ls_):
    tm = 512
    m = os_[0].shape[0]
    hbs = tuple(SW_HEADS // (l.shape[1] // LANES) for l in ls_)
    row = lambda width: pl.BlockSpec((tm, width), lambda i: (i, 0))
    return pl.pallas_call(
        functools.partial(_swa_merge_kernel, tm=tm, hbs=hbs),
        grid=(m // tm,),
        in_specs=[row(SW_WIDTH)] * 3 + [row(l.shape[1]) for l in ls_],
        out_specs=row(SW_WIDTH),
        out_shape=jax.ShapeDtypeStruct((m, SW_WIDTH), BF16),
        compiler_params=_cparams(("arbitrary",)),
        name="swa_merge",
    )(*os_, *ls_)


def _swa_step_kernel(qkv_ref, c1_ref, c2_ref, c3_ref, o_ref):
    scale = SW_HEAD_DIM ** -0.5
    outs, lses = [], []
    for gi, c_ref in enumerate((c1_ref, c2_ref, c3_ref)):
        q = _bf16_round(qkv_ref[gi, 0])
        kn = _bf16_round(qkv_ref[gi, 1])
        vn = _bf16_round(qkv_ref[gi, 2])
        kc = _bf16_round(c_ref[:, 0])
        vc = _bf16_round(c_ref[:, 1])
        s = jnp.sum(kc * q[None], axis=-1, keepdims=True) * scale
        sn = jnp.sum(kn * q, axis=-1, keepdims=True) * scale
        m = jnp.maximum(jnp.max(s, axis=0), sn)
        lse = m + jnp.log(jnp.sum(jnp.exp(s - m[None]), axis=0) + jnp.exp(sn - m))
        shape = vc.shape
        p = _bf16_round(jnp.broadcast_to(jnp.exp(s - lse[None]), shape))
        pn = _bf16_round(jnp.broadcast_to(jnp.exp(sn - lse), shape[1:]))
        outs.append(jnp.sum(p * vc, axis=0) + pn * vn)
        lses.append(lse)
    mx = jnp.maximum(jnp.maximum(lses[0], lses[1]), lses[2])
    es = [jnp.exp(l - mx) for l in lses]
    den = es[0] + es[1] + es[2]
    acc = None
    for e, o in zip(es, outs):
        term = _bf16_round(jnp.broadcast_to(e / den, o.shape)) * _bf16_round(o)
        acc = term if acc is None else acc + term
    o_ref[...] = acc


def _swa_step(qkv, caches):
    b = qkv.shape[0]
    q5 = qkv.reshape(b, len(SW_GROUPS), 3, SW_HEADS, SW_HEAD_DIM)
    views, specs = [], []
    for c, (win, dil) in zip(caches, SW_GROUPS):
        views.append(c.reshape(b, win // dil, dil, 2, SW_HEADS, SW_HEAD_DIM))
        specs.append(pl.BlockSpec((None, win // dil, None, 2, SW_HEADS, SW_HEAD_DIM),
                                  lambda i: (i, 0, 0, 0, 0, 0)))
    o = pl.pallas_call(
        _swa_step_kernel,
        grid=(b,),
        in_specs=[pl.BlockSpec((None, len(SW_GROUPS), 3, SW_HEADS, SW_HEAD_DIM),
                               lambda i: (i, 0, 0, 0, 0))] + specs,
        out_specs=pl.BlockSpec((None, SW_HEADS, SW_HEAD_DIM), lambda i: (i, 0, 0)),
        out_shape=jax.ShapeDtypeStruct((b, SW_HEADS, SW_HEAD_DIM), F32),
        compiler_params=_cparams(("arbitrary",)),
        name="swa_step",
    )(q5, *views)
    return o.reshape(b, SW_WIDTH)


def _cache_shift_kernel(cur_ref, nxt_ref, knew_ref, vnew_ref, o_ref, *, wb):
    k = pl.program_id(1)
    last = pl.num_programs(1) - 1

    def row(i, carry):
        o_ref[i] = cur_ref[i + 1]
        return carry

    lax.fori_loop(0, wb - 1, row, 0, unroll=8)

    @pl.when(k < last)
    def _():
        o_ref[wb - 1] = nxt_ref[0]

    @pl.when(k == last)
    def _():
        o_ref[wb - 1, 0] = knew_ref[...]
        o_ref[wb - 1, 1] = vnew_ref[...]


def _cache_shift(qkv, caches):
    b = qkv.shape[0]
    q5 = qkv.reshape(b, len(SW_GROUPS), 3, SW_HEADS, SW_HEAD_DIM)
    outs = []
    for gi, c in enumerate(caches):
        w = c.shape[1]
        wb = min(w, 512)
        tail = (2, SW_HEADS, SW_HEAD_DIM)
        outs.append(pl.pallas_call(
            functools.partial(_cache_shift_kernel, wb=wb),
            grid=(b, w // wb),
            in_specs=[
                pl.BlockSpec((None, wb) + tail, lambda i, k: (i, k, 0, 0, 0)),
                pl.BlockSpec((None, 1) + tail, lambda i, k: (i, jnp.minimum((k + 1) * wb, w - 1), 0, 0, 0)),
                pl.BlockSpec((None, None, None, SW_HEADS, SW_HEAD_DIM), lambda i, k: (i, gi, 1, 0, 0)),
                pl.BlockSpec((None, None, None, SW_HEADS, SW_HEAD_DIM), lambda i, k: (i, gi, 2, 0, 0)),
            ],
            out_specs=pl.BlockSpec((None, wb) + tail, lambda i, k: (i, k, 0, 0, 0)),
            out_shape=jax.ShapeDtypeStruct(c.shape, F32),
            compiler_params=_cparams(("arbitrary", "arbitrary")),
            name=f"cache_shift_g{gi}",
        )(c, c, q5, q5))
    return outs


def _kv_window_kernel(k_ref, v_ref, o_ref, *, rows):
    for kv, src in enumerate((k_ref, v_ref)):
        for h in range(SW_HEADS):
            o_ref[pl.ds(kv * SW_HEADS + h, rows, stride=2 * SW_HEADS), :] = \
                src[:, h * SW_HEAD_DIM:(h + 1) * SW_HEAD_DIM]


def _kv_window(qkv, gi, win, batch, seq):
    wlen = min(win, seq)
    rows = 128
    nt = wlen // rows
    per_row = 2 * SW_HEADS
    src = lambda which: pl.BlockSpec(
        (rows, SW_WIDTH), lambda b, t: ((b * seq + seq - wlen) // rows + t, gi * 3 + which))
    out = pl.pallas_call(
        functools.partial(_kv_window_kernel, rows=rows),
        grid=(batch, nt),
        in_specs=[src(1), src(2)],
        out_specs=pl.BlockSpec((None, rows * per_row, SW_HEAD_DIM), lambda b, t: (b, t, 0)),
        out_shape=jax.ShapeDtypeStruct((batch, wlen * per_row, SW_HEAD_DIM), F32),
        compiler_params=_cparams(("arbitrary", "arbitrary")),
        name=f"kv_window_g{gi}",
    )(qkv, qkv)
    return out.reshape(batch, wlen, 2, SW_HEADS, SW_HEAD_DIM)


def _router_kernel(x_ref, gn_ref, sc_ref, sh_ref, wr_ref, br_ref, *rest, tm):
    h_ref, ids_ref, gates_ref, hf_ref = rest[-4:]
    _norm_mod_into(x_ref, gn_ref, sc_ref, sh_ref, [hf_ref], tm)
    for s in range(ROW_TILES):
        h_ref[pl.ds(s, tm, stride=ROW_TILES), :] = hf_ref[:, s * LANES:(s + 1) * LANES]
    logits = jnp.dot(hf_ref[...].astype(BF16), wr_ref[...].astype(BF16),
                     preferred_element_type=F32) + br_ref[...]
    lane = lax.broadcasted_iota(jnp.int32, logits.shape, 1)
    big = jnp.int32(1 << 20)
    is_g = (lane >= MOE_EXPERTS) & (lane < MOE_EXPERTS + MOE_GROUPS)
    glog = jnp.where(is_g, logits, -jnp.inf)
    gmax = jnp.max(glog, axis=-1, keepdims=True)
    gsel = jnp.min(jnp.where(glog == gmax, lane - MOE_EXPERTS, big), axis=-1, keepdims=True)
    gsum = jnp.sum(jnp.where(is_g, jnp.exp(glog - gmax), 0.0), axis=-1, keepdims=True)
    pg = 1.0 / gsum
    in_grp = (lane < MOE_EXPERTS) & ((lane // MOE_PER_GROUP) == gsel)
    el = jnp.where(in_grp, logits, -jnp.inf)
    v1 = jnp.max(el, axis=-1, keepdims=True)
    i1 = jnp.min(jnp.where(el == v1, lane, big), axis=-1, keepdims=True)
    el2 = jnp.where(lane == i1, -jnp.inf, el)
    v2 = jnp.max(el2, axis=-1, keepdims=True)
    i2 = jnp.min(jnp.where(el2 == v2, lane, big), axis=-1, keepdims=True)
    e2 = jnp.exp(v2 - v1)
    den = 1.0 + e2
    ids_ref[...] = jnp.where(lane == 0, i1, jnp.where(lane == 1, i2, 0))
    gates_ref[...] = jnp.where(lane == 0, pg * (1.0 / den), jnp.where(lane == 1, pg * (e2 / den), 0.0))


def _router(x, gn, mod, wr, br, hcat, n_tok, *, tm, rows_per_batch, row0):
    m = x.shape[0]
    blk0 = row0 // tm
    bpb = max(rows_per_batch // tm, 1)
    nb = m // tm
    n_steps = nb + (1 if hcat is None and n_tok > m else 0)
    cl = lambda i: jnp.minimum(i, nb - 1)
    msp = lambda col: pl.BlockSpec((None, mod.shape[1], D_MODEL), lambda i: (cl(i) // bpb, 0, col))
    in_specs = [
        pl.BlockSpec((tm, D_MODEL), lambda i: (cl(i), 0)),
        pl.BlockSpec((1, D_MODEL), lambda i: (0, 0)),
        msp(4), msp(3),
        pl.BlockSpec((D_MODEL, LANES), lambda i: (0, 0)),
        pl.BlockSpec((1, LANES), lambda i: (0, 0)),
    ]
    args = [x, gn.reshape(1, D_MODEL), mod, mod, wr, br]
    aliases = {}
    if hcat is not None:
        in_specs.append(pl.BlockSpec(memory_space=pl.ANY))
        args.append(hcat)
        aliases = {len(args) - 1: 0}
    return pl.pallas_call(
        functools.partial(_router_kernel, tm=tm),
        grid=(n_steps,),
        in_specs=in_specs,
        out_specs=[pl.BlockSpec((tm * ROW_TILES, LANES), lambda i: (blk0 + i, 0)),
                   pl.BlockSpec((tm, LANES), lambda i: (cl(i), 0)),
                   pl.BlockSpec((tm, LANES), lambda i: (cl(i), 0))],
        out_shape=[jax.ShapeDtypeStruct((n_tok * ROW_TILES, LANES), F32),
                   jax.ShapeDtypeStruct((m, LANES), jnp.int32),
                   jax.ShapeDtypeStruct((m, LANES), F32)],
        scratch_shapes=[pltpu.VMEM((tm, D_MODEL), F32)],
        input_output_aliases=aliases,
        compiler_params=_cparams(("arbitrary",)),
        name="moe_router",
    )(*args)


def _experts_kernel(tok_ref, run_ref, ng_ref, qlo_ref, qhi_ref, ce_ref, ck_ref, cs_ref, meta_ref,
                    h_hbm, wgu_hbm, wd_hbm, y_ref,
                    rows_ref, stg_ref, wgu_ref, wd_ref, row_sem, w_sem, *, layer):
    b = pl.program_id(0)
    n_used, q_total = meta_ref[0], meta_ref[1]
    rt = ROW_TILES
    grp = DMA_UNROLL * rt

    def issue_rows(blk, slot):
        def body(g, carry):
            for u in range(DMA_UNROLL):
                r = g * DMA_UNROLL + u
                tok = tok_ref[blk * MOE_ROWS + r]
                pltpu.make_async_copy(h_hbm.at[pl.ds(pl.multiple_of(tok * rt, rt), rt)],
                                      rows_ref.at[slot, pl.ds(pl.multiple_of(r * rt, rt), rt)],
                                      row_sem.at[slot]).start()
            return carry

        lax.fori_loop(0, ng_ref[blk], body, 0)

    def wait_rows(blk, slot):
        def body(g, carry):
            pltpu.make_async_copy(h_hbm.at[pl.ds(0, grp)], rows_ref.at[slot, pl.ds(0, grp)], row_sem.at[slot]).wait()
            return carry

        lax.fori_loop(0, ng_ref[blk], body, 0)

    def chunk_start(q):
        e, k, st = ce_ref[q], ck_ref[q], q % W_STAGES
        row0 = lambda kk: pl.ds(pl.multiple_of(kk * W_CHUNK, W_CHUNK), W_CHUNK)

        @pl.when(k < GU_CHUNKS)
        def _():
            pltpu.make_async_copy(wgu_hbm.at[layer, e, row0(k)], stg_ref.at[st], w_sem.at[st]).start(priority=1)

        @pl.when(k >= GU_CHUNKS)
        def _():
            pltpu.make_async_copy(wd_hbm.at[layer, e, row0(k - GU_CHUNKS)], stg_ref.at[st],
                                  w_sem.at[st]).start(priority=1)

    def chunk_process(q):
        k, slot, st = ck_ref[q], cs_ref[q], q % W_STAGES
        pltpu.make_async_copy(wgu_hbm.at[layer, 0, pl.ds(0, W_CHUNK)], stg_ref.at[st], w_sem.at[st]).wait()
        sub = 32

        @pl.when(k < GU_CHUNKS)
        def _():
            for i in range(W_CHUNK // sub):
                dst = pl.ds(pl.multiple_of(k * W_CHUNK + i * sub, sub), sub)
                wgu_ref[slot, dst, :] = stg_ref[st, pl.ds(i * sub, sub), :].astype(BF16)

        @pl.when(k >= GU_CHUNKS)
        def _():
            for i in range(W_CHUNK // sub):
                dst = pl.ds(pl.multiple_of((k - GU_CHUNKS) * W_CHUNK + i * sub, sub), sub)
                wd_ref[slot, dst, :] = stg_ref[st, pl.ds(i * sub, sub), :].astype(BF16)

        @pl.when(q + W_STAGES < q_total)
        def _():
            chunk_start(q + W_STAGES)

    def process_range(lo, hi):
        def body(q, carry):
            chunk_process(q)
            return carry

        lax.fori_loop(lo, hi, body, 0)

    @pl.when(b == 0)
    def _():
        rows_ref[...] = jnp.zeros_like(rows_ref)
        issue_rows(0, 0)
        for q in range(W_STAGES):
            chunk_start(q)
        process_range(0, RUN_CHUNKS)

    @pl.when(b + 1 < n_used)
    def _():
        issue_rows(b + 1, (b + 1) % 2)

    @pl.when(b < n_used)
    def _():
        rslot = b % 2
        q_lo, q_hi = qlo_ref[b], qhi_ref[b]
        q_mid = (q_lo + q_hi) // 2
        process_range(q_lo, q_mid)
        wait_rows(b, rslot)
        x = jnp.concatenate(
            [rows_ref[rslot, pl.ds(s, MOE_ROWS, stride=rt), :].astype(BF16) for s in range(rt)], axis=1)
        wslot = run_ref[b] % 2
        for sl in range(2):
            @pl.when(wslot == sl)
            def _():
                au = jnp.dot(x, wgu_ref[sl], preferred_element_type=F32)
                hmid = (_silu(au[:, :MOE_FF]) * au[:, MOE_FF:]).astype(BF16)
                y = jnp.dot(hmid, wd_ref[sl], preferred_element_type=F32)
                for s in range(rt):
                    y_ref[pl.ds(s, MOE_ROWS, stride=rt), :] = y[:, s * LANES:(s + 1) * LANES]

        process_range(q_mid, q_hi)

    @pl.when(b >= n_used)
    def _():
        y_ref[...] = jnp.zeros_like(y_ref)


def _moe_experts(hcat, w_gu, w_down, layer, row_tok, blk_run, blk_groups, qlo, qhi, chunk_e, chunk_k, chunk_s, meta,
                 n_blocks):
    any_spec = pl.BlockSpec(memory_space=pl.ANY)
    return pl.pallas_call(
        functools.partial(_experts_kernel, layer=layer),
        grid_spec=pltpu.PrefetchScalarGridSpec(
            num_scalar_prefetch=9,
            grid=(n_blocks,),
            in_specs=[any_spec, any_spec, any_spec],
            out_specs=pl.BlockSpec((MOE_ROWS * ROW_TILES, LANES), lambda b, *_: (b, 0)),
            scratch_shapes=[pltpu.VMEM((2, MOE_ROWS * ROW_TILES, LANES), F32),
                            pltpu.VMEM((W_STAGES, W_CHUNK, D_MODEL), F32),
                            pltpu.VMEM((2, D_MODEL, 2 * MOE_FF), BF16),
                            pltpu.VMEM((2, MOE_FF, D_MODEL), BF16),
                            pltpu.SemaphoreType.DMA((2,)),
                            pltpu.SemaphoreType.DMA((W_STAGES,))],
        ),
        out_shape=jax.ShapeDtypeStruct((n_blocks * MOE_ROWS * ROW_TILES, LANES), F32),
        compiler_params=pltpu.CompilerParams(dimension_semantics=("arbitrary",), vmem_limit_bytes=VMEM_LIMIT,
                                             disable_bounds_checks=True),
        name="moe_experts",
    )(row_tok, blk_run, blk_groups, qlo, qhi, chunk_e, chunk_k, chunk_s, meta, hcat, w_gu, w_down)


def _combine_kernel(dest_ref, y_hbm, x_ref, g_ref, gates_ref, o_ref, y0_ref, y1_ref, sem, *, tm, tok0):
    i = pl.program_id(0)
    rt = ROW_TILES

    def issue(tile, slot):
        def body(g, carry):
            for u in range(DMA_UNROLL // 2):
                r = g * (DMA_UNROLL // 2) + u
                a = (tok0 + tile * tm + r) * MOE_TOP_K
                for j, y_ref in enumerate((y0_ref, y1_ref)):
                    pltpu.make_async_copy(y_hbm.at[pl.ds(pl.multiple_of(dest_ref[a + j] * rt, rt), rt)],
                                          y_ref.at[slot, pl.ds(pl.multiple_of(r * rt, rt), rt)],
                                          sem.at[slot]).start(priority=j)
            return carry

        lax.fori_loop(0, tm // (DMA_UNROLL // 2), body, 0)

    @pl.when(i == 0)
    def _():
        issue(0, 0)

    @pl.when(i + 1 < pl.num_programs(0))
    def _():
        issue(i + 1, (i + 1) % 2)

    slot = i % 2
    for y_ref in (y0_ref, y1_ref):
        pltpu.make_async_copy(y_hbm.at[pl.ds(0, tm * rt)], y_ref.at[slot], sem.at[slot]).wait()
    gt = gates_ref[...]
    g0 = jnp.broadcast_to(gt[:, 0:1], (tm, LANES))
    g1 = jnp.broadcast_to(gt[:, 1:2], (tm, LANES))
    for s in range(rt):
        cs = slice(s * LANES, (s + 1) * LANES)
        rows = pl.ds(s, tm, stride=rt)
        y = y0_ref[slot, rows, :] * g0 + y1_ref[slot, rows, :] * g1
        o_ref[:, cs] = x_ref[:, cs] + g_ref[:, cs] * y


def _moe_combine(y, dest, x, mod, gates, *, tm, rows_per_batch, tok0):
    m = x.shape[0]
    bpb = max(rows_per_batch // tm, 1)
    return pl.pallas_call(
        functools.partial(_combine_kernel, tm=tm, tok0=tok0),
        grid_spec=pltpu.PrefetchScalarGridSpec(
            num_scalar_prefetch=1,
            grid=(m // tm,),
            in_specs=[pl.BlockSpec(memory_space=pl.ANY),
                      pl.BlockSpec((tm, D_MODEL), lambda i, d: (i, 0)),
                      pl.BlockSpec((None, mod.shape[1], D_MODEL), lambda i, d: (i // bpb, 0, 5)),
                      pl.BlockSpec((tm, LANES), lambda i, d: (i, 0))],
            out_specs=pl.BlockSpec((tm, D_MODEL), lambda i, d: (i, 0)),
            scratch_shapes=[pltpu.VMEM((2, tm * ROW_TILES, LANES), F32)] * 2
            + [pltpu.SemaphoreType.DMA((2,))],
        ),
        out_shape=jax.ShapeDtypeStruct((m, D_MODEL), F32),
        compiler_params=pltpu.CompilerParams(dimension_semantics=("arbitrary",), vmem_limit_bytes=VMEM_LIMIT,
                                             disable_bounds_checks=True),
        name="moe_combine",
    )(dest, y, x, mod, gates)


def _moe_layer(xp, xs, mod_p, mod_s, gn, w_group, b_group, w_router, b_router, w_gu, w_down, layer, seq):
    n_p, n_s = xp.shape[0], xs.shape[0]
    n_tok = n_p + n_s
    pad = LANES - MOE_EXPERTS - MOE_GROUPS
    wr = jnp.concatenate([w_router, w_group, jnp.zeros((D_MODEL, pad), F32)], axis=1)
    br = jnp.concatenate([b_router, b_group, jnp.zeros((pad,), F32)]).reshape(1, LANES)
    hcat, ids_p, gates_p = _router(xp, gn, mod_p, wr, br, None, n_tok, tm=256, rows_per_batch=seq, row0=0)
    hcat, ids_s, gates_s = _router(xs, gn, mod_s, wr, br, hcat, n_tok, tm=n_s, rows_per_batch=n_s, row0=n_p)

    expert = jnp.concatenate([ids_p[:, :MOE_TOP_K], ids_s[:, :MOE_TOP_K]], axis=0).reshape(-1)
    n_assign = n_tok * MOE_TOP_K
    n_blocks = -(-n_assign // MOE_ROWS) + MOE_EXPERTS
    onehot = (expert[:, None] == jnp.arange(MOE_EXPERTS, dtype=jnp.int32)[None, :]).astype(jnp.int32)
    csum = jnp.cumsum(onehot, axis=0)
    rank = jnp.sum(csum * onehot, axis=1) - 1
    counts = csum[-1]
    padded = (counts + MOE_ROWS - 1) // MOE_ROWS * MOE_ROWS
    pend = jnp.cumsum(padded)
    pstart = pend - padded
    dest = (pstart[expert] + rank).astype(jnp.int32)
    tok = jnp.arange(n_assign, dtype=jnp.int32) // MOE_TOP_K
    row_tok = jnp.zeros((n_blocks * MOE_ROWS,), jnp.int32).at[dest].set(tok, unique_indices=True)
    blk_ids = jnp.arange(n_blocks, dtype=jnp.int32)
    blk_expert = jnp.minimum(jnp.searchsorted(pend, blk_ids * MOE_ROWS, side="right"),
                             MOE_EXPERTS - 1).astype(jnp.int32)
    n_used = (pend[-1:] // MOE_ROWS).astype(jnp.int32)
    nonempty = counts > 0
    n_runs = jnp.sum(nonempty.astype(jnp.int32))
    run_of_expert = jnp.cumsum(nonempty.astype(jnp.int32)) - 1
    run_expert = jnp.argsort(jnp.where(nonempty, 0, 1), stable=True).astype(jnp.int32)
    blk_run = run_of_expert[blk_expert].astype(jnp.int32)
    j_in_run = blk_ids - pstart[blk_expert] // MOE_ROWS
    n_in_run = jnp.maximum(padded[blk_expert] // MOE_ROWS, 1)
    q_total = n_runs * RUN_CHUNKS
    streams = (blk_ids < n_used[0]) & (blk_run + 1 < n_runs)
    base = (blk_run + 1) * RUN_CHUNKS
    qlo = jnp.where(streams, base + (RUN_CHUNKS * j_in_run) // n_in_run, q_total).astype(jnp.int32)
    qhi = jnp.where(streams, base + (RUN_CHUNKS * (j_in_run + 1)) // n_in_run, q_total).astype(jnp.int32)
    chunk_ids = jnp.arange(MOE_EXPERTS * RUN_CHUNKS, dtype=jnp.int32)
    chunk_e = run_expert[chunk_ids // RUN_CHUNKS]
    chunk_k = chunk_ids % RUN_CHUNKS
    chunk_s = (chunk_ids // RUN_CHUNKS) % 2
    meta = jnp.concatenate([n_used, q_total[None].astype(jnp.int32)])
    valid_rows = jnp.clip(counts[blk_expert] - j_in_run * MOE_ROWS, 0, MOE_ROWS)
    blk_groups = jnp.where(blk_ids < n_used[0], (valid_rows + DMA_UNROLL - 1) // DMA_UNROLL, 0).astype(jnp.int32)

    y = _moe_experts(hcat, w_gu, w_down, layer, row_tok, blk_run, blk_groups, qlo, qhi, chunk_e, chunk_k, chunk_s,
                     meta, n_blocks)
    xp_new = _moe_combine(y, dest, xp, mod_p, gates_p, tm=256, rows_per_batch=seq, tok0=0)
    xs_new = _moe_combine(y, dest, xs, mod_s, gates_s, tm=n_s, rows_per_batch=n_s, tok0=n_p)
    return xp_new, xs_new


def _final_norm_kernel(x_ref, g_ref, o_ref):
    x = x_ref[...]
    inv = lax.rsqrt(jnp.mean(x * x, axis=-1, keepdims=True) + EPS)
    o_ref[...] = (x * inv) * g_ref[...]


def _final_norm(x, g, tm):
    m = x.shape[0]
    return pl.pallas_call(
        _final_norm_kernel,
        grid=(m // tm,),
        in_specs=[pl.BlockSpec((tm, D_MODEL), lambda i: (i, 0)), pl.BlockSpec((1, D_MODEL), lambda i: (0, 0))],
        out_specs=pl.BlockSpec((tm, D_MODEL), lambda i: (i, 0)),
        out_shape=jax.ShapeDtypeStruct((m, D_MODEL), F32),
        compiler_params=_cparams(("arbitrary",)),
        name="final_norm",
    )(x, g.reshape(1, D_MODEL))


def _rope_tables(pos):
    half = SW_HEAD_DIM // 2
    inv_freq = ROPE_THETA ** (-jnp.arange(half, dtype=F32) / half)
    ang = pos.astype(F32)[:, None] * inv_freq[None, :]
    cos, sin = jnp.cos(ang), jnp.sin(ang)
    return jnp.concatenate([cos, cos], axis=1), jnp.concatenate([-sin, sin], axis=1)


def kernel(x_prompt, x_sample, state_hgrn, state_conv, cache_swa_g1, cache_swa_g2, cache_swa_g3, c_prompt, c_sample, ada_w, ada_b, norm_mix, norm_ffn, norm_final, hg_w_in, hg_w_out, hg_norm, hg_lower, cv_w_in, cv_w_conv, cv_w_out, sw_w_in, sw_w_out, moe_w_group, moe_b_group, moe_w_router, moe_b_router, moe_w_gu, moe_w_down):
    bp, seq, d = x_prompt.shape
    bs = x_sample.shape[0]
    n_p = bp * seq
    xp = x_prompt.reshape(n_p, d)
    xs = x_sample.reshape(bs, d)

    mod_all = _ada_mod(jnp.concatenate([c_prompt, c_sample], axis=0), ada_w, ada_b)
    sm = jax.nn.softmax(hg_lower.astype(F32), axis=0)
    lower = jnp.cumsum(sm, axis=0) - sm[0]
    rope_p = _rope_tables(jnp.arange(seq))
    rope_s = _rope_tables(jnp.full((bs,), PAST_LEN, jnp.int32))

    tm_p, tn_p = 1024, 512
    hg_p, hg_s, cv_p, cv_s = [], [], [], []
    sw_p, sw_s = None, None
    for i in range(DEPTH):
        kind, j = i % N_MIXERS, i // N_MIXERS
        mod_p = mod_all[i, :bp].reshape(bp, 1, 6 * d)
        mod_s = mod_all[i, bp:].reshape(1, bs, 6 * d)
        pp = dict(tm=tm_p, tn=tn_p)
        pp_in = dict(tm=tm_p, tn=1024)
        ps = dict(tm=bs, tn=1024)
        hp = _norm_mod(xp, norm_mix[i], mod_p, 0, 1, tm=512, rows_per_batch=seq)
        hs = _norm_mod(xs, norm_mix[i], mod_s, 0, 1, tm=bs, rows_per_batch=bs)
        if kind == 0:
            proj_p = _mm(hp, hg_w_in, j, **pp_in)
            proj_s = _mm(hs, hg_w_in, j, **ps)
            lhs_p, st_p = _gla_prompt(proj_p, lower[j], hg_norm[j], bp, seq)
            lhs_s, st_s = _gla_step(proj_s, lower[j], hg_norm[j], state_hgrn, j)
            hg_p.append(st_p)
            hg_s.append(st_s)
            w_out = hg_w_out
        elif kind == 1:
            proj_p = _mm(hp, cv_w_in, j, **pp_in)
            proj_s = _mm(hs, cv_w_in, j, **ps)
            lhs_p, tail_p = _conv_prompt(proj_p, cv_w_conv[j], bp, seq)
            lhs_s, tail_s = _conv_step(proj_s, cv_w_conv[j], state_conv[j])
            cv_p.append(tail_p)
            cv_s.append(tail_s)
            w_out = cv_w_out
        else:
            qkv_p = _mm(hp, sw_w_in, j, tm=512, tn=SW_WIDTH, rope=rope_p)
            qkv_s = _mm(hs, sw_w_in, j, rope=rope_s, **ps)
            os_, ls_ = [], []
            for gi, (win, dil) in enumerate(SW_GROUPS):
                o_g, l_g = _swa_prompt_group(qkv_p, gi, dil, bp, seq)
                os_.append(o_g)
                ls_.append(l_g)
            lhs_p = _swa_merge(os_, ls_)
            caches = (cache_swa_g1[j], cache_swa_g2[j], cache_swa_g3[j])
            lhs_s = _swa_step(qkv_s, caches)
            w_out = sw_w_out
            sw_p = [_kv_window(qkv_p, gi, win, bp, seq)[None] for gi, (win, _) in enumerate(SW_GROUPS)]
            sw_s = [c[None] for c in _cache_shift(qkv_s, caches)]
        xp = _mm(lhs_p, w_out, j, resgate=(xp, mod_p, 2, seq), **pp)
        xs = _mm(lhs_s, w_out, j, resgate=(xs, mod_s, 2, bs), **ps)
        xp, xs = _moe_layer(xp, xs, mod_p, mod_s, norm_ffn[i], moe_w_group[i], moe_b_group[i],
                            moe_w_router[i], moe_b_router[i], moe_w_gu, moe_w_down, i, seq)

    y_p = _final_norm(xp, norm_final, 512).reshape(bp, seq, d)
    y_s = _final_norm(xs, norm_final, bs).reshape(bs, 1, d)
    return (y_p, y_s, jnp.stack(hg_p), jnp.stack(hg_s), jnp.stack(cv_p), jnp.stack(cv_s),
            sw_p[0], sw_s[0], sw_p[1], sw_s[1], sw_p[2], sw_s[2])
```

```python
import functools

import jax
import jax.numpy as jnp
from jax import lax
from jax.experimental import pallas as pl
from jax.experimental.pallas import tpu as pltpu

F32 = jnp.float32
BF16 = jnp.bfloat16

D_MODEL = 2048
DEPTH = 4
N_MIXERS = 3
EPS = 1e-6
MASK_VALUE = -1e30
F_FLOOR = 1e-30
HG_DK = 128
HG_HEADS = D_MODEL // HG_DK
HG_CHUNK = 64
HG_SUB = 16
CONV_WIDTH = 3
SW_GROUPS = ((128, 1), (512, 4), (2048, 16))
SW_HEADS = 8
SW_HEAD_DIM = 128
SW_WIDTH = SW_HEADS * SW_HEAD_DIM
SW_BAND = 128
ROPE_THETA = 10000.0
MOE_GROUPS = 4
MOE_PER_GROUP = 8
MOE_EXPERTS = MOE_GROUPS * MOE_PER_GROUP
MOE_TOP_K = 2
MOE_FF = 1024
PAST_LEN = 16384

LANES = 128
SUBLANES = 8
VMEM_LIMIT = 52 * 1024 * 1024
MOE_ROWS = 256
ROW_TILES = D_MODEL // LANES
ROW_PITCH = ROW_TILES + 1
DMA_UNROLL = 8
W_CHUNK = 256
GU_CHUNKS = D_MODEL // W_CHUNK
RUN_CHUNKS = GU_CHUNKS + MOE_FF // W_CHUNK
W_STAGES = 4


def _cparams(sem):
    return pltpu.CompilerParams(dimension_semantics=sem, vmem_limit_bytes=VMEM_LIMIT)


def _sigmoid(x):
    return 1.0 / (1.0 + jnp.exp(-x))


def _silu(x):
    return x * _sigmoid(x)


def _bf16_round(x):
    return x.astype(BF16).astype(F32)


def _ada_kernel(c_ref, w_ref, b_ref, o_ref):
    c = c_ref[...]
    cond = _silu(c).astype(BF16)
    o_ref[...] = jnp.dot(cond, w_ref[...].astype(BF16), preferred_element_type=F32) + b_ref[...]


def _ada_mod(c_all, ada_w, ada_b):
    rows = c_all.shape[0]
    n = ada_w.shape[-1]
    tn = 1024
    return pl.pallas_call(
        _ada_kernel,
        grid=(DEPTH, n // tn),
        in_specs=[
            pl.BlockSpec((rows, D_MODEL), lambda l, j: (0, 0)),
            pl.BlockSpec((None, D_MODEL, tn), lambda l, j: (l, 0, j)),
            pl.BlockSpec((None, 1, tn), lambda l, j: (l, 0, j)),
        ],
        out_specs=pl.BlockSpec((None, rows, tn), lambda l, j: (l, 0, j)),
        out_shape=jax.ShapeDtypeStruct((DEPTH, rows, n), F32),
        compiler_params=_cparams(("arbitrary", "arbitrary")),
        name="ada_mod",
    )(c_all, ada_w, ada_b.reshape(DEPTH, 1, n))


def _norm_mod_rows(x, gn, sc, sh):
    inv = lax.rsqrt(jnp.mean(x * x, axis=-1, keepdims=True) + EPS)
    return (x * inv) * gn * (1.0 + sc) + sh


def _norm_mod_into(x_ref, gn_ref, sc_ref, sh_ref, dst_refs, tm):
    ch = min(tm, 128)
    per_row = sc_ref.shape[0] != 1

    def body(c, carry):
        rs = pl.ds(pl.multiple_of(c * ch, ch), ch)
        sc = sc_ref[rs, :] if per_row else sc_ref[...]
        sh = sh_ref[rs, :] if per_row else sh_ref[...]
        h = _norm_mod_rows(x_ref[rs, :], gn_ref[...], sc, sh)
        for d in dst_refs:
            d[rs, :] = h.astype(d.dtype)
        return carry

    lax.fori_loop(0, tm // ch, body, 0)


def _norm_mod_kernel(x_ref, gn_ref, sc_ref, sh_ref, o_ref, *, tm):
    _norm_mod_into(x_ref, gn_ref, sc_ref, sh_ref, [o_ref], tm)


def _norm_mod(x, gn, mod, shift_col, scale_col, *, tm, rows_per_batch):
    m = x.shape[0]
    bpb = max(rows_per_batch // tm, 1)
    msp = lambda col: pl.BlockSpec((None, mod.shape[1], D_MODEL), lambda i: (i // bpb, 0, col))
    return pl.pallas_call(
        functools.partial(_norm_mod_kernel, tm=tm),
        grid=(m // tm,),
        in_specs=[pl.BlockSpec((tm, D_MODEL), lambda i: (i, 0)),
                  pl.BlockSpec((1, D_MODEL), lambda i: (0, 0)),
                  msp(scale_col), msp(shift_col)],
        out_specs=pl.BlockSpec((tm, D_MODEL), lambda i: (i, 0)),
        out_shape=jax.ShapeDtypeStruct((m, D_MODEL), BF16),
        compiler_params=_cparams(("arbitrary",)),
        name="norm_mod",
    )(x, gn.reshape(1, D_MODEL), mod, mod)


def _rope_tile(acc, cos, sin_signed):
    outs = []
    for h in range(acc.shape[1] // SW_HEAD_DIM):
        xh = acc[:, h * SW_HEAD_DIM:(h + 1) * SW_HEAD_DIM]
        outs.append(xh * cos + pltpu.roll(xh, SW_HEAD_DIM // 2, 1) * sin_signed)
    return jnp.concatenate(outs, axis=1)


def _mm_kernel(x_ref, w_ref, *rest, epilogue):
    o_ref, wb_ref = rest[-2:]
    j, i = pl.program_id(0), pl.program_id(1)

    @pl.when(i == 0)
    def _():
        wb_ref[...] = w_ref[...].astype(BF16)

    acc = jnp.dot(x_ref[...].astype(BF16), wb_ref[...], preferred_element_type=F32)
    if epilogue == "rope":
        cos_ref, sin_ref = rest[:2]
        is_qk = (j % 3) < 2
        cos = jnp.where(is_qk, cos_ref[...], 1.0)
        sin = jnp.where(is_qk, sin_ref[...], 0.0)
        o_ref[...] = _rope_tile(acc, cos, sin)
    elif epilogue == "resgate":
        res_ref, g_ref = rest[:2]
        o_ref[...] = res_ref[...] + g_ref[...] * acc
    else:
        o_ref[...] = acc


def _mm(x, w, wl, *, tm, tn, rope=None, resgate=None):
    m, k = x.shape
    n = w.shape[2]
    in_specs = [pl.BlockSpec((tm, k), lambda j, i: (i, 0)),
                pl.BlockSpec((None, k, tn), lambda j, i: (wl, 0, j))]
    args = [x, w]
    epilogue = "none"
    if rope is not None:
        epilogue = "rope"
        nblk = rope[0].shape[0] // tm
        in_specs += [pl.BlockSpec((tm, SW_HEAD_DIM), lambda j, i: (i % nblk, 0))] * 2
        args += list(rope)
    if resgate is not None:
        epilogue = "resgate"
        res, mod, gate_col, rows_per_batch = resgate
        bpb = max(rows_per_batch // tm, 1)
        per = D_MODEL // tn
        in_specs += [pl.BlockSpec((tm, tn), lambda j, i: (i, j)),
                     pl.BlockSpec((None, mod.shape[1], tn), lambda j, i: (i // bpb, 0, gate_col * per + j))]
        args += [res, mod]
    return pl.pallas_call(
        functools.partial(_mm_kernel, epilogue=epilogue),
        grid=(n // tn, m // tm),
        in_specs=in_specs,
        out_specs=pl.BlockSpec((tm, tn), lambda j, i: (i, j)),
        out_shape=jax.ShapeDtypeStruct((m, n), F32),
        scratch_shapes=[pltpu.VMEM((k, tn), BF16)],
        compiler_params=_cparams(("arbitrary", "arbitrary")),
        name="mm_" + epilogue,
    )(*args)


def _hgrn_gates(qp, fp, lb):
    q = _silu(qp)
    f = lb + (1.0 - lb) * _sigmoid(fp)
    logf = jnp.log(jnp.maximum(f, F_FLOOR))
    k = (1.0 - lb) * _sigmoid(-fp)
    return q, logf, k


def _head_norm_gate(o, gn, gp):
    inv = lax.rsqrt(jnp.mean(o * o, axis=-1, keepdims=True) + EPS)
    return (o * inv) * gn * _silu(gp)


def _gla_kernel(q_ref, f_ref, i_ref, g_ref, lb_ref, gn_ref, o_ref, s_out_ref, st_ref, *, n_chunks, n_hb):
    c_len, n_sub = HG_CHUNK, HG_CHUNK // HG_SUB
    t = pl.program_id(2)

    @pl.when(t == 0)
    def _():
        st_ref[...] = jnp.zeros_like(st_ref)

    row = lax.broadcasted_iota(jnp.int32, (c_len, HG_DK), 0)
    r4 = lax.broadcasted_iota(jnp.int32, (c_len, n_sub * c_len), 0)
    c4 = lax.broadcasted_iota(jnp.int32, (c_len, n_sub * c_len), 1)
    mask4 = ((r4 // HG_SUB) == (c4 // c_len)) & ((c4 % c_len) <= r4)
    nt = (((1,), (1,)), ((), ()))
    tn = (((0,), (0,)), ((), ()))

    for c, hh in [(c, hh) for c in range(n_chunks) for hh in range(n_hb)]:
        rs = pl.ds(c * c_len, c_len)
        cs = slice(hh * HG_DK, (hh + 1) * HG_DK)
        lb, gn = lb_ref[:, cs], gn_ref[:, cs]
        q, logf, k = _hgrn_gates(q_ref[rs, cs], f_ref[rs, cs], lb)
        v = i_ref[rs, cs]
        g = logf
        sh = 1
        while sh < c_len:
            g = g + jnp.where(row >= sh, pltpu.roll(g, sh, 0), 0.0)
            sh *= 2
        refs = [g[HG_SUB * i + HG_SUB // 2 - 1:HG_SUB * i + HG_SUB // 2, :] for i in range(n_sub)]
        mrows = jnp.concatenate([jnp.broadcast_to(r, (HG_SUB, HG_DK)) for r in refs], axis=0)
        qh = (q * jnp.exp(g - mrows)).astype(BF16)
        khs = [k * jnp.exp(jnp.where(row < HG_SUB * (i + 1), refs[i] - g, -jnp.inf)) for i in range(n_sub)]
        kh = jnp.concatenate(khs, axis=0).astype(BF16)
        a4 = lax.dot_general(qh, kh, nt, preferred_element_type=F32)
        a4 = jnp.where(mask4, a4, 0.0).astype(BF16)
        vb = v.astype(BF16)
        v4 = jnp.concatenate([vb] * n_sub, axis=0)
        st = st_ref[hh]
        o = jnp.dot(a4, v4, preferred_element_type=F32)
        o = o + lax.dot_general((q * jnp.exp(g)).astype(BF16), st.astype(BF16), nt,
                                preferred_element_type=F32)
        glast = g[c_len - 1:c_len, :]
        kd = (k * jnp.exp(glast - g)).astype(BF16)
        st_ref[hh] = st * jnp.exp(glast) + lax.dot_general(vb, kd, tn, preferred_element_type=F32)
        o_ref[rs, cs] = _head_norm_gate(o, gn, g_ref[rs, cs]).astype(o_ref.dtype)

    @pl.when(t == pl.num_programs(2) - 1)
    def _():
        for hh in range(n_hb):
            s_out_ref[hh] = st_ref[hh].T


def _gla_prompt(proj, lower, g_norm, batch, seq):
    tb = 512
    n_hb = 2
    nt = seq // tb
    hblk = HG_HEADS // n_hb
    wcol = n_hb * HG_DK

    def col(off):
        return pl.BlockSpec((tb, wcol), lambda b, hh, t: (b * nt + t, off * hblk + hh))

    return pl.pallas_call(
        functools.partial(_gla_kernel, n_chunks=tb // HG_CHUNK, n_hb=n_hb),
        grid=(batch, hblk, nt),
        in_specs=[col(0), col(1), col(2), col(3),
                  pl.BlockSpec((1, wcol), lambda b, hh, t: (0, hh)),
                  pl.BlockSpec((1, wcol), lambda b, hh, t: (0, hh))],
        out_specs=[pl.BlockSpec((tb, wcol), lambda b, hh, t: (b * nt + t, hh)),
                   pl.BlockSpec((None, n_hb, HG_DK, HG_DK), lambda b, hh, t: (b, hh, 0, 0))],
        out_shape=[jax.ShapeDtypeStruct((batch * seq, D_MODEL), BF16),
                   jax.ShapeDtypeStruct((batch, HG_HEADS, HG_DK, HG_DK), F32)],
        scratch_shapes=[pltpu.VMEM((n_hb, HG_DK, HG_DK), F32)],
        compiler_params=_cparams(("arbitrary", "arbitrary", "arbitrary")),
        name="hgrn_prompt",
    )(proj, proj, proj, proj, lower.reshape(1, D_MODEL), g_norm.reshape(1, D_MODEL))


def _gla_step_kernel(qt_ref, ft_ref, i_ref, g_ref, lbt_ref, gn_ref, s_ref, o_ref, s_out_ref):
    for h in range(HG_HEADS):
        cs = slice(h * HG_DK, (h + 1) * HG_DK)
        q, logf, k = _hgrn_gates(qt_ref[:, h:h + 1], ft_ref[:, h:h + 1], lbt_ref[:, h:h + 1])
        dec = jnp.exp(logf)
        s_old = s_ref[h]
        v = i_ref[:, cs]
        s_out_ref[h] = dec * s_old + k * v
        qd = _bf16_round(jnp.broadcast_to(q * dec, s_old.shape))
        o = jnp.sum(_bf16_round(s_old) * qd, axis=0, keepdims=True) + jnp.sum(q * k, axis=0, keepdims=True) * v
        o_ref[:, cs] = _head_norm_gate(o, gn_ref[:, cs], g_ref[:, cs])


def _gla_step(proj, lower, g_norm, states, layer):
    b = proj.shape[0]
    h = HG_HEADS
    pq = proj[:, :D_MODEL].reshape(b, h, HG_DK).transpose(0, 2, 1)
    pf = proj[:, D_MODEL:2 * D_MODEL].reshape(b, h, HG_DK).transpose(0, 2, 1)
    p3 = proj.reshape(b, 1, 4 * D_MODEL)
    lbt = lower.reshape(h, HG_DK).T
    vec = lambda col: pl.BlockSpec((None, 1, D_MODEL), lambda i: (i, 0, col))
    tr = pl.BlockSpec((None, HG_DK, h), lambda i: (i, 0, 0))
    o, s_new = pl.pallas_call(
        _gla_step_kernel,
        grid=(b,),
        in_specs=[tr, tr, vec(2), vec(3),
                  pl.BlockSpec((HG_DK, h), lambda i: (0, 0)),
                  pl.BlockSpec((1, D_MODEL), lambda i: (0, 0)),
                  pl.BlockSpec((None, None, h, HG_DK, HG_DK), lambda i: (layer, i, 0, 0, 0))],
        out_specs=[pl.BlockSpec((None, 1, D_MODEL), lambda i: (i, 0, 0)),
                   pl.BlockSpec((None, h, HG_DK, HG_DK), lambda i: (i, 0, 0, 0))],
        out_shape=[jax.ShapeDtypeStruct((b, 1, D_MODEL), F32),
                   jax.ShapeDtypeStruct(states.shape[1:], F32)],
        compiler_params=_cparams(("arbitrary",)),
        name="hgrn_step",
    )(pq, pf, p3, p3, lbt, g_norm.reshape(1, D_MODEL), states)
    return o.reshape(b, D_MODEL), s_new


def _conv_kernel(bg_ref, cg_ref, u_ref, hc_ref, hu_ref, w_ref, o_ref, tail_ref, *, tm):
    t = pl.program_id(1)
    z = cg_ref[...] * u_ref[...]
    hz = hc_ref[...] * hu_ref[...]
    hz = jnp.where(t == 0, 0.0, hz)
    z1p, z2p = hz[SUBLANES - 1:SUBLANES, :], hz[SUBLANES - 2:SUBLANES - 1, :]
    row = lax.broadcasted_iota(jnp.int32, z.shape, 0)
    z1 = jnp.where(row == 0, z1p, pltpu.roll(z, 1, 0))
    z2 = jnp.where(row == 0, z2p, jnp.where(row == 1, z1p, pltpu.roll(z, 2, 0)))
    y = z2 * w_ref[0:1, :] + z1 * w_ref[1:2, :] + z * w_ref[2:3, :]
    o_ref[...] = (bg_ref[...] * y).astype(o_ref.dtype)
    tail_ref[...] = z[tm - (CONV_WIDTH - 1):, :]


def _conv_prompt(proj, w_conv, batch, seq):
    tm = 256
    nt = seq // tm
    hb = tm // SUBLANES
    blk = lambda col: pl.BlockSpec((tm, D_MODEL), lambda b, t: (b * nt + t, col))
    halo = lambda col: pl.BlockSpec(
        (SUBLANES, D_MODEL), lambda b, t: (jnp.maximum((b * nt + t) * hb - 1, 0), col))
    return pl.pallas_call(
        functools.partial(_conv_kernel, tm=tm),
        grid=(batch, nt),
        in_specs=[blk(0), blk(1), blk(2), halo(1), halo(2),
                  pl.BlockSpec((CONV_WIDTH, D_MODEL), lambda b, t: (0, 0))],
        out_specs=[pl.BlockSpec((tm, D_MODEL), lambda b, t: (b * nt + t, 0)),
                   pl.BlockSpec((None, CONV_WIDTH - 1, D_MODEL), lambda b, t: (b, 0, 0))],
        out_shape=[jax.ShapeDtypeStruct((batch * seq, D_MODEL), BF16),
                   jax.ShapeDtypeStruct((batch, CONV_WIDTH - 1, D_MODEL), F32)],
        compiler_params=_cparams(("arbitrary", "arbitrary")),
        name="conv_prompt",
    )(proj, proj, proj, proj, proj, w_conv)


def _conv_step_kernel(bg_ref, cg_ref, u_ref, buf_ref, w_ref, o_ref, nb_ref):
    z = cg_ref[...] * u_ref[...]
    buf = buf_ref[...]
    y = buf[:, 0:1, :] * w_ref[0:1, :] + buf[:, 1:2, :] * w_ref[1:2, :] + z * w_ref[2:3, :]
    o_ref[...] = bg_ref[...] * y
    nb_ref[:, 0:1, :] = buf[:, 1:2, :]
    nb_ref[:, 1:2, :] = z


def _conv_step(proj, w_conv, buf):
    b = proj.shape[0]
    p3 = proj.reshape(b, 1, 3 * D_MODEL)
    vec = lambda col: pl.BlockSpec((b, 1, D_MODEL), lambda i: (0, 0, col))
    o, nb = pl.pallas_call(
        _conv_step_kernel,
        grid=(1,),
        in_specs=[vec(0), vec(1), vec(2),
                  pl.BlockSpec((b, CONV_WIDTH - 1, D_MODEL), lambda i: (0, 0, 0)),
                  pl.BlockSpec((CONV_WIDTH, D_MODEL), lambda i: (0, 0))],
        out_specs=[pl.BlockSpec((b, 1, D_MODEL), lambda i: (0, 0, 0)),
                   pl.BlockSpec((b, CONV_WIDTH - 1, D_MODEL), lambda i: (0, 0, 0))],
        out_shape=[jax.ShapeDtypeStruct((b, 1, D_MODEL), F32),
                   jax.ShapeDtypeStruct(buf.shape, F32)],
        compiler_params=_cparams(("arbitrary",)),
        name="conv_step",
    )(p3, p3, p3, buf, w_conv)
    return o.reshape(b, D_MODEL), nb


def _swa_kernel(*refs, dil, n_units, n_heads, has_prev):
    if has_prev:
        qc_ref, kp_ref, kc_ref, vp_ref, vc_ref, o_ref, l_ref = refs
    else:
        qc_ref, kc_ref, vc_ref, o_ref, l_ref = refs
    blk = SW_BAND
    n = pl.program_id(1)
    ri = lax.broadcasted_iota(jnp.int32, (blk, blk), 0)
    ci = lax.broadcasted_iota(jnp.int32, (blk, blk), 1)
    m_cur = ci <= ri
    m_prev_tri = ci >= ri
    lane = lax.broadcasted_iota(jnp.int32, (blk, LANES), 1)
    scale = SW_HEAD_DIM ** -0.5
    nt = (((1,), (1,)), ((), ()))

    for u in range(n_units):
        if dil == 1:
            rows = pl.ds(u * blk, blk)
            prev_src = None
            if has_prev:
                prev_src = (kp_ref, vp_ref, pl.ds(0, blk), True) if u == 0 else \
                    (kc_ref, vc_ref, pl.ds((u - 1) * blk, blk), False)
        else:
            rows = pl.ds(u, blk, stride=dil)
            prev_src = (kp_ref, vp_ref, rows, True) if has_prev else None

        def head(h, lse_acc, rows=rows, prev_src=prev_src):
            cs = pl.ds(0, SW_HEAD_DIM) if n_heads == 1 else \
                pl.ds(pl.multiple_of(h * SW_HEAD_DIM, SW_HEAD_DIM), SW_HEAD_DIM)
            q = qc_ref[rows, cs].astype(BF16)
            s_c = lax.dot_general(q, kc_ref[rows, cs].astype(BF16), nt, preferred_element_type=F32) * scale
            s_c = jnp.where(m_cur, s_c, MASK_VALUE)
            m = jnp.max(s_c, axis=-1, keepdims=True)
            if prev_src is not None:
                kr, vr, prow, first_only = prev_src
                s_p = lax.dot_general(q, kr[prow, cs].astype(BF16), nt, preferred_element_type=F32) * scale
                m_prev = (m_prev_tri & (n > 0)) if first_only else m_prev_tri
                s_p = jnp.where(m_prev, s_p, MASK_VALUE)
                m = jnp.maximum(m, jnp.max(s_p, axis=-1, keepdims=True))
            p_c = jnp.exp(s_c - m)
            l = jnp.sum(p_c, axis=-1, keepdims=True)
            o = jnp.dot(p_c.astype(BF16), vc_ref[rows, cs].astype(BF16), preferred_element_type=F32)
            if prev_src is not None:
                p_p = jnp.exp(s_p - m)
                l = l + jnp.sum(p_p, axis=-1, keepdims=True)
                o = o + jnp.dot(p_p.astype(BF16), vr[prow, cs].astype(BF16), preferred_element_type=F32)
            o_ref[rows, cs] = o / l
            lse = m + jnp.log(l)
            return jnp.where(lane == h, lse, lse_acc)

        lse0 = jnp.zeros((blk, LANES), F32)
        l_ref[rows, :] = head(0, lse0) if n_heads == 1 else lax.fori_loop(0, n_heads, head, lse0, unroll=2)


def _swa_prompt_group(qkv, gi, dil, batch, seq):
    span = SW_BAND * dil
    if dil == 1:
        span, n_units = 512, 4
    else:
        n_units = dil
    nb = seq // span
    has_prev = nb > 1
    hb = SW_HEADS if dil == 1 else 1
    n_hblk = SW_HEADS // hb
    wcol = hb * SW_HEAD_DIM
    per = SW_WIDTH // wcol
    pspan = SW_BAND if dil == 1 else span
    pmul = span // pspan

    def cur(which):
        return pl.BlockSpec((span, wcol), lambda b, n, hh: (b * nb + n, (gi * 3 + which) * per + hh))

    def prev(which):
        return pl.BlockSpec(
            (pspan, wcol),
            lambda b, n, hh: (jnp.maximum((b * nb + n) * pmul - 1, 0), (gi * 3 + which) * per + hh))

    if has_prev:
        in_specs = [cur(0), prev(1), cur(1), prev(2), cur(2)]
        args = [qkv] * 5
    else:
        in_specs = [cur(0), cur(1), cur(2)]
        args = [qkv] * 3
    return pl.pallas_call(
        functools.partial(_swa_kernel, dil=dil, n_units=n_units, n_heads=hb, has_prev=has_prev),
        grid=(batch, nb, n_hblk),
        in_specs=in_specs,
        out_specs=[pl.BlockSpec((span, wcol), lambda b, n, hh: (b * nb + n, hh)),
                   pl.BlockSpec((span, LANES), lambda b, n, hh: (b * nb + n, hh))],
        out_shape=[jax.ShapeDtypeStruct((batch * seq, SW_WIDTH), F32),
                   jax.ShapeDtypeStruct((batch * seq, n_hblk * LANES), F32)],
        compiler_params=_cparams(("arbitrary", "arbitrary", "arbitrary")),
        name=f"swa_prompt_g{gi}",
    )(*args)


def _merge_heads(o_refs, l_refs, rs, hbs):
    outs = []
    for h in range(SW_HEADS):
        cs = slice(h * SW_HEAD_DIM, (h + 1) * SW_HEAD_DIM)
        lses = []
        for l_ref, hb in zip(l_refs, hbs):
            lane = (h // hb) * LANES + h % hb
            lses.append(l_ref[rs, lane:lane + 1])
        mx = jnp.maximum(jnp.maximum(lses[0], lses[1]), lses[2])
        es = [jnp.exp(l - mx) for l in lses]
        den = es[0] + es[1] + es[2]
        acc = None
        for e, o_ref in zip(es, o_refs):
            o = o_ref[rs, cs]
            term = _bf16_round(jnp.broadcast_to(e / den, o.shape)) * _bf16_round(o)
            acc = term if acc is None else acc + term
        outs.append(acc)
    return jnp.concatenate(outs, axis=1)


def _swa_merge_kernel(o1, o2, o3, l1, l2, l3, out_ref, *, tm, hbs):
    ch = 128

    def body(c, carry):
        rs = pl.ds(pl.multiple_of(c * ch, ch), ch)
        out_ref[rs, :] = _merge_heads((o1, o2, o3), (l1, l2, l3), rs, hbs).astype(out_ref.dtype)
        return carry

    lax.fori_loop(0, tm // ch, body, 0)


def _swa_merge(os_, ls_):
    tm = 512
    m = os_[0].shape[0]
    hbs = tuple(SW_HEADS // (l.shape[1] // LANES) for l in ls_)
    row = lambda width: pl.BlockSpec((tm, width), lambda i: (i, 0))
    return pl.pallas_call(
        functools.partial(_swa_merge_kernel, tm=tm, hbs=hbs),
        grid=(m // tm,),
        in_specs=[row(SW_WIDTH)] * 3 + [row(l.shape[1]) for l in ls_],
        out_specs=row(SW_WIDTH),
        out_shape=jax.ShapeDtypeStruct((m, SW_WIDTH), BF16),
        compiler_params=_cparams(("arbitrary",)),
        name="swa_merge",
    )(*os_, *ls_)


def _swa_step_kernel(qkv_ref, c1_ref, c2_ref, c3_ref, o_ref):
    scale = SW_HEAD_DIM ** -0.5
    outs, lses = [], []
    for gi, c_ref in enumerate((c1_ref, c2_ref, c3_ref)):
        q = _bf16_round(qkv_ref[gi, 0])
        kn = _bf16_round(qkv_ref[gi, 1])
        vn = _bf16_round(qkv_ref[gi, 2])
        kc = _bf16_round(c_ref[:, 0])
        vc = _bf16_round(c_ref[:, 1])
        s = jnp.sum(kc * q[None], axis=-1, keepdims=True) * scale
        sn = jnp.sum(kn * q, axis=-1, keepdims=True) * scale
        m = jnp.maximum(jnp.max(s, axis=0), sn)
        lse = m + jnp.log(jnp.sum(jnp.exp(s - m[None]), axis=0) + jnp.exp(sn - m))
        shape = vc.shape
        p = _bf16_round(jnp.broadcast_to(jnp.exp(s - lse[None]), shape))
        pn = _bf16_round(jnp.broadcast_to(jnp.exp(sn - lse), shape[1:]))
        outs.append(jnp.sum(p * vc, axis=0) + pn * vn)
        lses.append(lse)
    mx = jnp.maximum(jnp.maximum(lses[0], lses[1]), lses[2])
    es = [jnp.exp(l - mx) for l in lses]
    den = es[0] + es[1] + es[2]
    acc = None
    for e, o in zip(es, outs):
        term = _bf16_round(jnp.broadcast_to(e / den, o.shape)) * _bf16_round(o)
        acc = term if acc is None else acc + term
    o_ref[...] = acc


def _swa_step(qkv, caches):
    b = qkv.shape[0]
    q5 = qkv.reshape(b, len(SW_GROUPS), 3, SW_HEADS, SW_HEAD_DIM)
    views, specs = [], []
    for c, (win, dil) in zip(caches, SW_GROUPS):
        views.append(c.reshape(b, win // dil, dil, 2, SW_HEADS, SW_HEAD_DIM))
        specs.append(pl.BlockSpec((None, win // dil, None, 2, SW_HEADS, SW_HEAD_DIM),
                                  lambda i: (i, 0, 0, 0, 0, 0)))
    o = pl.pallas_call(
        _swa_step_kernel,
        grid=(b,),
        in_specs=[pl.BlockSpec((None, len(SW_GROUPS), 3, SW_HEADS, SW_HEAD_DIM),
                               lambda i: (i, 0, 0, 0, 0))] + specs,
        out_specs=pl.BlockSpec((None, SW_HEADS, SW_HEAD_DIM), lambda i: (i, 0, 0)),
        out_shape=jax.ShapeDtypeStruct((b, SW_HEADS, SW_HEAD_DIM), F32),
        compiler_params=_cparams(("arbitrary",)),
        name="swa_step",
    )(q5, *views)
    return o.reshape(b, SW_WIDTH)


def _cache_shift_kernel(cur_ref, nxt_ref, knew_ref, vnew_ref, o_ref, *, wb):
    k = pl.program_id(1)
    last = pl.num_programs(1) - 1

    def row(i, carry):
        o_ref[i] = cur_ref[i + 1]
        return carry

    lax.fori_loop(0, wb - 1, row, 0, unroll=8)

    @pl.when(k < last)
    def _():
        o_ref[wb - 1] = nxt_ref[0]

    @pl.when(k == last)
    def _():
        o_ref[wb - 1, 0] = knew_ref[...]
        o_ref[wb - 1, 1] = vnew_ref[...]


def _cache_shift(qkv, caches):
    b = qkv.shape[0]
    q5 = qkv.reshape(b, len(SW_GROUPS), 3, SW_HEADS, SW_HEAD_DIM)
    outs = []
    for gi, c in enumerate(caches):
        w = c.shape[1]
        wb = min(w, 512)
        tail = (2, SW_HEADS, SW_HEAD_DIM)
        outs.append(pl.pallas_call(
            functools.partial(_cache_shift_kernel, wb=wb),
            grid=(b, w // wb),
            in_specs=[
                pl.BlockSpec((None, wb) + tail, lambda i, k: (i, k, 0, 0, 0)),
                pl.BlockSpec((None, 1) + tail, lambda i, k: (i, jnp.minimum((k + 1) * wb, w - 1), 0, 0, 0)),
                pl.BlockSpec((None, None, None, SW_HEADS, SW_HEAD_DIM), lambda i, k: (i, gi, 1, 0, 0)),
                pl.BlockSpec((None, None, None, SW_HEADS, SW_HEAD_DIM), lambda i, k: (i, gi, 2, 0, 0)),
            ],
            out_specs=pl.BlockSpec((None, wb) + tail, lambda i, k: (i, k, 0, 0, 0)),
            out_shape=jax.ShapeDtypeStruct(c.shape, F32),
            compiler_params=_cparams(("arbitrary", "arbitrary")),
            name=f"cache_shift_g{gi}",
        )(c, c, q5, q5))
    return outs


def _kv_window_kernel(k_ref, v_ref, o_ref, *, rows):
    for kv, src in enumerate((k_ref, v_ref)):
        for h in range(SW_HEADS):
            o_ref[pl.ds(kv * SW_HEADS + h, rows, stride=2 * SW_HEADS), :] = \
                src[:, h * SW_HEAD_DIM:(h + 1) * SW_HEAD_DIM]


def _kv_window(qkv, gi, win, batch, seq):
    wlen = min(win, seq)
    rows = 128
    nt = wlen // rows
    per_row = 2 * SW_HEADS
    src = lambda which: pl.BlockSpec(
        (rows, SW_WIDTH), lambda b, t: ((b * seq + seq - wlen) // rows + t, gi * 3 + which))
    out = pl.pallas_call(
        functools.partial(_kv_window_kernel, rows=rows),
        grid=(batch, nt),
        in_specs=[src(1), src(2)],
        out_specs=pl.BlockSpec((None, rows * per_row, SW_HEAD_DIM), lambda b, t: (b, t, 0)),
        out_shape=jax.ShapeDtypeStruct((batch, wlen * per_row, SW_HEAD_DIM), F32),
        compiler_params=_cparams(("arbitrary", "arbitrary")),
        name=f"kv_window_g{gi}",
    )(qkv, qkv)
    return out.reshape(batch, wlen, 2, SW_HEADS, SW_HEAD_DIM)


def _router_kernel(x_ref, gn_ref, sc_ref, sh_ref, wr_ref, br_ref, *rest, tm):
    h_ref, ids_ref, gates_ref, hf_ref = rest[-4:]
    _norm_mod_into(x_ref, gn_ref, sc_ref, sh_ref, [hf_ref], tm)
    for s in range(ROW_TILES):
        h_ref[pl.ds(s, tm, stride=ROW_TILES), :] = hf_ref[:, s * LANES:(s + 1) * LANES]
    logits = jnp.dot(hf_ref[...].astype(BF16), wr_ref[...].astype(BF16),
                     preferred_element_type=F32) + br_ref[...]
    lane = lax.broadcasted_iota(jnp.int32, logits.shape, 1)
    big = jnp.int32(1 << 20)
    is_g = (lane >= MOE_EXPERTS) & (lane < MOE_EXPERTS + MOE_GROUPS)
    glog = jnp.where(is_g, logits, -jnp.inf)
    gmax = jnp.max(glog, axis=-1, keepdims=True)
    gsel = jnp.min(jnp.where(glog == gmax, lane - MOE_EXPERTS, big), axis=-1, keepdims=True)
    gsum = jnp.sum(jnp.where(is_g, jnp.exp(glog - gmax), 0.0), axis=-1, keepdims=True)
    pg = 1.0 / gsum
    in_grp = (lane < MOE_EXPERTS) & ((lane // MOE_PER_GROUP) == gsel)
    el = jnp.where(in_grp, logits, -jnp.inf)
    v1 = jnp.max(el, axis=-1, keepdims=True)
    i1 = jnp.min(jnp.where(el == v1, lane, big), axis=-1, keepdims=True)
    el2 = jnp.where(lane == i1, -jnp.inf, el)
    v2 = jnp.max(el2, axis=-1, keepdims=True)
    i2 = jnp.min(jnp.where(el2 == v2, lane, big), axis=-1, keepdims=True)
    e2 = jnp.exp(v2 - v1)
    den = 1.0 + e2
    ids_ref[...] = jnp.where(lane == 0, i1, jnp.where(lane == 1, i2, 0))
    gates_ref[...] = jnp.where(lane == 0, pg * (1.0 / den), jnp.where(lane == 1, pg * (e2 / den), 0.0))


def _router(x, gn, mod, wr, br, hcat, n_tok, *, tm, rows_per_batch, row0):
    m = x.shape[0]
    blk0 = row0 // tm
    bpb = max(rows_per_batch // tm, 1)
    nb = m // tm
    n_steps = nb + (1 if hcat is None and n_tok > m else 0)
    cl = lambda i: jnp.minimum(i, nb - 1)
    msp = lambda col: pl.BlockSpec((None, mod.shape[1], D_MODEL), lambda i: (cl(i) // bpb, 0, col))
    in_specs = [
        pl.BlockSpec((tm, D_MODEL), lambda i: (cl(i), 0)),
        pl.BlockSpec((1, D_MODEL), lambda i: (0, 0)),
        msp(4), msp(3),
        pl.BlockSpec((D_MODEL, LANES), lambda i: (0, 0)),
        pl.BlockSpec((1, LANES), lambda i: (0, 0)),
    ]
    args = [x, gn.reshape(1, D_MODEL), mod, mod, wr, br]
    aliases = {}
    if hcat is not None:
        in_specs.append(pl.BlockSpec(memory_space=pl.ANY))
        args.append(hcat)
        aliases = {len(args) - 1: 0}
    return pl.pallas_call(
        functools.partial(_router_kernel, tm=tm),
        grid=(n_steps,),
        in_specs=in_specs,
        out_specs=[pl.BlockSpec((tm * ROW_TILES, LANES), lambda i: (blk0 + i, 0)),
                   pl.BlockSpec((tm, LANES), lambda i: (cl(i), 0)),
                   pl.BlockSpec((tm, LANES), lambda i: (cl(i), 0))],
        out_shape=[jax.ShapeDtypeStruct((n_tok * ROW_TILES, LANES), F32),
                   jax.ShapeDtypeStruct((m, LANES), jnp.int32),
                   jax.ShapeDtypeStruct((m, LANES), F32)],
        scratch_shapes=[pltpu.VMEM((tm, D_MODEL), F32)],
        input_output_aliases=aliases,
        compiler_params=_cparams(("arbitrary",)),
        name="moe_router",
    )(*args)


def _experts_kernel(tok_ref, run_ref, ng_ref, qlo_ref, qhi_ref, ce_ref, ck_ref, cs_ref, meta_ref,
                    h_hbm, wgu_hbm, wd_hbm, y_ref,
                    rows_ref, stg_ref, wgu_ref, wd_ref, row_sem, w_sem, *, layer):
    b = pl.program_id(0)
    n_used, q_total = meta_ref[0], meta_ref[1]
    rt = ROW_TILES
    grp = DMA_UNROLL * rt

    def issue_rows(blk, slot):
        def body(g, carry):
            for u in range(DMA_UNROLL):
                r = g * DMA_UNROLL + u
                tok = tok_ref[blk * MOE_ROWS + r]
                pltpu.make_async_copy(h_hbm.at[pl.ds(pl.multiple_of(tok * rt, rt), rt)],
                                      rows_ref.at[slot, pl.ds(pl.multiple_of(r * rt, rt), rt)],
                                      row_sem.at[slot]).start()
            return carry

        lax.fori_loop(0, ng_ref[blk], body, 0)

    def wait_rows(blk, slot):
        def body(g, carry):
            pltpu.make_async_copy(h_hbm.at[pl.ds(0, grp)], rows_ref.at[slot, pl.ds(0, grp)], row_sem.at[slot]).wait()
            return carry

        lax.fori_loop(0, ng_ref[blk], body, 0)

    def chunk_start(q):
        e, k, st = ce_ref[q], ck_ref[q], q % W_STAGES
        row0 = lambda kk: pl.ds(pl.multiple_of(kk * W_CHUNK, W_CHUNK), W_CHUNK)

        @pl.when(k < GU_CHUNKS)
        def _():
            pltpu.make_async_copy(wgu_hbm.at[layer, e, row0(k)], stg_ref.at[st], w_sem.at[st]).start(priority=1)

        @pl.when(k >= GU_CHUNKS)
        def _():
            pltpu.make_async_copy(wd_hbm.at[layer, e, row0(k - GU_CHUNKS)], stg_ref.at[st],
                                  w_sem.at[st]).start(priority=1)

    def chunk_process(q):
        k, slot, st = ck_ref[q], cs_ref[q], q % W_STAGES
        pltpu.make_async_copy(wgu_hbm.at[layer, 0, pl.ds(0, W_CHUNK)], stg_ref.at[st], w_sem.at[st]).wait()
        sub = 32

        @pl.when(k < GU_CHUNKS)
        def _():
            for i in range(W_CHUNK // sub):
                dst = pl.ds(pl.multiple_of(k * W_CHUNK + i * sub, sub), sub)
                wgu_ref[slot, dst, :] = stg_ref[st, pl.ds(i * sub, sub), :].astype(BF16)

        @pl.when(k >= GU_CHUNKS)
        def _():
            for i in range(W_CHUNK // sub):
                dst = pl.ds(pl.multiple_of((k - GU_CHUNKS) * W_CHUNK + i * sub, sub), sub)
                wd_ref[slot, dst, :] = stg_ref[st, pl.ds(i * sub, sub), :].astype(BF16)

        @pl.when(q + W_STAGES < q_total)
        def _():
            chunk_start(q + W_STAGES)

    def process_range(lo, hi):
        def body(q, carry):
            chunk_process(q)
            return carry

        lax.fori_loop(lo, hi, body, 0)

    @pl.when(b == 0)
    def _():
        rows_ref[...] = jnp.zeros_like(rows_ref)
        issue_rows(0, 0)
        for q in range(W_STAGES):
            chunk_start(q)
        process_range(0, RUN_CHUNKS)

    @pl.when(b + 1 < n_used)
    def _():
        issue_rows(b + 1, (b + 1) % 2)

    @pl.when(b < n_used)
    def _():
        rslot = b % 2
        q_lo, q_hi = qlo_ref[b], qhi_ref[b]
        q_mid = (q_lo + q_hi) // 2
        process_range(q_lo, q_mid)
        wait_rows(b, rslot)
        x = jnp.concatenate(
            [rows_ref[rslot, pl.ds(s, MOE_ROWS, stride=rt), :].astype(BF16) for s in range(rt)], axis=1)
        wslot = run_ref[b] % 2
        for sl in range(2):
            @pl.when(wslot == sl)
            def _():
                au = jnp.dot(x, wgu_ref[sl], preferred_element_type=F32)
                hmid = (_silu(au[:, :MOE_FF]) * au[:, MOE_FF:]).astype(BF16)
                y = jnp.dot(hmid, wd_ref[sl], preferred_element_type=F32)
                for s in range(rt):
                    y_ref[pl.ds(s, MOE_ROWS, stride=rt), :] = y[:, s * LANES:(s + 1) * LANES]

        process_range(q_mid, q_hi)

    @pl.when(b >= n_used)
    def _():
        y_ref[...] = jnp.zeros_like(y_ref)


def _moe_experts(hcat, w_gu, w_down, layer, row_tok, blk_run, blk_groups, qlo, qhi, chunk_e, chunk_k, chunk_s, meta,
                 n_blocks):
    any_spec = pl.BlockSpec(memory_space=pl.ANY)
    return pl.pallas_call(
        functools.partial(_experts_kernel, layer=layer),
        grid_spec=pltpu.PrefetchScalarGridSpec(
            num_scalar_prefetch=9,
            grid=(n_blocks,),
            in_specs=[any_spec, any_spec, any_spec],
            out_specs=pl.BlockSpec((MOE_ROWS * ROW_TILES, LANES), lambda b, *_: (b, 0)),
            scratch_shapes=[pltpu.VMEM((2, MOE_ROWS * ROW_TILES, LANES), F32),
                            pltpu.VMEM((W_STAGES, W_CHUNK, D_MODEL), F32),
                            pltpu.VMEM((2, D_MODEL, 2 * MOE_FF), BF16),
                            pltpu.VMEM((2, MOE_FF, D_MODEL), BF16),
                            pltpu.SemaphoreType.DMA((2,)),
                            pltpu.SemaphoreType.DMA((W_STAGES,))],
        ),
        out_shape=jax.ShapeDtypeStruct((n_blocks * MOE_ROWS * ROW_TILES, LANES), F32),
        compiler_params=pltpu.CompilerParams(dimension_semantics=("arbitrary",), vmem_limit_bytes=VMEM_LIMIT,
                                             disable_bounds_checks=True),
        name="moe_experts",
    )(row_tok, blk_run, blk_groups, qlo, qhi, chunk_e, chunk_k, chunk_s, meta, hcat, w_gu, w_down)


def _combine_kernel(dest_ref, y_hbm, x_ref, g_ref, gates_ref, o_ref, y0_ref, y1_ref, sem, *, tm, tok0):
    i = pl.program_id(0)
    rt = ROW_TILES

    def issue(tile, slot):
        def body(g, carry):
            for u in range(DMA_UNROLL // 2):
                r = g * (DMA_UNROLL // 2) + u
                a = (tok0 + tile * tm + r) * MOE_TOP_K
                for j, y_ref in enumerate((y0_ref, y1_ref)):
                    pltpu.make_async_copy(y_hbm.at[pl.ds(pl.multiple_of(dest_ref[a + j] * rt, rt), rt)],
                                          y_ref.at[slot, pl.ds(r * ROW_PITCH, rt)],
                                          sem.at[slot]).start(priority=j)
            return carry

        lax.fori_loop(0, tm // (DMA_UNROLL // 2), body, 0)

    @pl.when(i == 0)
    def _():
        issue(0, 0)

    @pl.when(i + 1 < pl.num_programs(0))
    def _():
        issue(i + 1, (i + 1) % 2)

    slot = i % 2
    for y_ref in (y0_ref, y1_ref):
        pltpu.make_async_copy(y_hbm.at[pl.ds(0, tm * rt)], y_ref.at[slot, pl.ds(0, tm * rt)], sem.at[slot]).wait()
    gt = gates_ref[...]
    g0 = jnp.broadcast_to(gt[:, 0:1], (tm, LANES))
    g1 = jnp.broadcast_to(gt[:, 1:2], (tm, LANES))
    for s in range(rt):
        cs = slice(s * LANES, (s + 1) * LANES)
        rows = pl.ds(s, tm, stride=ROW_PITCH)
        y = y0_ref[slot, rows, :] * g0 + y1_ref[slot, rows, :] * g1
        o_ref[:, cs] = x_ref[:, cs] + g_ref[:, cs] * y


def _moe_combine(y, dest, x, mod, gates, *, tm, rows_per_batch, tok0):
    m = x.shape[0]
    bpb = max(rows_per_batch // tm, 1)
    return pl.pallas_call(
        functools.partial(_combine_kernel, tm=tm, tok0=tok0),
        grid_spec=pltpu.PrefetchScalarGridSpec(
            num_scalar_prefetch=1,
            grid=(m // tm,),
            in_specs=[pl.BlockSpec(memory_space=pl.ANY),
                      pl.BlockSpec((tm, D_MODEL), lambda i, d: (i, 0)),
                      pl.BlockSpec((None, mod.shape[1], D_MODEL), lambda i, d: (i // bpb, 0, 5)),
                      pl.BlockSpec((tm, LANES), lambda i, d: (i, 0))],
            out_specs=pl.BlockSpec((tm, D_MODEL), lambda i, d: (i, 0)),
            scratch_shapes=[pltpu.VMEM((2, tm * ROW_PITCH, LANES), F32)] * 2
            + [pltpu.SemaphoreType.DMA((2,))],
        ),
        out_shape=jax.ShapeDtypeStruct((m, D_MODEL), F32),
        compiler_params=pltpu.CompilerParams(dimension_semantics=("arbitrary",), vmem_limit_bytes=VMEM_LIMIT,
                                             disable_bounds_checks=True),
        name="moe_combine",
    )(dest, y, x, mod, gates)


def _moe_layer(xp, xs, mod_p, mod_s, gn, w_group, b_group, w_router, b_router, w_gu, w_down, layer, seq):
    n_p, n_s = xp.shape[0], xs.shape[0]
    n_tok = n_p + n_s
    pad = LANES - MOE_EXPERTS - MOE_GROUPS
    wr = jnp.concatenate([w_router, w_group, jnp.zeros((D_MODEL, pad), F32)], axis=1)
    br = jnp.concatenate([b_router, b_group, jnp.zeros((pad,), F32)]).reshape(1, LANES)
    hcat, ids_p, gates_p = _router(xp, gn, mod_p, wr, br, None, n_tok, tm=256, rows_per_batch=seq, row0=0)
    hcat, ids_s, gates_s = _router(xs, gn, mod_s, wr, br, hcat, n_tok, tm=n_s, rows_per_batch=n_s, row0=n_p)

    expert = jnp.concatenate([ids_p[:, :MOE_TOP_K], ids_s[:, :MOE_TOP_K]], axis=0).reshape(-1)
    n_assign = n_tok * MOE_TOP_K
    n_blocks = -(-n_assign // MOE_ROWS) + MOE_EXPERTS
    onehot = (expert[:, None] == jnp.arange(MOE_EXPERTS, dtype=jnp.int32)[None, :]).astype(jnp.int32)
    csum = jnp.cumsum(onehot, axis=0)
    rank = jnp.sum(csum * onehot, axis=1) - 1
    counts = csum[-1]
    padded = (counts + MOE_ROWS - 1) // MOE_ROWS * MOE_ROWS
    pend = jnp.cumsum(padded)
    pstart = pend - padded
    dest = (pstart[expert] + rank).astype(jnp.int32)
    tok = jnp.arange(n_assign, dtype=jnp.int32) // MOE_TOP_K
    row_tok = jnp.zeros((n_blocks * MOE_ROWS,), jnp.int32).at[dest].set(tok, unique_indices=True)
    blk_ids = jnp.arange(n_blocks, dtype=jnp.int32)
    blk_expert = jnp.minimum(jnp.searchsorted(pend, blk_ids * MOE_ROWS, side="right"),
                             MOE_EXPERTS - 1).astype(jnp.int32)
    n_used = (pend[-1:] // MOE_ROWS).astype(jnp.int32)
    nonempty = counts > 0
    n_runs = jnp.sum(nonempty.astype(jnp.int32))
    run_of_expert = jnp.cumsum(nonempty.astype(jnp.int32)) - 1
    run_expert = jnp.argsort(jnp.where(nonempty, 0, 1), stable=True).astype(jnp.int32)
    blk_run = run_of_expert[blk_expert].astype(jnp.int32)
    j_in_run = blk_ids - pstart[blk_expert] // MOE_ROWS
    n_in_run = jnp.maximum(padded[blk_expert] // MOE_ROWS, 1)
    q_total = n_runs * RUN_CHUNKS
    streams = (blk_ids < n_used[0]) & (blk_run + 1 < n_runs)
    base = (blk_run + 1) * RUN_CHUNKS
    qlo = jnp.where(streams, base + (RUN_CHUNKS * j_in_run) // n_in_run, q_total).astype(jnp.int32)
    qhi = jnp.where(streams, base + (RUN_CHUNKS * (j_in_run + 1)) // n_in_run, q_total).astype(jnp.int32)
    chunk_ids = jnp.arange(MOE_EXPERTS * RUN_CHUNKS, dtype=jnp.int32)
    chunk_e = run_expert[chunk_ids // RUN_CHUNKS]
    chunk_k = chunk_ids % RUN_CHUNKS
    chunk_s = (chunk_ids // RUN_CHUNKS) % 2
    meta = jnp.concatenate([n_used, q_total[None].astype(jnp.int32)])
    valid_rows = jnp.clip(counts[blk_expert] - j_in_run * MOE_ROWS, 0, MOE_ROWS)
    blk_groups = jnp.where(blk_ids < n_used[0], (valid_rows + DMA_UNROLL - 1) // DMA_UNROLL, 0).astype(jnp.int32)

    y = _moe_experts(hcat, w_gu, w_down, layer, row_tok, blk_run, blk_groups, qlo, qhi, chunk_e, chunk_k, chunk_s,
                     meta, n_blocks)
    xp_new = _moe_combine(y, dest, xp, mod_p, gates_p, tm=256, rows_per_batch=seq, tok0=0)
    xs_new = _moe_combine(y, dest, xs, mod_s, gates_s, tm=n_s, rows_per_batch=n_s, tok0=n_p)
    return xp_new, xs_new


def _final_norm_kernel(x_ref, g_ref, o_ref):
    x = x_ref[...]
    inv = lax.rsqrt(jnp.mean(x * x, axis=-1, keepdims=True) + EPS)
    o_ref[...] = (x * inv) * g_ref[...]


def _final_norm(x, g, tm):
    m = x.shape[0]
    return pl.pallas_call(
        _final_norm_kernel,
        grid=(m // tm,),
        in_specs=[pl.BlockSpec((tm, D_MODEL), lambda i: (i, 0)), pl.BlockSpec((1, D_MODEL), lambda i: (0, 0))],
        out_specs=pl.BlockSpec((tm, D_MODEL), lambda i: (i, 0)),
        out_shape=jax.ShapeDtypeStruct((m, D_MODEL), F32),
        compiler_params=_cparams(("arbitrary",)),
        name="final_norm",
    )(x, g.reshape(1, D_MODEL))


def _rope_tables(pos):
    half = SW_HEAD_DIM // 2
    inv_freq = ROPE_THETA ** (-jnp.arange(half, dtype=F32) / half)
    ang = pos.astype(F32)[:, None] * inv_freq[None, :]
    cos, sin = jnp.cos(ang), jnp.sin(ang)
    return jnp.concatenate([cos, cos], axis=1), jnp.concatenate([-sin, sin], axis=1)


def kernel(x_prompt, x_sample, state_hgrn, state_conv, cache_swa_g1, cache_swa_g2, cache_swa_g3, c_prompt, c_sample, ada_w, ada_b, norm_mix, norm_ffn, norm_final, hg_w_in, hg_w_out, hg_norm, hg_lower, cv_w_in, cv_w_conv, cv_w_out, sw_w_in, sw_w_out, moe_w_group, moe_b_group, moe_w_router, moe_b_router, moe_w_gu, moe_w_down):
    bp, seq, d = x_prompt.shape
    bs = x_sample.shape[0]
    n_p = bp * seq
    xp = x_prompt.reshape(n_p, d)
    xs = x_sample.reshape(bs, d)

    mod_all = _ada_mod(jnp.concatenate([c_prompt, c_sample], axis=0), ada_w, ada_b)
    sm = jax.nn.softmax(hg_lower.astype(F32), axis=0)
    lower = jnp.cumsum(sm, axis=0) - sm[0]
    rope_p = _rope_tables(jnp.arange(seq))
    rope_s = _rope_tables(jnp.full((bs,), PAST_LEN, jnp.int32))

    tm_p, tn_p = 1024, 512
    hg_p, hg_s, cv_p, cv_s = [], [], [], []
    sw_p, sw_s = None, None
    for i in range(DEPTH):
        kind, j = i % N_MIXERS, i // N_MIXERS
        mod_p = mod_all[i, :bp].reshape(bp, 1, 6 * d)
        mod_s = mod_all[i, bp:].reshape(1, bs, 6 * d)
        pp = dict(tm=tm_p, tn=tn_p)
        pp_in = dict(tm=tm_p, tn=1024)
        ps = dict(tm=bs, tn=1024)
        hp = _norm_mod(xp, norm_mix[i], mod_p, 0, 1, tm=512, rows_per_batch=seq)
        hs = _norm_mod(xs, norm_mix[i], mod_s, 0, 1, tm=bs, rows_per_batch=bs)
        if kind == 0:
            proj_p = _mm(hp, hg_w_in, j, **pp_in)
            proj_s = _mm(hs, hg_w_in, j, **ps)
            lhs_p, st_p = _gla_prompt(proj_p, lower[j], hg_norm[j], bp, seq)
            lhs_s, st_s = _gla_step(proj_s, lower[j], hg_norm[j], state_hgrn, j)
            hg_p.append(st_p)
            hg_s.append(st_s)
            w_out = hg_w_out
        elif kind == 1:
            proj_p = _mm(hp, cv_w_in, j, **pp_in)
            proj_s = _mm(hs, cv_w_in, j, **ps)
            lhs_p, tail_p = _conv_prompt(proj_p, cv_w_conv[j], bp, seq)
            lhs_s, tail_s = _conv_step(proj_s, cv_w_conv[j], state_conv[j])
            cv_p.append(tail_p)
            cv_s.append(tail_s)
            w_out = cv_w_out
        else:
            qkv_p = _mm(hp, sw_w_in, j, tm=512, tn=SW_WIDTH, rope=rope_p)
            qkv_s = _mm(hs, sw_w_in, j, rope=rope_s, **ps)
            os_, ls_ = [], []
            for gi, (win, dil) in enumerate(SW_GROUPS):
                o_g, l_g = _swa_prompt_group(qkv_p, gi, dil, bp, seq)
                os_.append(o_g)
                ls_.append(l_g)
            lhs_p = _swa_merge(os_, ls_)
            caches = (cache_swa_g1[j], cache_swa_g2[j], cache_swa_g3[j])
            lhs_s = _swa_step(qkv_s, caches)
            w_out = sw_w_out
            sw_p = [_kv_window(qkv_p, gi, win, bp, seq)[None] for gi, (win, _) in enumerate(SW_GROUPS)]
            sw_s = [c[None] for c in _cache_shift(qkv_s, caches)]
        xp = _mm(lhs_p, w_out, j, resgate=(xp, mod_p, 2, seq), **pp)
        xs = _mm(lhs_s, w_out, j, resgate=(xs, mod_s, 2, bs), **ps)
        xp, xs = _moe_layer(xp, xs, mod_p, mod_s, norm_ffn[i], moe_w_group[i], moe_b_group[i],
                            moe_w_router[i], moe_b_router[i], moe_w_gu, moe_w_down, i, seq)

    y_p = _final_norm(xp, norm_final, 512).reshape(bp, seq, d)
    y_s = _final_norm(xs, norm_final, bs).reshape(bs, 1, d)
    return (y_p, y_s, jnp.stack(hg_p), jnp.stack(hg_s), jnp.stack(cv_p), jnp.stack(cv_s),
            sw_p[0], sw_s[0], sw_p[1], sw_s[1], sw_p[2], sw_s[2])
```
